```python
import math
import jax, jax.numpy as jnp
from jax import lax
import numpy as np

D_MODEL = 2048
BATCH = 4
SEQ = 4096
DEPTH = 1

HEAD_DIM = 128
Q_BLOCK = 128
A_Q_HEADS = 8
A_KV_HEADS = 2
A_REP = A_Q_HEADS // A_KV_HEADS
ROPE_THETA = 10000.0
ROPE_AXIS_DIM = HEAD_DIM // 2
GRID_W = 64
B_WINDOWS = (128, 512, 2048)
B_DILATIONS = (1, 4, 16)
B_GROUPS = 3
B_HEADS_PER_GROUP = 4
B_HEADS = B_GROUPS * B_HEADS_PER_GROUP
B_SIDE = B_WINDOWS[0] // (2 * B_DILATIONS[0])
B_KEYS = 2 * B_SIDE + 1
A_Q_W = A_Q_HEADS * HEAD_DIM
A_KV_W = A_KV_HEADS * HEAD_DIM
B_W = B_HEADS * HEAD_DIM
IN_WIDTHS = (A_Q_W, A_KV_W, A_KV_W, B_W, B_W, B_W, D_MODEL, D_MODEL)
IN_TOTAL = A_Q_W + 2 * A_KV_W + 3 * B_W + 2 * D_MODEL
A_OUT_W = A_Q_W
B_OUT_W = B_HEADS_PER_GROUP * HEAD_DIM
FFN_HIDDEN = -(-8 * D_MODEL // (3 * 256)) * 256
ALPHA = (2.0 * DEPTH) ** 0.25
BETA = (8.0 * DEPTH) ** -0.25
RMS_EPS = 1e-6
LN_EPS = 1e-5

kernel_name = 'hybrid_gqa_axialrope_dilated_alibi_deepnorm_swiglu'


def _layer_norm(x, g, b):
    x32 = x.astype(jnp.float32)
    mu = jnp.mean(x32, axis=-1, keepdims=True)
    var = jnp.mean(jnp.square(x32 - mu), axis=-1, keepdims=True)
    y = (x32 - mu) * lax.rsqrt(var + LN_EPS) * g.astype(jnp.float32) + b.astype(jnp.float32)
    return y.astype(x.dtype)


def _rms_norm(x, g):
    x32 = x.astype(jnp.float32)
    y = x32 * lax.rsqrt(jnp.mean(jnp.square(x32), axis=-1, keepdims=True) + RMS_EPS)
    return (y * g.astype(jnp.float32)).astype(x.dtype)


def _axial_rope_tables(seq_len):
    rows = seq_len // GRID_W
    row_id = jnp.broadcast_to(jnp.arange(rows)[:, None], (rows, GRID_W)).reshape(-1)
    col_id = jnp.broadcast_to(jnp.arange(GRID_W)[None, :], (rows, GRID_W)).reshape(-1)
    n_freq = ROPE_AXIS_DIM // 2
    freqs = ROPE_THETA ** (-jnp.arange(n_freq, dtype=jnp.float32) / n_freq)
    ang_r = row_id.astype(jnp.float32)[:, None] * freqs[None, :]
    ang_c = col_id.astype(jnp.float32)[:, None] * freqs[None, :]
    return (jnp.cos(ang_r), jnp.sin(ang_r), jnp.cos(ang_c), jnp.sin(ang_c))


def _rotate_half(xh, cos, sin):
    x1, x2 = jnp.split(xh, 2, axis=-1)
    c = cos[None, :, None, :]
    s = sin[None, :, None, :]
    return jnp.concatenate([x1 * c - x2 * s, x2 * c + x1 * s], axis=-1)


def _apply_axial_rope(x, tabs):
    cos_r, sin_r, cos_c, sin_c = tabs
    xr = _rotate_half(x[..., :ROPE_AXIS_DIM], cos_r, sin_r)
    xc = _rotate_half(x[..., ROPE_AXIS_DIM:], cos_c, sin_c)
    return jnp.concatenate([xr, xc], axis=-1).astype(x.dtype)


def _gqa_blocks(q, k, v):
    bsz, seq_len = q.shape[0], q.shape[1]
    nb = seq_len // Q_BLOCK
    qb = q.reshape(bsz, nb, Q_BLOCK, A_KV_HEADS, A_REP, HEAD_DIM).swapaxes(0, 1)

    def one_block(qblk):
        s = jnp.einsum('bqgrd,bkgd->bgrqk', qblk, k, preferred_element_type=jnp.float32)
        p = jax.nn.softmax(s, axis=-1)
        return jnp.einsum('bgrqk,bkgd->bqgrd', p.astype(v.dtype), v)

    o = lax.map(one_block, qb)
    return o.swapaxes(0, 1).reshape(bsz, seq_len, A_OUT_W)


def _dilated_offsets():
    side = jnp.arange(-B_SIDE, B_SIDE + 1, dtype=jnp.int32)
    return jnp.stack([dil * side for dil in B_DILATIONS], axis=0)


def _alibi_slopes():
    i = jnp.arange(1, B_HEADS + 1, dtype=jnp.float32)
    return (2.0 ** (-8.0 * i / B_HEADS)).reshape(B_GROUPS, B_HEADS_PER_GROUP)


def _dilated_window_attention(q, k, v):
    bsz, seq_len = q.shape[0], q.shape[1]
    nb = seq_len // Q_BLOCK
    offs = _dilated_offsets()
    bias = -_alibi_slopes()[:, :, None] * jnp.abs(offs).astype(jnp.float32)[:, None, :]
    g_idx = jnp.arange(B_GROUPS)[None, :, None]

    def one_block(n):
        t = n * Q_BLOCK + jnp.arange(Q_BLOCK)
        pos = t[:, None, None] + offs[None, :, :]
        valid = (pos >= 0) & (pos < seq_len)
        pos_c = jnp.clip(pos, 0, seq_len - 1)
        qblk = lax.dynamic_slice_in_dim(q, n * Q_BLOCK, Q_BLOCK, axis=1)
        ks = k[:, pos_c, g_idx]
        vs = v[:, pos_c, g_idx]
        s = jnp.einsum('bqghd,bqgjhd->bqghj', qblk, ks, preferred_element_type=jnp.float32) + bias
        s = jnp.where(valid[None, :, :, None, :], s, -jnp.inf)
        lse = jax.nn.logsumexp(s, axis=-1, keepdims=True)
        p = jnp.exp(s - lse)
        o = jnp.einsum('bqghj,bqgjhd->bqghd', p.astype(v.dtype), vs)
        w = jax.nn.softmax(lse[..., 0], axis=2)
        return jnp.einsum('bqgh,bqghd->bqhd', w.astype(o.dtype), o)

    o = lax.map(one_block, jnp.arange(nb))
    return o.swapaxes(0, 1).reshape(bsz, seq_len, B_OUT_W)


def _hybrid_layer(x, w_in, b_gate, q_norm_a, k_norm_a, w_proj_a, w_proj_b, w_out,
                  ln1_g, ln1_b, w_ffn_gate, w_ffn_up, w_ffn_down, ln2_g, ln2_b, rope_tabs):
    bsz, seq_len, _ = x.shape
    scale = HEAD_DIM ** -0.5
    split_pts = [int(p) for p in np.cumsum(IN_WIDTHS)[:-1]]
    qa, ka, va, qb, kb, vb, ga, gb = jnp.split(x @ w_in, split_pts, axis=-1)
    qa = _apply_axial_rope(_rms_norm(qa.reshape(bsz, seq_len, A_Q_HEADS, HEAD_DIM), q_norm_a), rope_tabs)
    ka = _apply_axial_rope(_rms_norm(ka.reshape(bsz, seq_len, A_KV_HEADS, HEAD_DIM), k_norm_a), rope_tabs)
    va = va.reshape(bsz, seq_len, A_KV_HEADS, HEAD_DIM)
    out_a = _gqa_blocks(qa * scale, ka, va)
    shp = (bsz, seq_len, B_GROUPS, B_HEADS_PER_GROUP, HEAD_DIM)
    out_b = _dilated_window_attention(qb.reshape(shp) * scale, kb.reshape(shp), vb.reshape(shp))
    gate_a = jax.nn.sigmoid(ga + b_gate[0])
    gate_b = jax.nn.sigmoid(gb + b_gate[1])
    mixed = (gate_a * (out_a @ w_proj_a) + gate_b * (out_b @ w_proj_b)) @ w_out
    x = _layer_norm(ALPHA * x + mixed, ln1_g, ln1_b)
    h = jax.nn.silu(x @ w_ffn_gate) * (x @ w_ffn_up)
    x = _layer_norm(ALPHA * x + h @ w_ffn_down, ln2_g, ln2_b)
    return x


def setup_inputs(seed: int = 0) -> dict:
    key = jax.random.key(seed)
    ks = jax.random.split(key, 16)
    f32 = jnp.float32

    def nrm(k, shape, fan_in, s=1.0):
        return jax.random.normal(k, shape, f32) * (s * fan_in ** -0.5)

    col_scale = jnp.concatenate([jnp.full((w,), sc, f32) for w, sc in
                                 zip(IN_WIDTHS, (1.0, 1.0, BETA, 1.0, 1.0, BETA, 1.0, 1.0))])
    return {
        'x': jax.random.normal(ks[0], (BATCH, SEQ, D_MODEL), f32),
        'w_in': nrm(ks[1], (DEPTH, D_MODEL, IN_TOTAL), D_MODEL) * col_scale,
        'b_gate': 0.02 * jax.random.normal(ks[2], (DEPTH, 2, D_MODEL), f32),
        'q_norm_a': 1.0 + 0.02 * jax.random.normal(ks[3], (DEPTH, HEAD_DIM), f32),
        'k_norm_a': 1.0 + 0.02 * jax.random.normal(ks[4], (DEPTH, HEAD_DIM), f32),
        'w_proj_a': nrm(ks[5], (DEPTH, A_OUT_W, D_MODEL), A_OUT_W, BETA),
        'w_proj_b': nrm(ks[6], (DEPTH, B_OUT_W, D_MODEL), B_OUT_W, BETA),
        'w_out': nrm(ks[7], (DEPTH, D_MODEL, D_MODEL), D_MODEL, BETA),
        'ln1_g': 1.0 + 0.02 * jax.random.normal(ks[8], (DEPTH, D_MODEL), f32),
        'ln1_b': 0.02 * jax.random.normal(ks[9], (DEPTH, D_MODEL), f32),
        'w_ffn_gate': nrm(ks[10], (DEPTH, D_MODEL, FFN_HIDDEN), D_MODEL),
        'w_ffn_up': nrm(ks[11], (DEPTH, D_MODEL, FFN_HIDDEN), D_MODEL),
        'w_ffn_down': nrm(ks[12], (DEPTH, FFN_HIDDEN, D_MODEL), FFN_HIDDEN, BETA),
        'ln2_g': 1.0 + 0.02 * jax.random.normal(ks[13], (DEPTH, D_MODEL), f32),
        'ln2_b': 0.02 * jax.random.normal(ks[14], (DEPTH, D_MODEL), f32),
    }


def reference(x, w_in, b_gate, q_norm_a, k_norm_a, w_proj_a, w_proj_b, w_out,
              ln1_g, ln1_b, w_ffn_gate, w_ffn_up, w_ffn_down, ln2_g, ln2_b):
    rope_tabs = _axial_rope_tables(x.shape[1])
    for l in range(DEPTH):
        x = _hybrid_layer(x, w_in[l], b_gate[l], q_norm_a[l], k_norm_a[l], w_proj_a[l], w_proj_b[l],
                          w_out[l], ln1_g[l], ln1_b[l], w_ffn_gate[l], w_ffn_up[l], w_ffn_down[l],
                          ln2_g[l], ln2_b[l], rope_tabs)
    return x
```

```python
import functools

import jax
import jax.numpy as jnp
from jax import lax
from jax.experimental import pallas as pl
from jax.experimental.pallas import tpu as pltpu

F32 = jnp.float32
BF16 = jnp.bfloat16

D_MODEL = 2048
HEAD_DIM = 128
A_Q_HEADS = 8
A_KV_HEADS = 2
A_REP = A_Q_HEADS // A_KV_HEADS
ROPE_THETA = 10000.0
ROPE_AXIS_DIM = HEAD_DIM // 2
GRID_W = 64
B_DILATIONS = (1, 4, 16)
B_GROUPS = 3
B_HEADS_PER_GROUP = 4
B_HEADS = B_GROUPS * B_HEADS_PER_GROUP
B_SIDE = 64
A_Q_W = A_Q_HEADS * HEAD_DIM
A_KV_W = A_KV_HEADS * HEAD_DIM
B_W = B_HEADS * HEAD_DIM
B_OUT_W = B_HEADS_PER_GROUP * HEAD_DIM
FFN_HIDDEN = 5632
ALPHA = 2.0 ** 0.25
RMS_EPS = 1e-6
LN_EPS = 1e-5
Q_SCALE = HEAD_DIM ** -0.5

COL_QA = 0
COL_KA = A_Q_W
COL_VA = A_Q_W + A_KV_W
COL_QB = COL_VA + A_KV_W
COL_GA = COL_QB + 3 * B_W

VMEM_LIMIT = 60 * 1024 * 1024

NT_DIMS = (((1,), (1,)), ((), ()))


def _params(*sem):
    return pltpu.CompilerParams(dimension_semantics=sem, vmem_limit_bytes=VMEM_LIMIT)


def _layer_norm(z, g, b):
    mu = jnp.mean(z, axis=-1, keepdims=True)
    zc = z - mu
    var = jnp.mean(zc * zc, axis=-1, keepdims=True)
    return zc * lax.rsqrt(var + LN_EPS) * g + b


def _proj_norm_rope_kernel(x_ref, w_ref, gain_ref, cos_ref, sin_ref, o_ref):
    y = jnp.dot(x_ref[...], w_ref[...], preferred_element_type=F32)
    cos = cos_ref[...]
    sin = sin_ref[...]
    lane = lax.broadcasted_iota(jnp.int32, cos.shape, 1)
    first_half = (lane % ROPE_AXIS_DIM) < (ROPE_AXIS_DIM // 2)
    for h in range(y.shape[1] // HEAD_DIM):
        sl = slice(h * HEAD_DIM, (h + 1) * HEAD_DIM)
        yh = y[:, sl]
        ms = jnp.mean(yh * yh, axis=-1, keepdims=True)
        yh = yh * lax.rsqrt(ms + RMS_EPS) * gain_ref[:, sl]
        partner = jnp.where(first_half, pltpu.roll(yh, HEAD_DIM - 32, 1), pltpu.roll(yh, 32, 1))
        o_ref[:, sl] = (yh * cos + partner * sin).astype(o_ref.dtype)


def _proj_scale_kernel(x_ref, w_ref, cs_ref, o_ref):
    y = jnp.dot(x_ref[...], w_ref[...], preferred_element_type=F32)
    o_ref[...] = (y * cs_ref[...]).astype(o_ref.dtype)


def _proj_gate_kernel(x_ref, w_ref, b_ref, o_ref):
    y = jnp.dot(x_ref[...], w_ref[...], preferred_element_type=F32)
    o_ref[...] = (1.0 / (1.0 + jnp.exp(-(y + b_ref[...])))).astype(o_ref.dtype)


def _in_proj(kern, xb, w, col_off, n_cols, row_vecs, tabs=(), *, tm, tn, seq_len):
    t, d = xb.shape
    off = col_off // tn
    pos_blocks = seq_len // tm
    in_specs = [pl.BlockSpec((tm, d), lambda i, j: (i, 0)),
                pl.BlockSpec((d, tn), lambda i, j: (0, off + j))]
    in_specs += [pl.BlockSpec((1, tn), lambda i, j: (0, j)) for _ in row_vecs]
    in_specs += [pl.BlockSpec((tm, HEAD_DIM), lambda i, j: (i % pos_blocks, 0)) for _ in tabs]
    return pl.pallas_call(
        kern,
        grid=(t // tm, n_cols // tn),
        in_specs=in_specs,
        out_specs=pl.BlockSpec((tm, tn), lambda i, j: (i, j)),
        out_shape=jax.ShapeDtypeStruct((t, n_cols), BF16),
        compiler_params=_params("parallel", "arbitrary"),
    )(xb, w, *row_vecs, *tabs)


def _gqa_kernel(q_ref, k_ref, v_ref, o_ref, m_sc, l_sc, acc_sc, *, tk):
    tq = q_ref.shape[1]
    seq_len = k_ref.shape[1]
    q = jnp.concatenate([q_ref[0, :, r * HEAD_DIM:(r + 1) * HEAD_DIM] for r in range(A_REP)], axis=0)
    m_sc[...] = jnp.full(m_sc.shape, -jnp.inf, F32)
    l_sc[...] = jnp.zeros(l_sc.shape, F32)
    acc_sc[...] = jnp.zeros(acc_sc.shape, F32)

    def body(c, carry):
        k0 = pl.multiple_of(c * tk, tk)
        k = k_ref[0, pl.ds(k0, tk), :]
        v = v_ref[0, pl.ds(k0, tk), :]
        s = lax.dot_general(q, k, NT_DIMS, preferred_element_type=F32)
        m_prev = m_sc[...]
        m_new = jnp.maximum(m_prev, jnp.max(s, axis=-1, keepdims=True))
        alpha = jnp.exp(m_prev - m_new)
        p = jnp.exp(s - m_new)
        l_sc[...] = alpha * l_sc[...] + jnp.sum(p, axis=-1, keepdims=True)
        acc_sc[...] = alpha * acc_sc[...] + jnp.dot(p.astype(BF16), v, preferred_element_type=F32)
        m_sc[...] = m_new
        return carry

    lax.fori_loop(0, seq_len // tk, body, 0)
    o = acc_sc[...] / l_sc[...]
    for r in range(A_REP):
        o_ref[0, :, r * HEAD_DIM:(r + 1) * HEAD_DIM] = o[r * tq:(r + 1) * tq, :].astype(o_ref.dtype)


def _gqa(qk, va, *, tq, tk):
    b, s, _ = qk.shape
    gw = A_REP * HEAD_DIM
    rows = A_REP * tq
    return pl.pallas_call(
        functools.partial(_gqa_kernel, tk=tk),
        grid=(b, A_KV_HEADS, s // tq),
        in_specs=[pl.BlockSpec((1, tq, gw), lambda bi, g, i: (bi, i, g)),
                  pl.BlockSpec((1, s, HEAD_DIM), lambda bi, g, i: (bi, 0, A_Q_HEADS + g)),
                  pl.BlockSpec((1, s, HEAD_DIM), lambda bi, g, i: (bi, 0, g))],
        out_specs=pl.BlockSpec((1, tq, gw), lambda bi, g, i: (bi, i, g)),
        out_shape=jax.ShapeDtypeStruct((b, s, A_Q_W), BF16),
        scratch_shapes=[pltpu.VMEM((rows, 1), F32), pltpu.VMEM((rows, 1), F32),
                        pltpu.VMEM((rows, HEAD_DIM), F32)],
        compiler_params=_params("parallel", "parallel", "arbitrary"),
    )(qk, qk, va)


def _dilated_kernel(q_ref, k_ref, v_ref, o_ref, lse_ref, *, dil, slopes):
    qb = 2 * B_SIDE
    kw = 4 * B_SIDE
    n_rows = q_ref.shape[1]
    row = lax.broadcasted_iota(jnp.int32, (qb, kw), 0)
    col = lax.broadcasted_iota(jnp.int32, (qb, kw), 1)
    rel0 = col - row

    def blk(i, carry):
        q0 = pl.multiple_of(i * qb, qb)
        ks = pl.multiple_of(jnp.clip(q0 - B_SIDE, 0, n_rows - kw), B_SIDE)
        dist = jnp.abs(rel0 + (ks - q0))
        valid = dist <= B_SIDE
        token_dist = dist.astype(F32) * float(dil)
        for h in range(B_HEADS_PER_GROUP):
            sl = slice(h * HEAD_DIM, (h + 1) * HEAD_DIM)
            q = q_ref[0, pl.ds(q0, qb), sl]
            k = k_ref[0, pl.ds(ks, kw), sl]
            v = v_ref[0, pl.ds(ks, kw), sl]
            s = lax.dot_general(q, k, NT_DIMS, preferred_element_type=F32) - slopes[h] * token_dist
            s = jnp.where(valid, s, -jnp.inf)
            m = jnp.max(s, axis=-1, keepdims=True)
            p = jnp.exp(s - m)
            l = jnp.sum(p, axis=-1, keepdims=True)
            o = jnp.dot(p.astype(BF16), v, preferred_element_type=F32) / l
            o_ref[0, pl.ds(q0, qb), sl] = o
            lse_ref[0, pl.ds(q0, qb), sl] = jnp.broadcast_to(m + jnp.log(l), (qb, HEAD_DIM))
        return carry

    lax.fori_loop(0, n_rows // qb, blk, 0)


def _dilated_group(qkv_b, g):
    b, s, w = qkv_b.shape
    dil = B_DILATIONS[g]
    n_rows = s // dil
    assert n_rows >= 4 * B_SIDE and n_rows % (2 * B_SIDE) == 0
    blocks_per_row = w // B_OUT_W
    view = qkv_b.reshape(b, n_rows, dil * w)
    slopes = tuple(2.0 ** (-8.0 * (g * B_HEADS_PER_GROUP + h + 1) / B_HEADS) for h in range(B_HEADS_PER_GROUP))

    def spec(which):
        return pl.BlockSpec((1, n_rows, B_OUT_W), lambda bi, r: (bi, 0, r * blocks_per_row + which * B_GROUPS + g))

    out_spec = pl.BlockSpec((1, n_rows, B_OUT_W), lambda bi, r: (bi, 0, r))
    o, lse = pl.pallas_call(
        functools.partial(_dilated_kernel, dil=dil, slopes=slopes),
        grid=(b, dil),
        in_specs=[spec(0), spec(1), spec(2)],
        out_specs=[out_spec, out_spec],
        out_shape=[jax.ShapeDtypeStruct((b, n_rows, dil * B_OUT_W), F32)] * 2,
        compiler_params=_params("parallel", "parallel"),
    )(view, view, view)
    return o.reshape(b * s, B_OUT_W), lse.reshape(b * s, B_OUT_W)


def _mix_kernel(oa_ref, o0_ref, o1_ref, o2_ref, l0_ref, l1_ref, l2_ref, ga_ref, gb_ref, wpa_ref, wpb_ref, out_ref):
    l0, l1, l2 = l0_ref[...], l1_ref[...], l2_ref[...]
    mx = jnp.maximum(jnp.maximum(l0, l1), l2)
    e0, e1, e2 = jnp.exp(l0 - mx), jnp.exp(l1 - mx), jnp.exp(l2 - mx)
    ob = (e0 * o0_ref[...] + e1 * o1_ref[...] + e2 * o2_ref[...]) / (e0 + e1 + e2)
    pa = jnp.dot(oa_ref[...], wpa_ref[...], preferred_element_type=F32)
    pb = jnp.dot(ob.astype(BF16), wpb_ref[...], preferred_element_type=F32)
    out_ref[...] = (ga_ref[...].astype(F32) * pa + gb_ref[...].astype(F32) * pb).astype(out_ref.dtype)


def _mix(out_a, o_groups, lse_groups, gates, wpa, wpb, *, tm):
    t = out_a.shape[0]
    row = lambda w, cb=0: pl.BlockSpec((tm, w), lambda i: (i, cb))
    full = lambda a: pl.BlockSpec(a.shape, lambda i: (0, 0))
    return pl.pallas_call(
        _mix_kernel,
        grid=(t // tm,),
        in_specs=[row(A_Q_W)] + [row(B_OUT_W)] * 6 + [row(D_MODEL, 0), row(D_MODEL, 1), full(wpa), full(wpb)],
        out_specs=row(D_MODEL),
        out_shape=jax.ShapeDtypeStruct((t, D_MODEL), BF16),
        compiler_params=_params("parallel"),
    )(out_a, *o_groups, *lse_groups, gates, gates, wpa, wpb)


def _out_ln_kernel(mixed_ref, x_ref, w_ref, g_ref, b_ref, o_ref):
    y = jnp.dot(mixed_ref[...], w_ref[...], preferred_element_type=F32)
    o_ref[...] = _layer_norm(ALPHA * x_ref[...] + y, g_ref[...], b_ref[...])


def _out_ln(mixed, x2d, w_out, g, b, *, tm):
    t, d = x2d.shape
    row = pl.BlockSpec((tm, d), lambda i: (i, 0))
    vec = pl.BlockSpec((1, d), lambda i: (0, 0))
    return pl.pallas_call(
        _out_ln_kernel,
        grid=(t // tm,),
        in_specs=[row, row, pl.BlockSpec((d, d), lambda i: (0, 0)), vec, vec],
        out_specs=row,
        out_shape=jax.ShapeDtypeStruct((t, d), F32),
        compiler_params=_params("parallel"),
    )(mixed, x2d, w_out, g, b)


def _ffn_kernel(x_ref, wg_ref, wu_ref, wd_ref, g_ref, b_ref, o_ref, xb_sc, acc_sc):
    f = pl.program_id(1)

    @pl.when(f == 0)
    def _():
        xb_sc[...] = x_ref[...].astype(BF16)
        acc_sc[...] = jnp.zeros(acc_sc.shape, F32)

    xb = xb_sc[...]
    gate = jnp.dot(xb, wg_ref[...], preferred_element_type=F32)
    up = jnp.dot(xb, wu_ref[...], preferred_element_type=F32)
    h = gate * (1.0 / (1.0 + jnp.exp(-gate))) * up
    acc_sc[...] += jnp.dot(h.astype(BF16), wd_ref[...], preferred_element_type=F32)

    @pl.when(f == pl.num_programs(1) - 1)
    def _():
        o_ref[...] = _layer_norm(ALPHA * x_ref[...] + acc_sc[...], g_ref[...], b_ref[...])


def _ffn(x1, wg, wu, wd, g, b, *, tm, tf):
    t, d = x1.shape
    hidden = wg.shape[1]
    row = pl.BlockSpec((tm, d), lambda i, f: (i, 0))
    vec = pl.BlockSpec((1, d), lambda i, f: (0, 0))
    return pl.pallas_call(
        _ffn_kernel,
        grid=(t // tm, hidden // tf),
        in_specs=[row,
                  pl.BlockSpec((d, tf), lambda i, f: (0, f)),
                  pl.BlockSpec((d, tf), lambda i, f: (0, f)),
                  pl.BlockSpec((tf, d), lambda i, f: (f, 0)),
                  vec, vec],
        out_specs=row,
        out_shape=jax.ShapeDtypeStruct((t, d), F32),
        scratch_shapes=[pltpu.VMEM((tm, d), BF16), pltpu.VMEM((tm, d), F32)],
        compiler_params=_params("parallel", "arbitrary"),
    )(x1, wg, wu, wd, g, b)


def _rope_tables(seq_len):
    pos = jnp.arange(seq_len)
    n_freq = ROPE_AXIS_DIM // 2
    freqs = ROPE_THETA ** (-jnp.arange(n_freq, dtype=F32) / n_freq)
    ang_r = (pos // GRID_W).astype(F32)[:, None] * freqs[None, :]
    ang_c = (pos % GRID_W).astype(F32)[:, None] * freqs[None, :]
    cos = jnp.concatenate([jnp.cos(ang_r)] * 2 + [jnp.cos(ang_c)] * 2, axis=-1)
    sin = jnp.concatenate([-jnp.sin(ang_r), jnp.sin(ang_r), -jnp.sin(ang_c), jnp.sin(ang_c)], axis=-1)
    return cos, sin


def _layer(x, w_in, b_gate, q_norm_a, k_norm_a, w_proj_a, w_proj_b, w_out,
           ln1_g, ln1_b, w_ffn_gate, w_ffn_up, w_ffn_down, ln2_g, ln2_b, rope):
    bsz, seq_len, d = x.shape
    t = bsz * seq_len
    x2d = x.reshape(t, d)
    xb = x2d.astype(BF16)
    w_in_b = w_in.astype(BF16)
    vec = lambda a: a.reshape(1, -1).astype(F32)

    gains = jnp.concatenate([jnp.tile(q_norm_a * Q_SCALE, A_Q_HEADS), jnp.tile(k_norm_a, A_KV_HEADS)])
    col_scale = jnp.concatenate([jnp.full((B_W,), Q_SCALE, F32), jnp.ones((2 * B_W,), F32)])
    proj = functools.partial(_in_proj, tm=1024, seq_len=seq_len)
    qk = proj(_proj_norm_rope_kernel, xb, w_in_b, COL_QA, A_Q_W + A_KV_W, [vec(gains)], rope, tn=256)
    va = proj(_proj_scale_kernel, xb, w_in_b, COL_VA, A_KV_W, [jnp.ones((1, A_KV_W), F32)], tn=256)
    qkv_b = proj(_proj_scale_kernel, xb, w_in_b, COL_QB, 3 * B_W, [vec(col_scale)], tn=512)
    gates = proj(_proj_gate_kernel, xb, w_in_b, COL_GA, 2 * D_MODEL, [vec(b_gate)], tn=512)

    out_a = _gqa(qk.reshape(bsz, seq_len, -1), va.reshape(bsz, seq_len, -1), tq=256, tk=512)
    groups = [_dilated_group(qkv_b.reshape(bsz, seq_len, -1), g) for g in range(B_GROUPS)]
    mixed = _mix(out_a.reshape(t, A_Q_W), [o for o, _ in groups], [l for _, l in groups], gates,
                 w_proj_a.astype(BF16), w_proj_b.astype(BF16), tm=512)
    x1 = _out_ln(mixed, x2d, w_out.astype(BF16), vec(ln1_g), vec(ln1_b), tm=512)
    x2 = _ffn(x1, w_ffn_gate.astype(BF16), w_ffn_up.astype(BF16), w_ffn_down.astype(BF16),
              vec(ln2_g), vec(ln2_b), tm=512, tf=512)
    return x2.reshape(bsz, seq_len, d)


def kernel(x, w_in, b_gate, q_norm_a, k_norm_a, w_proj_a, w_proj_b, w_out, ln1_g, ln1_b, w_ffn_gate, w_ffn_up, w_ffn_down, ln2_g, ln2_b):
    rope = _rope_tables(x.shape[1])
    for l in range(w_in.shape[0]):
        x = _layer(x, w_in[l], b_gate[l], q_norm_a[l], k_norm_a[l], w_proj_a[l], w_proj_b[l], w_out[l],
                   ln1_g[l], ln1_b[l], w_ffn_gate[l], w_ffn_up[l], w_ffn_down[l], ln2_g[l], ln2_b[l], rope)
    return x
```

```python
import functools

import jax
import jax.numpy as jnp
from jax import lax
from jax.experimental import pallas as pl
from jax.experimental.pallas import tpu as pltpu

F32 = jnp.float32
BF16 = jnp.bfloat16

D_MODEL = 2048
HEAD_DIM = 128
A_Q_HEADS = 8
A_KV_HEADS = 2
A_REP = A_Q_HEADS // A_KV_HEADS
ROPE_THETA = 10000.0
ROPE_AXIS_DIM = HEAD_DIM // 2
GRID_W = 64
B_DILATIONS = (1, 4, 16)
B_GROUPS = 3
B_HEADS_PER_GROUP = 4
B_HEADS = B_GROUPS * B_HEADS_PER_GROUP
B_SIDE = 64
A_Q_W = A_Q_HEADS * HEAD_DIM
A_KV_W = A_KV_HEADS * HEAD_DIM
B_W = B_HEADS * HEAD_DIM
B_OUT_W = B_HEADS_PER_GROUP * HEAD_DIM
FFN_HIDDEN = 5632
ALPHA = 2.0 ** 0.25
RMS_EPS = 1e-6
LN_EPS = 1e-5
Q_SCALE = HEAD_DIM ** -0.5

COL_QA = 0
COL_KA = A_Q_W
COL_VA = A_Q_W + A_KV_W
COL_QB = COL_VA + A_KV_W
COL_GA = COL_QB + 3 * B_W

VMEM_LIMIT = 60 * 1024 * 1024

NT_DIMS = (((1,), (1,)), ((), ()))
TN_DIMS = (((0,), (0,)), ((), ()))
LANES = 128
LOG2_E = 1.4426950408889634


def _params(*sem):
    return pltpu.CompilerParams(dimension_semantics=sem, vmem_limit_bytes=VMEM_LIMIT)


def _layer_norm(z, g, b):
    mu = jnp.mean(z, axis=-1, keepdims=True)
    zc = z - mu
    var = jnp.mean(zc * zc, axis=-1, keepdims=True)
    return zc * lax.rsqrt(var + LN_EPS) * g + b


def _proj_norm_rope_kernel(x_ref, w_ref, gain_ref, cos_ref, sin_ref, o_ref):
    y = jnp.dot(x_ref[...], w_ref[...], preferred_element_type=F32)
    cos = cos_ref[...]
    sin = sin_ref[...]
    lane = lax.broadcasted_iota(jnp.int32, cos.shape, 1)
    first_half = (lane % ROPE_AXIS_DIM) < (ROPE_AXIS_DIM // 2)
    for h in range(y.shape[1] // HEAD_DIM):
        sl = slice(h * HEAD_DIM, (h + 1) * HEAD_DIM)
        yh = y[:, sl]
        ms = jnp.mean(yh * yh, axis=-1, keepdims=True)
        yh = yh * lax.rsqrt(ms + RMS_EPS) * gain_ref[:, sl]
        partner = jnp.where(first_half, pltpu.roll(yh, HEAD_DIM - 32, 1), pltpu.roll(yh, 32, 1))
        o_ref[:, sl] = (yh * cos + partner * sin).astype(o_ref.dtype)


def _proj_scale_kernel(x_ref, w_ref, cs_ref, o_ref):
    y = jnp.dot(x_ref[...], w_ref[...], preferred_element_type=F32)
    o_ref[...] = (y * cs_ref[...]).astype(o_ref.dtype)


def _proj_gate_kernel(x_ref, w_ref, b_ref, o_ref):
    y = jnp.dot(x_ref[...], w_ref[...], preferred_element_type=F32)
    o_ref[...] = (1.0 / (1.0 + jnp.exp(-(y + b_ref[...])))).astype(o_ref.dtype)


def _in_proj(kern, xb, w, col_off, n_cols, row_vecs, tabs=(), *, tm, tn, seq_len):
    t, d = xb.shape
    off = col_off // tn
    pos_blocks = seq_len // tm
    in_specs = [pl.BlockSpec((tm, d), lambda i, j: (i, 0)),
                pl.BlockSpec((d, tn), lambda i, j: (0, off + j))]
    in_specs += [pl.BlockSpec((1, tn), lambda i, j: (0, j)) for _ in row_vecs]
    in_specs += [pl.BlockSpec((tm, HEAD_DIM), lambda i, j: (i % pos_blocks, 0)) for _ in tabs]
    return pl.pallas_call(
        kern,
        grid=(t // tm, n_cols // tn),
        in_specs=in_specs,
        out_specs=pl.BlockSpec((tm, tn), lambda i, j: (i, j)),
        out_shape=jax.ShapeDtypeStruct((t, n_cols), BF16),
        compiler_params=_params("parallel", "arbitrary"),
    )(xb, w, *row_vecs, *tabs)


def _gqa_kernel(q_ref, k_ref, v_ref, o_ref, m_sc, l_sc, acc_sc, s0_sc, s1_sc, *, tk):
    tq = q_ref.shape[1]
    n_chunks = k_ref.shape[1] // tk
    s_sc = (s0_sc, s1_sc)
    q = jnp.concatenate([q_ref[0, :, r * HEAD_DIM:(r + 1) * HEAD_DIM] for r in range(A_REP)], axis=0)
    m_sc[...] = jnp.full(m_sc.shape, -jnp.inf, F32)
    l_sc[...] = jnp.zeros(l_sc.shape, F32)
    acc_sc[...] = jnp.zeros(acc_sc.shape, F32)

    def scores(c, slot):
        k0 = pl.multiple_of(c * tk, tk)
        s_sc[slot][...] = lax.dot_general(k_ref[0, pl.ds(k0, tk), :], q, NT_DIMS, preferred_element_type=F32)

    def softmax_pv(c, slot):
        k0 = pl.multiple_of(c * tk, tk)
        v = v_ref[0, pl.ds(k0, tk), :]
        st = s_sc[slot][...]
        m_prev = m_sc[...]
        m_new = jnp.maximum(m_prev, jnp.max(st, axis=0, keepdims=True))
        alpha = jnp.exp2(m_prev - m_new)
        pt = jnp.exp2(st - m_new)
        l_sc[...] = alpha * l_sc[...] + jnp.sum(pt, axis=0, keepdims=True)
        pv = lax.dot_general(v, pt.astype(BF16), TN_DIMS, preferred_element_type=F32)
        acc_sc[...] = alpha * acc_sc[...] + pv
        m_sc[...] = m_new

    scores(0, 0)

    def body(c2, carry):
        c = 2 * c2
        scores(c + 1, 1)
        softmax_pv(c, 0)
        scores(jnp.minimum(c + 2, n_chunks - 1), 0)
        softmax_pv(c + 1, 1)
        return carry

    lax.fori_loop(0, n_chunks // 2, body, 0)
    ot = acc_sc[...] / l_sc[...]
    for r in range(A_REP):
        o_ref[0, :, r * HEAD_DIM:(r + 1) * HEAD_DIM] = ot[:, r * tq:(r + 1) * tq].T.astype(o_ref.dtype)


def _gqa(qk, va, *, tq, tk):
    b, s, _ = qk.shape
    assert (s // tk) % 2 == 0
    gw = A_REP * HEAD_DIM
    rows = A_REP * tq
    return pl.pallas_call(
        functools.partial(_gqa_kernel, tk=tk),
        grid=(b, A_KV_HEADS, s // tq),
        in_specs=[pl.BlockSpec((1, tq, gw), lambda bi, g, i: (bi, i, g)),
                  pl.BlockSpec((1, s, HEAD_DIM), lambda bi, g, i: (bi, 0, A_Q_HEADS + g)),
                  pl.BlockSpec((1, s, HEAD_DIM), lambda bi, g, i: (bi, 0, g))],
        out_specs=pl.BlockSpec((1, tq, gw), lambda bi, g, i: (bi, i, g)),
        out_shape=jax.ShapeDtypeStruct((b, s, A_Q_W), BF16),
        scratch_shapes=[pltpu.VMEM((1, rows), F32), pltpu.VMEM((1, rows), F32), pltpu.VMEM((HEAD_DIM, rows), F32),
                        pltpu.VMEM((tk, rows), F32), pltpu.VMEM((tk, rows), F32)],
        compiler_params=_params("parallel", "parallel", "arbitrary"),
    )(qk, qk, va)


def _proj_deint_kernel(x_ref, w_ref, o_ref, y_sc, *, dil):
    y = jnp.dot(x_ref[...], w_ref[...], preferred_element_type=F32)
    y = y * jnp.where(pl.program_id(1) == 0, Q_SCALE, 1.0)
    tm, tn = y.shape
    if dil == 1:
        o_ref[0, 0] = y.astype(o_ref.dtype)
        return
    for s in range(tn // LANES):
        y_sc[s] = y[:, s * LANES:(s + 1) * LANES]
    for r in range(dil):
        for s in range(tn // LANES):
            o_ref[0, r, :, s * LANES:(s + 1) * LANES] = y_sc[s, pl.ds(r, tm // dil, stride=dil), :].astype(o_ref.dtype)


def _proj_deint(xb, w, g, *, tm, bsz, seq_len):
    t, d = xb.shape
    dil = B_DILATIONS[g]
    tn = B_OUT_W
    tiles_per_seq = seq_len // tm
    col0 = COL_QB // tn
    return pl.pallas_call(
        functools.partial(_proj_deint_kernel, dil=dil),
        grid=(t // tm, 3),
        in_specs=[pl.BlockSpec((tm, d), lambda i, j: (i, 0)),
                  pl.BlockSpec((d, tn), lambda i, j: (0, col0 + B_GROUPS * j + g))],
        out_specs=pl.BlockSpec((1, dil, tm // dil, tn), lambda i, j: (i // tiles_per_seq, 0, i % tiles_per_seq, j)),
        out_shape=jax.ShapeDtypeStruct((bsz, dil, seq_len // dil, 3 * tn), BF16),
        scratch_shapes=[pltpu.VMEM((tn // LANES, tm, LANES), F32)],
        compiler_params=_params("parallel", "arbitrary"),
    )(xb, w)


def _dilated_kernel(q_ref, k_ref, v_ref, o_ref, lse_ref, *token_order_scratch, dil, slopes):
    qb = 2 * B_SIDE
    kw = 4 * B_SIDE
    n_rows = k_ref.shape[2]
    row = lax.broadcasted_iota(jnp.int32, (qb, kw), 0)
    col = lax.broadcasted_iota(jnp.int32, (qb, kw), 1)
    q0 = pl.program_id(1) * qb
    ks = pl.multiple_of(jnp.clip(q0 - B_SIDE, 0, n_rows - kw), B_SIDE)
    dist = jnp.abs(col - row + (ks - q0))
    valid = dist <= B_SIDE
    token_dist = dist.astype(F32) * float(dil)

    def residue(r, carry):
        for h in range(B_HEADS_PER_GROUP):
            sl = slice(h * HEAD_DIM, (h + 1) * HEAD_DIM)
            q = q_ref[0, r, :, sl]
            k = k_ref[0, r, pl.ds(ks, kw), sl]
            v = v_ref[0, r, pl.ds(ks, kw), sl]
            s = lax.dot_general(q, k, NT_DIMS, preferred_element_type=F32) - slopes[h] * token_dist
            s = jnp.where(valid, s, -jnp.inf)
            m = jnp.max(s, axis=-1, keepdims=True)
            p = jnp.exp(s - m)
            l = jnp.sum(p, axis=-1, keepdims=True)
            o = jnp.dot(p.astype(BF16), v, preferred_element_type=F32) / l
            lse = jnp.broadcast_to(m + jnp.log(l), (qb, HEAD_DIM))
            if dil == 1:
                o_ref[0, :, sl] = o
                lse_ref[0, :, sl] = lse
            else:
                o_sc[h, pl.ds(r, qb, stride=dil), :] = o
                l_sc[h, pl.ds(r, qb, stride=dil), :] = lse
        return carry

    if dil == 1:
        residue(0, 0)
    else:
        o_sc, l_sc = token_order_scratch
        lax.fori_loop(0, dil, residue, 0)
        for h in range(B_HEADS_PER_GROUP):
            sl = slice(h * HEAD_DIM, (h + 1) * HEAD_DIM)
            o_ref[0, :, sl] = o_sc[h]
            lse_ref[0, :, sl] = l_sc[h]


def _dilated_group(qkv, g):
    b, dil, n_rows, _ = qkv.shape
    s = n_rows * dil
    qb = 2 * B_SIDE
    assert n_rows >= 4 * B_SIDE and n_rows % qb == 0
    slopes = tuple(2.0 ** (-8.0 * (g * B_HEADS_PER_GROUP + h + 1) / B_HEADS) for h in range(B_HEADS_PER_GROUP))
    kv_spec = lambda which: pl.BlockSpec((1, dil, n_rows, B_OUT_W), lambda bi, i: (bi, 0, 0, which))
    out_spec = pl.BlockSpec((1, qb * dil, B_OUT_W), lambda bi, i: (bi, i, 0))
    o, lse = pl.pallas_call(
        functools.partial(_dilated_kernel, dil=dil, slopes=slopes),
        grid=(b, n_rows // qb),
        in_specs=[pl.BlockSpec((1, dil, qb, B_OUT_W), lambda bi, i: (bi, 0, i, 0)), kv_spec(1), kv_spec(2)],
        out_specs=[out_spec, out_spec],
        out_shape=[jax.ShapeDtypeStruct((b, s, B_OUT_W), F32)] * 2,
        scratch_shapes=[pltpu.VMEM((B_HEADS_PER_GROUP, qb * dil, LANES), F32)] * (2 if dil > 1 else 0),
        compiler_params=_params("parallel", "arbitrary"),
    )(qkv, qkv, qkv)
    return o.reshape(b * s, B_OUT_W), lse.reshape(b * s, B_OUT_W)


def _mix_kernel(oa_ref, o0_ref, o1_ref, o2_ref, l0_ref, l1_ref, l2_ref, ga_ref, gb_ref, wpa_ref, wpb_ref, out_ref):
    l0, l1, l2 = l0_ref[...], l1_ref[...], l2_ref[...]
    mx = jnp.maximum(jnp.maximum(l0, l1), l2)
    e0, e1, e2 = jnp.exp(l0 - mx), jnp.exp(l1 - mx), jnp.exp(l2 - mx)
    ob = (e0 * o0_ref[...] + e1 * o1_ref[...] + e2 * o2_ref[...]) / (e0 + e1 + e2)
    pa = jnp.dot(oa_ref[...], wpa_ref[...], preferred_element_type=F32)
    pb = jnp.dot(ob.astype(BF16), wpb_ref[...], preferred_element_type=F32)
    out_ref[...] = (ga_ref[...].astype(F32) * pa + gb_ref[...].astype(F32) * pb).astype(out_ref.dtype)


def _mix(out_a, o_groups, lse_groups, gates, wpa, wpb, *, tm):
    t = out_a.shape[0]
    row = lambda w, cb=0: pl.BlockSpec((tm, w), lambda i: (i, cb))
    full = lambda a: pl.BlockSpec(a.shape, lambda i: (0, 0))
    return pl.pallas_call(
        _mix_kernel,
        grid=(t // tm,),
        in_specs=[row(A_Q_W)] + [row(B_OUT_W)] * 6 + [row(D_MODEL, 0), row(D_MODEL, 1), full(wpa), full(wpb)],
        out_specs=row(D_MODEL),
        out_shape=jax.ShapeDtypeStruct((t, D_MODEL), BF16),
        compiler_params=_params("parallel"),
    )(out_a, *o_groups, *lse_groups, gates, gates, wpa, wpb)


def _out_ln_kernel(mixed_ref, x_ref, w_ref, g_ref, b_ref, o_ref):
    y = jnp.dot(mixed_ref[...], w_ref[...], preferred_element_type=F32)
    o_ref[...] = _layer_norm(ALPHA * x_ref[...] + y, g_ref[...], b_ref[...])


def _out_ln(mixed, x2d, w_out, g, b, *, tm):
    t, d = x2d.shape
    row = pl.BlockSpec((tm, d), lambda i: (i, 0))
    vec = pl.BlockSpec((1, d), lambda i: (0, 0))
    return pl.pallas_call(
        _out_ln_kernel,
        grid=(t // tm,),
        in_specs=[row, row, pl.BlockSpec((d, d), lambda i: (0, 0)), vec, vec],
        out_specs=row,
        out_shape=jax.ShapeDtypeStruct((t, d), F32),
        compiler_params=_params("parallel"),
    )(mixed, x2d, w_out, g, b)


def _ffn_kernel(x_ref, wg_ref, wu_ref, wd_ref, g_ref, b_ref, o_ref, xb_sc, acc_sc):
    f = pl.program_id(1)

    @pl.when(f == 0)
    def _():
        xb_sc[...] = x_ref[...].astype(BF16)
        acc_sc[...] = jnp.zeros(acc_sc.shape, F32)

    xb = xb_sc[...]
    gate = jnp.dot(xb, wg_ref[...], preferred_element_type=F32)
    up = jnp.dot(xb, wu_ref[...], preferred_element_type=F32)
    h = gate * (1.0 / (1.0 + jnp.exp(-gate))) * up
    acc_sc[...] += jnp.dot(h.astype(BF16), wd_ref[...], preferred_element_type=F32)

    @pl.when(f == pl.num_programs(1) - 1)
    def _():
        o_ref[...] = _layer_norm(ALPHA * x_ref[...] + acc_sc[...], g_ref[...], b_ref[...])


def _ffn(x1, wg, wu, wd, g, b, *, tm, tf):
    t, d = x1.shape
    hidden = wg.shape[1]
    row = pl.BlockSpec((tm, d), lambda i, f: (i, 0))
    vec = pl.BlockSpec((1, d), lambda i, f: (0, 0))
    return pl.pallas_call(
        _ffn_kernel,
        grid=(t // tm, hidden // tf),
        in_specs=[row,
                  pl.BlockSpec((d, tf), lambda i, f: (0, f)),
                  pl.BlockSpec((d, tf), lambda i, f: (0, f)),
                  pl.BlockSpec((tf, d), lambda i, f: (f, 0)),
                  vec, vec],
        out_specs=row,
        out_shape=jax.ShapeDtypeStruct((t, d), F32),
        scratch_shapes=[pltpu.VMEM((tm, d), BF16), pltpu.VMEM((tm, d), F32)],
        compiler_params=_params("parallel", "arbitrary"),
    )(x1, wg, wu, wd, g, b)


def _rope_tables(seq_len):
    pos = jnp.arange(seq_len)
    n_freq = ROPE_AXIS_DIM // 2
    freqs = ROPE_THETA ** (-jnp.arange(n_freq, dtype=F32) / n_freq)
    ang_r = (pos // GRID_W).astype(F32)[:, None] * freqs[None, :]
    ang_c = (pos % GRID_W).astype(F32)[:, None] * freqs[None, :]
    cos = jnp.concatenate([jnp.cos(ang_r)] * 2 + [jnp.cos(ang_c)] * 2, axis=-1)
    sin = jnp.concatenate([-jnp.sin(ang_r), jnp.sin(ang_r), -jnp.sin(ang_c), jnp.sin(ang_c)], axis=-1)
    return cos, sin


def _layer(x, w_in, b_gate, q_norm_a, k_norm_a, w_proj_a, w_proj_b, w_out,
           ln1_g, ln1_b, w_ffn_gate, w_ffn_up, w_ffn_down, ln2_g, ln2_b, rope):
    bsz, seq_len, d = x.shape
    t = bsz * seq_len
    x2d = x.reshape(t, d)
    xb = x2d.astype(BF16)
    w_in_b = w_in.astype(BF16)
    vec = lambda a: a.reshape(1, -1).astype(F32)

    gains = jnp.concatenate([jnp.tile(q_norm_a * (Q_SCALE * LOG2_E), A_Q_HEADS), jnp.tile(k_norm_a, A_KV_HEADS)])
    proj = functools.partial(_in_proj, tm=1024, seq_len=seq_len)
    qk = proj(_proj_norm_rope_kernel, xb, w_in_b, COL_QA, A_Q_W + A_KV_W, [vec(gains)], rope, tn=256)
    va = proj(_proj_scale_kernel, xb, w_in_b, COL_VA, A_KV_W, [jnp.ones((1, A_KV_W), F32)], tn=256)
    gates = proj(_proj_gate_kernel, xb, w_in_b, COL_GA, 2 * D_MODEL, [vec(b_gate)], tn=512)

    out_a = _gqa(qk.reshape(bsz, seq_len, -1), va.reshape(bsz, seq_len, -1), tq=256, tk=512)
    groups = [_dilated_group(_proj_deint(xb, w_in_b, g, tm=1024, bsz=bsz, seq_len=seq_len), g)
              for g in range(B_GROUPS)]
    mixed = _mix(out_a.reshape(t, A_Q_W), [o for o, _ in groups], [l for _, l in groups], gates,
                 w_proj_a.astype(BF16), w_proj_b.astype(BF16), tm=512)
    x1 = _out_ln(mixed, x2d, w_out.astype(BF16), vec(ln1_g), vec(ln1_b), tm=512)
    x2 = _ffn(x1, w_ffn_gate.astype(BF16), w_ffn_up.astype(BF16), w_ffn_down.astype(BF16),
              vec(ln2_g), vec(ln2_b), tm=512, tf=512)
    return x2.reshape(bsz, seq_len, d)


def kernel(x, w_in, b_gate, q_norm_a, k_norm_a, w_proj_a, w_proj_b, w_out, ln1_g, ln1_b, w_ffn_gate, w_ffn_up, w_ffn_down, ln2_g, ln2_b):
    rope = _rope_tables(x.shape[1])
    for l in range(w_in.shape[0]):
        x = _layer(x, w_in[l], b_gate[l], q_norm_a[l], k_norm_a[l], w_proj_a[l], w_proj_b[l], w_out[l],
                   ln1_g[l], ln1_b[l], w_ffn_gate[l], w_ffn_up[l], w_ffn_down[l], ln2_g[l], ln2_b[l], rope)
    return x
```

```python
import functools

import jax
import jax.numpy as jnp
from jax import lax
from jax.experimental import pallas as pl
from jax.experimental.pallas import tpu as pltpu

F32 = jnp.float32
BF16 = jnp.bfloat16

D_MODEL = 2048
HEAD_DIM = 128
A_Q_HEADS = 8
A_KV_HEADS = 2
A_REP = A_Q_HEADS // A_KV_HEADS
ROPE_THETA = 10000.0
ROPE_AXIS_DIM = HEAD_DIM // 2
GRID_W = 64
B_DILATIONS = (1, 4, 16)
B_GROUPS = 3
B_HEADS_PER_GROUP = 4
B_HEADS = B_GROUPS * B_HEADS_PER_GROUP
B_SIDE = 64
A_Q_W = A_Q_HEADS * HEAD_DIM
A_KV_W = A_KV_HEADS * HEAD_DIM
B_W = B_HEADS * HEAD_DIM
B_OUT_W = B_HEADS_PER_GROUP * HEAD_DIM
FFN_HIDDEN = 5632
ALPHA = 2.0 ** 0.25
RMS_EPS = 1e-6
LN_EPS = 1e-5
Q_SCALE = HEAD_DIM ** -0.5

COL_QA = 0
COL_KA = A_Q_W
COL_VA = A_Q_W + A_KV_W
COL_QB = COL_VA + A_KV_W
COL_GA = COL_QB + 3 * B_W

VMEM_LIMIT = 60 * 1024 * 1024

NT_DIMS = (((1,), (1,)), ((), ()))
TN_DIMS = (((0,), (0,)), ((), ()))
LANES = 128
LOG2_E = 1.4426950408889634


def _params(*sem):
    return pltpu.CompilerParams(dimension_semantics=sem, vmem_limit_bytes=VMEM_LIMIT)


def _layer_norm(z, g, b):
    mu = jnp.mean(z, axis=-1, keepdims=True)
    zc = z - mu
    var = jnp.mean(zc * zc, axis=-1, keepdims=True)
    return zc * lax.rsqrt(var + LN_EPS) * g + b


ROPE_LANE_ORDER = tuple(list(range(0, 32)) + list(range(64, 96)) + list(range(32, 64)) + list(range(96, 128)))


def _norm_rope(yh, gain, cos, sin):
    mean_w = jnp.full((HEAD_DIM, HEAD_DIM), 1.0 / HEAD_DIM, BF16)
    sq = yh * yh
    hi = sq.astype(BF16)
    lo = (sq - hi.astype(F32)).astype(BF16)
    ms = jnp.dot(hi, mean_w, preferred_element_type=F32) + jnp.dot(lo, mean_w, preferred_element_type=F32)
    yn = yh * lax.rsqrt(ms + RMS_EPS) * gain
    return yn * cos + pltpu.roll(yn, HEAD_DIM // 2, 1) * sin


def _proj_q_kernel(x_ref, w_ref, gain_ref, cos_ref, sin_ref, o_ref):
    y = jnp.dot(x_ref[...], w_ref[...], preferred_element_type=F32)
    cos = cos_ref[...]
    sin = sin_ref[...]
    for h in range(y.shape[1] // HEAD_DIM):
        sl = slice(h * HEAD_DIM, (h + 1) * HEAD_DIM)
        o_ref[:, sl] = _norm_rope(y[:, sl], gain_ref[:, sl], cos, sin).astype(o_ref.dtype)


def _proj_kv_kernel(x_ref, w_ref, gain_ref, cos_ref, sin_ref, o_ref):
    y = jnp.dot(x_ref[...], w_ref[...], preferred_element_type=F32)
    cos = cos_ref[...]
    sin = sin_ref[...]
    for h in range(A_KV_HEADS):
        sl = slice(h * HEAD_DIM, (h + 1) * HEAD_DIM)
        o_ref[h] = _norm_rope(y[:, sl], gain_ref[:, sl], cos, sin).astype(o_ref.dtype)
    for h in range(A_KV_HEADS, 2 * A_KV_HEADS):
        o_ref[h] = y[:, h * HEAD_DIM:(h + 1) * HEAD_DIM].astype(o_ref.dtype)


def _proj_gate_kernel(x_ref, w_ref, b_ref, o_ref):
    y = jnp.dot(x_ref[...], w_ref[...], preferred_element_type=F32)
    o_ref[...] = (1.0 / (1.0 + jnp.exp(-(y + b_ref[...])))).astype(o_ref.dtype)


def _in_proj(kern, xb, w, col_off, n_cols, row_vecs, tabs=(), *, tm, tn, seq_len, head_major=False):
    t, d = xb.shape
    off = col_off // tn
    pos_blocks = seq_len // tm
    in_specs = [pl.BlockSpec((tm, d), lambda i, j: (i, 0)),
                pl.BlockSpec((d, tn), lambda i, j: (0, off + j))]
    in_specs += [pl.BlockSpec((1, tn), lambda i, j: (0, j)) for _ in row_vecs]
    in_specs += [pl.BlockSpec((tm, HEAD_DIM), lambda i, j: (i % pos_blocks, 0)) for _ in tabs]
    if head_major:
        assert n_cols == tn
        out_spec = pl.BlockSpec((tn // HEAD_DIM, tm, HEAD_DIM), lambda i, j: (0, i, 0))
        out_shape = jax.ShapeDtypeStruct((tn // HEAD_DIM, t, HEAD_DIM), BF16)
    else:
        out_spec = pl.BlockSpec((tm, tn), lambda i, j: (i, j))
        out_shape = jax.ShapeDtypeStruct((t, n_cols), BF16)
    return pl.pallas_call(
        kern,
        grid=(t // tm, n_cols // tn),
        in_specs=in_specs,
        out_specs=out_spec,
        out_shape=out_shape,
        compiler_params=_params("parallel", "arbitrary"),
    )(xb, w, *row_vecs, *tabs)


def _gqa_kernel(q_ref, k_ref, v_ref, o_ref, m_sc, l_sc, acc_sc, s0_sc, s1_sc, *, tk):
    tq = q_ref.shape[1]
    n_chunks = k_ref.shape[1] // tk
    s_sc = (s0_sc, s1_sc)
    q = jnp.concatenate([q_ref[0, :, r * HEAD_DIM:(r + 1) * HEAD_DIM] for r in range(A_REP)], axis=0)
    m_sc[...] = jnp.full(m_sc.shape, -jnp.inf, F32)
    l_sc[...] = jnp.zeros(l_sc.shape, F32)
    acc_sc[...] = jnp.zeros(acc_sc.shape, F32)

    def scores(c, slot):
        k0 = pl.multiple_of(c * tk, tk)
        s_sc[slot][...] = lax.dot_general(k_ref[0, pl.ds(k0, tk), :], q, NT_DIMS, preferred_element_type=F32)

    def softmax_pv(c, slot):
        k0 = pl.multiple_of(c * tk, tk)
        v = v_ref[0, pl.ds(k0, tk), :]
        st = s_sc[slot][...]
        m_prev = m_sc[...]
        m_new = jnp.maximum(m_prev, jnp.max(st, axis=0, keepdims=True))
        alpha = jnp.exp2(m_prev - m_new)
        pt = jnp.exp2(st - m_new)
        l_sc[...] = alpha * l_sc[...] + jnp.sum(pt, axis=0, keepdims=True)
        pv = lax.dot_general(v, pt.astype(BF16), TN_DIMS, preferred_element_type=F32)
        acc_sc[...] = alpha * acc_sc[...] + pv
        m_sc[...] = m_new

    scores(0, 0)

    def body(c2, carry):
        c = 2 * c2
        scores(c + 1, 1)
        softmax_pv(c, 0)
        scores(jnp.minimum(c + 2, n_chunks - 1), 0)
        softmax_pv(c + 1, 1)
        return carry

    lax.fori_loop(0, n_chunks // 2, body, 0)
    ot = acc_sc[...] / l_sc[...]
    for r in range(A_REP):
        o_ref[0, :, r * HEAD_DIM:(r + 1) * HEAD_DIM] = ot[:, r * tq:(r + 1) * tq].T.astype(o_ref.dtype)


def _gqa(qa, kv, *, tq, tk):
    b, s, _ = qa.shape
    assert (s // tk) % 2 == 0
    gw = A_REP * HEAD_DIM
    rows = A_REP * tq
    return pl.pallas_call(
        functools.partial(_gqa_kernel, tk=tk),
        grid=(b, A_KV_HEADS, s // tq),
        in_specs=[pl.BlockSpec((1, tq, gw), lambda bi, g, i: (bi, i, g)),
                  pl.BlockSpec((None, 1, s, HEAD_DIM), lambda bi, g, i: (g, bi, 0, 0)),
                  pl.BlockSpec((None, 1, s, HEAD_DIM), lambda bi, g, i: (A_KV_HEADS + g, bi, 0, 0))],
        out_specs=pl.BlockSpec((1, tq, gw), lambda bi, g, i: (bi, i, g)),
        out_shape=jax.ShapeDtypeStruct((b, s, A_Q_W), BF16),
        scratch_shapes=[pltpu.VMEM((1, rows), F32), pltpu.VMEM((1, rows), F32), pltpu.VMEM((HEAD_DIM, rows), F32),
                        pltpu.VMEM((tk, rows), F32), pltpu.VMEM((tk, rows), F32)],
        compiler_params=_params("parallel", "parallel", "arbitrary"),
    )(qa, kv, kv)


def _proj_deint_kernel(x_ref, wq_ref, wk_ref, wv_ref, o_ref, *stage, dil):
    x = x_ref[...]
    tm = x.shape[0]
    for which, w_ref in enumerate((wq_ref, wk_ref, wv_ref)):
        y = jnp.dot(x, w_ref[...], preferred_element_type=F32)
        if which == 0:
            y = y * Q_SCALE
        tn = y.shape[1]
        if dil == 1:
            o_ref[0, 0, :, which * tn:(which + 1) * tn] = y.astype(o_ref.dtype)
            continue
        y_sc = stage[which]
        for s in range(tn // LANES):
            y_sc[s] = y[:, s * LANES:(s + 1) * LANES]
        for r in range(dil):
            for s in range(tn // LANES):
                c0 = which * tn + s * LANES
                o_ref[0, r, :, c0:c0 + LANES] = y_sc[s, pl.ds(r, tm // dil, stride=dil), :].astype(o_ref.dtype)


def _proj_deint(xb, w, g, *, tm, bsz, seq_len):
    t, d = xb.shape
    dil = B_DILATIONS[g]
    tn = B_OUT_W
    tiles_per_seq = seq_len // tm
    col0 = COL_QB // tn
    w_spec = lambda which: pl.BlockSpec((d, tn), lambda i: (0, col0 + B_GROUPS * which + g))
    return pl.pallas_call(
        functools.partial(_proj_deint_kernel, dil=dil),
        grid=(t // tm,),
        in_specs=[pl.BlockSpec((tm, d), lambda i: (i, 0)), w_spec(0), w_spec(1), w_spec(2)],
        out_specs=pl.BlockSpec((1, dil, tm // dil, 3 * tn), lambda i: (i // tiles_per_seq, 0, i % tiles_per_seq, 0)),
        out_shape=jax.ShapeDtypeStruct((bsz, dil, seq_len // dil, 3 * tn), BF16),
        scratch_shapes=[pltpu.VMEM((tn // LANES, tm, LANES), F32)] * (3 if dil > 1 else 0),
        compiler_params=_params("parallel"),
    )(xb, w, w, w)


def _dilated_kernel(q_ref, k_ref, v_ref, o_ref, lse_ref, *token_order_scratch, dil, slopes):
    qb = 2 * B_SIDE
    kw = 4 * B_SIDE
    n_rows = k_ref.shape[2]
    row = lax.broadcasted_iota(jnp.int32, (qb, kw), 0)
    col = lax.broadcasted_iota(jnp.int32, (qb, kw), 1)
    q0 = pl.program_id(1) * qb
    ks = pl.multiple_of(jnp.clip(q0 - B_SIDE, 0, n_rows - kw), B_SIDE)
    dist = jnp.abs(col - row + (ks - q0))
    valid = dist <= B_SIDE
    token_dist = dist.astype(F32) * float(dil)

    def residue(r, carry):
        for h in range(B_HEADS_PER_GROUP):
            sl = slice(h * HEAD_DIM, (h + 1) * HEAD_DIM)
            q = q_ref[0, r, :, sl]
            k = k_ref[0, r, pl.ds(ks, kw), sl]
            v = v_ref[0, r, pl.ds(ks, kw), sl]
            s = lax.dot_general(q, k, NT_DIMS, preferred_element_type=F32) - slopes[h] * token_dist
            s = jnp.where(valid, s, -jnp.inf)
            m = jnp.max(s, axis=-1, keepdims=True)
            p = jnp.exp(s - m)
            l = jnp.sum(p, axis=-1, keepdims=True)
            o = jnp.dot(p.astype(BF16), v, preferred_element_type=F32) / l
            lse = jnp.broadcast_to(m + jnp.log(l), (qb, HEAD_DIM))
            if dil == 1:
                o_ref[0, :, sl] = o
                lse_ref[0, :, sl] = lse
            else:
                o_sc[h, pl.ds(r, qb, stride=dil), :] = o
                l_sc[h, pl.ds(r, qb, stride=dil), :] = lse
        return carry

    if dil == 1:
        residue(0, 0)
    else:
        o_sc, l_sc = token_order_scratch
        lax.fori_loop(0, dil, residue, 0)
        for h in range(B_HEADS_PER_GROUP):
            sl = slice(h * HEAD_DIM, (h + 1) * HEAD_DIM)
            o_ref[0, :, sl] = o_sc[h]
            lse_ref[0, :, sl] = l_sc[h]


def _dilated_group(qkv, g):
    b, dil, n_rows, _ = qkv.shape
    s = n_rows * dil
    qb = 2 * B_SIDE
    assert n_rows >= 4 * B_SIDE and n_rows % qb == 0
    slopes = tuple(2.0 ** (-8.0 * (g * B_HEADS_PER_GROUP + h + 1) / B_HEADS) for h in range(B_HEADS_PER_GROUP))
    kv_spec = lambda which: pl.BlockSpec((1, dil, n_rows, B_OUT_W), lambda bi, i: (bi, 0, 0, which))
    out_spec = pl.BlockSpec((1, qb * dil, B_OUT_W), lambda bi, i: (bi, i, 0))
    o, lse = pl.pallas_call(
        functools.partial(_dilated_kernel, dil=dil, slopes=slopes),
        grid=(b, n_rows // qb),
        in_specs=[pl.BlockSpec((1, dil, qb, B_OUT_W), lambda bi, i: (bi, 0, i, 0)), kv_spec(1), kv_spec(2)],
        out_specs=[out_spec, out_spec],
        out_shape=[jax.ShapeDtypeStruct((b, s, B_OUT_W), F32)] * 2,
        scratch_shapes=[pltpu.VMEM((B_HEADS_PER_GROUP, qb * dil, LANES), F32)] * (2 if dil > 1 else 0),
        compiler_params=_params("parallel", "arbitrary"),
    )(qkv, qkv, qkv)
    return o.reshape(b * s, B_OUT_W), lse.reshape(b * s, B_OUT_W)


def _mix_kernel(oa_ref, o0_ref, o1_ref, o2_ref, l0_ref, l1_ref, l2_ref, ga_ref, gb_ref, wpa_ref, wpb_ref, out_ref):
    l0, l1, l2 = l0_ref[...], l1_ref[...], l2_ref[...]
    mx = jnp.maximum(jnp.maximum(l0, l1), l2)
    e0, e1, e2 = jnp.exp(l0 - mx), jnp.exp(l1 - mx), jnp.exp(l2 - mx)
    ob = (e0 * o0_ref[...] + e1 * o1_ref[...] + e2 * o2_ref[...]) / (e0 + e1 + e2)
    pa = jnp.dot(oa_ref[...], wpa_ref[...], preferred_element_type=F32)
    pb = jnp.dot(ob.astype(BF16), wpb_ref[...], preferred_element_type=F32)
    out_ref[...] = (ga_ref[...].astype(F32) * pa + gb_ref[...].astype(F32) * pb).astype(out_ref.dtype)


def _mix(out_a, o_groups, lse_groups, gates, wpa, wpb, *, tm):
    t = out_a.shape[0]
    row = lambda w, cb=0: pl.BlockSpec((tm, w), lambda i: (i, cb))
    full = lambda a: pl.BlockSpec(a.shape, lambda i: (0, 0))
    return pl.pallas_call(
        _mix_kernel,
        grid=(t // tm,),
        in_specs=[row(A_Q_W)] + [row(B_OUT_W)] * 6 + [row(D_MODEL, 0), row(D_MODEL, 1), full(wpa), full(wpb)],
        out_specs=row(D_MODEL),
        out_shape=jax.ShapeDtypeStruct((t, D_MODEL), BF16),
        compiler_params=_params("parallel"),
    )(out_a, *o_groups, *lse_groups, gates, gates, wpa, wpb)


def _out_ln_kernel(mixed_ref, x_ref, w_ref, g_ref, b_ref, o_ref):
    y = jnp.dot(mixed_ref[...], w_ref[...], preferred_element_type=F32)
    o_ref[...] = _layer_norm(ALPHA * x_ref[...] + y, g_ref[...], b_ref[...])


def _out_ln(mixed, x2d, w_out, g, b, *, tm):
    t, d = x2d.shape
    row = pl.BlockSpec((tm, d), lambda i: (i, 0))
    vec = pl.BlockSpec((1, d), lambda i: (0, 0))
    return pl.pallas_call(
        _out_ln_kernel,
        grid=(t // tm,),
        in_specs=[row, row, pl.BlockSpec((d, d), lambda i: (0, 0)), vec, vec],
        out_specs=row,
        out_shape=jax.ShapeDtypeStruct((t, d), F32),
        compiler_params=_params("parallel"),
    )(mixed, x2d, w_out, g, b)


def _ffn_kernel(x_ref, wg_ref, wu_ref, wd_ref, g_ref, b_ref, o_ref, xb_sc, acc_sc):
    f = pl.program_id(1)

    @pl.when(f == 0)
    def _():
        xb_sc[...] = x_ref[...].astype(BF16)
        acc_sc[...] = jnp.zeros(acc_sc.shape, F32)

    xb = xb_sc[...]
    gate = jnp.dot(xb, wg_ref[...], preferred_element_type=F32)
    up = jnp.dot(xb, wu_ref[...], preferred_element_type=F32)
    h = gate * (1.0 / (1.0 + jnp.exp(-gate))) * up
    acc_sc[...] += jnp.dot(h.astype(BF16), wd_ref[...], preferred_element_type=F32)

    @pl.when(f == pl.num_programs(1) - 1)
    def _():
        o_ref[...] = _layer_norm(ALPHA * x_ref[...] + acc_sc[...], g_ref[...], b_ref[...])


def _ffn(x1, wg, wu, wd, g, b, *, tm, tf):
    t, d = x1.shape
    hidden = wg.shape[1]
    row = pl.BlockSpec((tm, d), lambda i, f: (i, 0))
    vec = pl.BlockSpec((1, d), lambda i, f: (0, 0))
    return pl.pallas_call(
        _ffn_kernel,
        grid=(t // tm, hidden // tf),
        in_specs=[row,
                  pl.BlockSpec((d, tf), lambda i, f: (0, f)),
                  pl.BlockSpec((d, tf), lambda i, f: (0, f)),
                  pl.BlockSpec((tf, d), lambda i, f: (f, 0)),
                  vec, vec],
        out_specs=row,
        out_shape=jax.ShapeDtypeStruct((t, d), F32),
        scratch_shapes=[pltpu.VMEM((tm, d), BF16), pltpu.VMEM((tm, d), F32)],
        compiler_params=_params("parallel", "arbitrary"),
    )(x1, wg, wu, wd, g, b)


def _rope_tables(seq_len):
    pos = jnp.arange(seq_len)
    n_freq = ROPE_AXIS_DIM // 2
    freqs = ROPE_THETA ** (-jnp.arange(n_freq, dtype=F32) / n_freq)
    ang_r = (pos // GRID_W).astype(F32)[:, None] * freqs[None, :]
    ang_c = (pos % GRID_W).astype(F32)[:, None] * freqs[None, :]
    cos = jnp.concatenate([jnp.cos(ang_r), jnp.cos(ang_c)] * 2, axis=-1)
    sin = jnp.concatenate([-jnp.sin(ang_r), -jnp.sin(ang_c), jnp.sin(ang_r), jnp.sin(ang_c)], axis=-1)
    return cos, sin


def _layer(x, w_in, b_gate, q_norm_a, k_norm_a, w_proj_a, w_proj_b, w_out,
           ln1_g, ln1_b, w_ffn_gate, w_ffn_up, w_ffn_down, ln2_g, ln2_b, rope):
    bsz, seq_len, d = x.shape
    t = bsz * seq_len
    x2d = x.reshape(t, d)
    xb = x2d.astype(BF16)
    w_in_b = w_in.astype(BF16)
    vec = lambda a: a.reshape(1, -1).astype(F32)

    order = jnp.asarray(ROPE_LANE_ORDER)
    head_cols = lambda col0, n_heads: (col0 + HEAD_DIM * jnp.arange(n_heads)[:, None] + order[None, :]).reshape(-1)
    w_q = w_in_b[:, head_cols(COL_QA, A_Q_HEADS)]
    w_kv = jnp.concatenate([w_in_b[:, head_cols(COL_KA, A_KV_HEADS)], w_in_b[:, COL_VA:COL_VA + A_KV_W]], axis=1)
    q_gain = jnp.tile(q_norm_a[order] * (Q_SCALE * LOG2_E), A_Q_HEADS)
    kv_gain = jnp.concatenate([jnp.tile(k_norm_a[order], A_KV_HEADS), jnp.ones((A_KV_W,), F32)])
    proj = functools.partial(_in_proj, tm=1024, seq_len=seq_len)
    qa = proj(_proj_q_kernel, xb, w_q, 0, A_Q_W, [vec(q_gain)], rope, tn=512)
    kv = proj(_proj_kv_kernel, xb, w_kv, 0, 2 * A_KV_W, [vec(kv_gain)], rope, tn=2 * A_KV_W, head_major=True)
    gates = proj(_proj_gate_kernel, xb, w_in_b, COL_GA, 2 * D_MODEL, [vec(b_gate)], tn=1024)

    out_a = _gqa(qa.reshape(bsz, seq_len, A_Q_W), kv.reshape(2 * A_KV_HEADS, bsz, seq_len, HEAD_DIM), tq=256, tk=512)
    groups = [_dilated_group(_proj_deint(xb, w_in_b, g, tm=1024, bsz=bsz, seq_len=seq_len), g)
              for g in range(B_GROUPS)]
    mixed = _mix(out_a.reshape(t, A_Q_W), [o for o, _ in groups], [l for _, l in groups], gates,
                 w_proj_a.astype(BF16), w_proj_b.astype(BF16), tm=512)
    x1 = _out_ln(mixed, x2d, w_out.astype(BF16), vec(ln1_g), vec(ln1_b), tm=512)
    x2 = _ffn(x1, w_ffn_gate.astype(BF16), w_ffn_up.astype(BF16), w_ffn_down.astype(BF16),
              vec(ln2_g), vec(ln2_b), tm=512, tf=512)
    return x2.reshape(bsz, seq_len, d)


def kernel(x, w_in, b_gate, q_norm_a, k_norm_a, w_proj_a, w_proj_b, w_out, ln1_g, ln1_b, w_ffn_gate, w_ffn_up, w_ffn_down, ln2_g, ln2_b):
    rope = _rope_tables(x.shape[1])
    for l in range(w_in.shape[0]):
        x = _layer(x, w_in[l], b_gate[l], q_norm_a[l], k_norm_a[l], w_proj_a[l], w_proj_b[l], w_out[l],
                   ln1_g[l], ln1_b[l], w_ffn_gate[l], w_ffn_up[l], w_ffn_down[l], ln2_g[l], ln2_b[l], rope)
    return x
```

```python
import functools

import jax
import jax.numpy as jnp
from jax import lax
from jax.experimental import pallas as pl
from jax.experimental.pallas import tpu as pltpu

F32 = jnp.float32
BF16 = jnp.bfloat16

D_MODEL = 2048
HEAD_DIM = 128
A_Q_HEADS = 8
A_KV_HEADS = 2
A_REP = A_Q_HEADS // A_KV_HEADS
ROPE_THETA = 10000.0
ROPE_AXIS_DIM = HEAD_DIM // 2
GRID_W = 64
B_DILATIONS = (1, 4, 16)
B_GROUPS = 3
B_HEADS_PER_GROUP = 4
B_HEADS = B_GROUPS * B_HEADS_PER_GROUP
B_SIDE = 64
A_Q_W = A_Q_HEADS * HEAD_DIM
A_KV_W = A_KV_HEADS * HEAD_DIM
B_W = B_HEADS * HEAD_DIM
B_OUT_W = B_HEADS_PER_GROUP * HEAD_DIM
FFN_HIDDEN = 5632
ALPHA = 2.0 ** 0.25
RMS_EPS = 1e-6
LN_EPS = 1e-5
Q_SCALE = HEAD_DIM ** -0.5

COL_QA = 0
COL_KA = A_Q_W
COL_VA = A_Q_W + A_KV_W
COL_QB = COL_VA + A_KV_W
COL_GA = COL_QB + 3 * B_W

VMEM_LIMIT = 60 * 1024 * 1024

NT_DIMS = (((1,), (1,)), ((), ()))
TN_DIMS = (((0,), (0,)), ((), ()))
LANES = 128
LOG2_E = 1.4426950408889634


def _params(*sem):
    return pltpu.CompilerParams(dimension_semantics=sem, vmem_limit_bytes=VMEM_LIMIT)


def _layer_norm(z, g, b):
    mu = jnp.mean(z, axis=-1, keepdims=True)
    zc = z - mu
    var = jnp.mean(zc * zc, axis=-1, keepdims=True)
    return zc * lax.rsqrt(var + LN_EPS) * g + b


ROPE_LANE_ORDER = tuple(list(range(0, 32)) + list(range(64, 96)) + list(range(32, 64)) + list(range(96, 128)))


def _norm_rope(yh, gain, cos, sin):
    mean_w = jnp.full((HEAD_DIM, HEAD_DIM), 1.0 / HEAD_DIM, BF16)
    sq = yh * yh
    hi = sq.astype(BF16)
    lo = (sq - hi.astype(F32)).astype(BF16)
    ms = jnp.dot(hi, mean_w, preferred_element_type=F32) + jnp.dot(lo, mean_w, preferred_element_type=F32)
    yn = yh * lax.rsqrt(ms + RMS_EPS) * gain
    return yn * cos + pltpu.roll(yn, HEAD_DIM // 2, 1) * sin


def _proj_q_kernel(x_ref, w_ref, gain_ref, cos_ref, sin_ref, o_ref):
    y = jnp.dot(x_ref[...], w_ref[...], preferred_element_type=F32)
    cos = cos_ref[...]
    sin = sin_ref[...]
    for h in range(y.shape[1] // HEAD_DIM):
        sl = slice(h * HEAD_DIM, (h + 1) * HEAD_DIM)
        o_ref[:, sl] = _norm_rope(y[:, sl], gain_ref[:, sl], cos, sin).astype(o_ref.dtype)


def _proj_kv_kernel(x_ref, w_ref, gain_ref, cos_ref, sin_ref, o_ref):
    y = jnp.dot(x_ref[...], w_ref[...], preferred_element_type=F32)
    cos = cos_ref[...]
    sin = sin_ref[...]
    for h in range(A_KV_HEADS):
        sl = slice(h * HEAD_DIM, (h + 1) * HEAD_DIM)
        o_ref[h] = _norm_rope(y[:, sl], gain_ref[:, sl], cos, sin).astype(o_ref.dtype)
    for h in range(A_KV_HEADS, 2 * A_KV_HEADS):
        o_ref[h] = y[:, h * HEAD_DIM:(h + 1) * HEAD_DIM].astype(o_ref.dtype)


def _proj_gate_kernel(x_ref, w_ref, b_ref, o_ref):
    y = jnp.dot(x_ref[...], w_ref[...], preferred_element_type=F32)
    o_ref[...] = (1.0 / (1.0 + jnp.exp(-(y + b_ref[...])))).astype(o_ref.dtype)


def _in_proj(kern, xb, w, col_off, n_cols, row_vecs, tabs=(), *, tm, tn, seq_len, head_major=False):
    t, d = xb.shape
    off = col_off // tn
    pos_blocks = seq_len // tm
    in_specs = [pl.BlockSpec((tm, d), lambda i, j: (i, 0)),
                pl.BlockSpec((d, tn), lambda i, j: (0, off + j))]
    in_specs += [pl.BlockSpec((1, tn), lambda i, j: (0, j)) for _ in row_vecs]
    in_specs += [pl.BlockSpec((tm, HEAD_DIM), lambda i, j: (i % pos_blocks, 0)) for _ in tabs]
    if head_major:
        assert n_cols == tn
        out_spec = pl.BlockSpec((tn // HEAD_DIM, tm, HEAD_DIM), lambda i, j: (0, i, 0))
        out_shape = jax.ShapeDtypeStruct((tn // HEAD_DIM, t, HEAD_DIM), BF16)
    else:
        out_spec = pl.BlockSpec((tm, tn), lambda i, j: (i, j))
        out_shape = jax.ShapeDtypeStruct((t, n_cols), BF16)
    return pl.pallas_call(
        kern,
        grid=(t // tm, n_cols // tn),
        in_specs=in_specs,
        out_specs=out_spec,
        out_shape=out_shape,
        compiler_params=_params("parallel", "arbitrary"),
    )(xb, w, *row_vecs, *tabs)


def _gqa_kernel(q_ref, k_ref, v_ref, o_ref, m_sc, l_sc, acc_sc, s0_sc, s1_sc, *, tk):
    tq = q_ref.shape[1]
    n_chunks = k_ref.shape[1] // tk
    s_sc = (s0_sc, s1_sc)
    q = jnp.concatenate([q_ref[0, :, r * HEAD_DIM:(r + 1) * HEAD_DIM] for r in range(A_REP)], axis=0)
    m_sc[...] = jnp.full(m_sc.shape, -jnp.inf, F32)
    l_sc[...] = jnp.zeros(l_sc.shape, F32)
    acc_sc[...] = jnp.zeros(acc_sc.shape, F32)

    def scores(c, slot):
        s_sc[slot][...] = lax.dot_general(k_ref[0, c * tk:(c + 1) * tk, :], q, NT_DIMS, preferred_element_type=F32)

    def softmax_pv(c, slot):
        v = v_ref[0, c * tk:(c + 1) * tk, :]
        st = s_sc[slot][...]
        m_prev = m_sc[...]
        m_new = jnp.maximum(m_prev, jnp.max(st, axis=0, keepdims=True))
        alpha = jnp.exp2(m_prev - m_new)
        pt = jnp.exp2(st - m_new)
        l_sc[...] = alpha * l_sc[...] + jnp.sum(pt, axis=0, keepdims=True)
        pv = lax.dot_general(v, pt.astype(BF16), TN_DIMS, preferred_element_type=F32)
        acc_sc[...] = alpha * acc_sc[...] + pv
        m_sc[...] = m_new

    scores(0, 0)
    for c in range(n_chunks - 1):
        scores(c + 1, (c + 1) % 2)
        softmax_pv(c, c % 2)
    softmax_pv(n_chunks - 1, (n_chunks - 1) % 2)
    ot = acc_sc[...] / l_sc[...]
    for r in range(A_REP):
        o_ref[0, :, r * HEAD_DIM:(r + 1) * HEAD_DIM] = ot[:, r * tq:(r + 1) * tq].T.astype(o_ref.dtype)


def _gqa(qa, kv, *, tq, tk):
    b, s, _ = qa.shape
    assert s % tk == 0 and s % tq == 0
    gw = A_REP * HEAD_DIM
    rows = A_REP * tq
    return pl.pallas_call(
        functools.partial(_gqa_kernel, tk=tk),
        grid=(b, A_KV_HEADS, s // tq),
        in_specs=[pl.BlockSpec((1, tq, gw), lambda bi, g, i: (bi, i, g)),
                  pl.BlockSpec((None, 1, s, HEAD_DIM), lambda bi, g, i: (g, bi, 0, 0)),
                  pl.BlockSpec((None, 1, s, HEAD_DIM), lambda bi, g, i: (A_KV_HEADS + g, bi, 0, 0))],
        out_specs=pl.BlockSpec((1, tq, gw), lambda bi, g, i: (bi, i, g)),
        out_shape=jax.ShapeDtypeStruct((b, s, A_Q_W), BF16),
        scratch_shapes=[pltpu.VMEM((1, rows), F32), pltpu.VMEM((1, rows), F32), pltpu.VMEM((HEAD_DIM, rows), F32),
                        pltpu.VMEM((tk, rows), F32), pltpu.VMEM((tk, rows), F32)],
        compiler_params=_params("parallel", "parallel", "arbitrary"),
    )(qa, kv, kv)


def _proj_deint_kernel(x_ref, wq_ref, wk_ref, wv_ref, o_ref, *stage, dil):
    x = x_ref[...]
    tm = x.shape[0]
    for which, w_ref in enumerate((wq_ref, wk_ref, wv_ref)):
        y = jnp.dot(x, w_ref[...], preferred_element_type=F32)
        if which == 0:
            y = y * Q_SCALE
        tn = y.shape[1]
        if dil == 1:
            o_ref[0, 0, :, which * tn:(which + 1) * tn] = y.astype(o_ref.dtype)
            continue
        y_sc = stage[which]
        for s in range(tn // LANES):
            y_sc[s] = y[:, s * LANES:(s + 1) * LANES]
        for r in range(dil):
            for s in range(tn // LANES):
                c0 = which * tn + s * LANES
                o_ref[0, r, :, c0:c0 + LANES] = y_sc[s, pl.ds(r, tm // dil, stride=dil), :].astype(o_ref.dtype)


def _proj_deint(xb, w, g, *, tm, bsz, seq_len):
    t, d = xb.shape
    dil = B_DILATIONS[g]
    tn = B_OUT_W
    tiles_per_seq = seq_len // tm
    col0 = COL_QB // tn
    w_spec = lambda which: pl.BlockSpec((d, tn), lambda i: (0, col0 + B_GROUPS * which + g))
    return pl.pallas_call(
        functools.partial(_proj_deint_kernel, dil=dil),
        grid=(t // tm,),
        in_specs=[pl.BlockSpec((tm, d), lambda i: (i, 0)), w_spec(0), w_spec(1), w_spec(2)],
        out_specs=pl.BlockSpec((1, dil, tm // dil, 3 * tn), lambda i: (i // tiles_per_seq, 0, i % tiles_per_seq, 0)),
        out_shape=jax.ShapeDtypeStruct((bsz, dil, seq_len // dil, 3 * tn), BF16),
        scratch_shapes=[pltpu.VMEM((tn // LANES, tm, LANES), F32)] * (3 if dil > 1 else 0),
        compiler_params=_params("parallel"),
    )(xb, w, w, w)


def _dilated_kernel(q_ref, k_ref, v_ref, o_ref, lse_ref, *token_order_scratch, dil, slopes):
    qb = 2 * B_SIDE
    kw = 4 * B_SIDE
    n_rows = k_ref.shape[2]
    row = lax.broadcasted_iota(jnp.int32, (qb, kw), 0)
    col = lax.broadcasted_iota(jnp.int32, (qb, kw), 1)
    q0 = pl.program_id(1) * qb
    ks = pl.multiple_of(jnp.clip(q0 - B_SIDE, 0, n_rows - kw), B_SIDE)
    dist = jnp.abs(col - row + (ks - q0))
    valid = dist <= B_SIDE
    token_dist = dist.astype(F32) * float(dil)

    def residue(r, carry):
        for h in range(B_HEADS_PER_GROUP):
            sl = slice(h * HEAD_DIM, (h + 1) * HEAD_DIM)
            q = q_ref[0, r, :, sl]
            k = k_ref[0, r, pl.ds(ks, kw), sl]
            v = v_ref[0, r, pl.ds(ks, kw), sl]
            s = lax.dot_general(q, k, NT_DIMS, preferred_element_type=F32) - slopes[h] * token_dist
            s = jnp.where(valid, s, -jnp.inf)
            m = jnp.max(s, axis=-1, keepdims=True)
            p = jnp.exp(s - m)
            l = jnp.sum(p, axis=-1, keepdims=True)
            o = jnp.dot(p.astype(BF16), v, preferred_element_type=F32) / l
            lse = jnp.broadcast_to(m + jnp.log(l), (qb, HEAD_DIM))
            if dil == 1:
                o_ref[0, :, sl] = o.astype(o_ref.dtype)
                lse_ref[0, :, sl] = lse
            else:
                o_sc[h, pl.ds(r, qb, stride=dil), :] = o
                l_sc[h, pl.ds(r, qb, stride=dil), :] = lse
        return carry

    if dil == 1:
        residue(0, 0)
    else:
        o_sc, l_sc = token_order_scratch
        lax.fori_loop(0, dil, residue, 0)
        for h in range(B_HEADS_PER_GROUP):
            sl = slice(h * HEAD_DIM, (h + 1) * HEAD_DIM)
            o_ref[0, :, sl] = o_sc[h].astype(o_ref.dtype)
            lse_ref[0, :, sl] = l_sc[h]


def _dilated_group(qkv, g):
    b, dil, n_rows, _ = qkv.shape
    s = n_rows * dil
    qb = 2 * B_SIDE
    assert n_rows >= 4 * B_SIDE and n_rows % qb == 0
    slopes = tuple(2.0 ** (-8.0 * (g * B_HEADS_PER_GROUP + h + 1) / B_HEADS) for h in range(B_HEADS_PER_GROUP))
    kv_spec = lambda which: pl.BlockSpec((1, dil, n_rows, B_OUT_W), lambda bi, i: (bi, 0, 0, which))
    out_spec = pl.BlockSpec((1, qb * dil, B_OUT_W), lambda bi, i: (bi, i, 0))
    o, lse = pl.pallas_call(
        functools.partial(_dilated_kernel, dil=dil, slopes=slopes),
        grid=(b, n_rows // qb),
        in_specs=[pl.BlockSpec((1, dil, qb, B_OUT_W), lambda bi, i: (bi, 0, i, 0)), kv_spec(1), kv_spec(2)],
        out_specs=[out_spec, out_spec],
        out_shape=[jax.ShapeDtypeStruct((b, s, B_OUT_W), BF16), jax.ShapeDtypeStruct((b, s, B_OUT_W), F32)],
        scratch_shapes=[pltpu.VMEM((B_HEADS_PER_GROUP, qb * dil, LANES), F32)] * (2 if dil > 1 else 0),
        compiler_params=_params("parallel", "arbitrary"),
    )(qkv, qkv, qkv)
    return o.reshape(b * s, B_OUT_W), lse.reshape(b * s, B_OUT_W)


def _mix_kernel(oa_ref, o0_ref, o1_ref, o2_ref, l0_ref, l1_ref, l2_ref, ga_ref, gb_ref, wpa_ref, wpb_ref, out_ref):
    l0, l1, l2 = l0_ref[...], l1_ref[...], l2_ref[...]
    mx = jnp.maximum(jnp.maximum(l0, l1), l2)
    e0, e1, e2 = jnp.exp(l0 - mx), jnp.exp(l1 - mx), jnp.exp(l2 - mx)
    ob = (e0 * o0_ref[...] + e1 * o1_ref[...] + e2 * o2_ref[...]) / (e0 + e1 + e2)
    pa = jnp.dot(oa_ref[...], wpa_ref[...], preferred_element_type=F32)
    pb = jnp.dot(ob.astype(BF16), wpb_ref[...], preferred_element_type=F32)
    out_ref[...] = (ga_ref[...].astype(F32) * pa + gb_ref[...].astype(F32) * pb).astype(out_ref.dtype)


def _mix(out_a, o_groups, lse_groups, gates, wpa, wpb, *, tm):
    t = out_a.shape[0]
    row = lambda w, cb=0: pl.BlockSpec((tm, w), lambda i: (i, cb))
    full = lambda a: pl.BlockSpec(a.shape, lambda i: (0, 0))
    return pl.pallas_call(
        _mix_kernel,
        grid=(t // tm,),
        in_specs=[row(A_Q_W)] + [row(B_OUT_W)] * 6 + [row(D_MODEL, 0), row(D_MODEL, 1), full(wpa), full(wpb)],
        out_specs=row(D_MODEL),
        out_shape=jax.ShapeDtypeStruct((t, D_MODEL), BF16),
        compiler_params=_params("parallel"),
    )(out_a, *o_groups, *lse_groups, gates, gates, wpa, wpb)


def _out_ln_kernel(mixed_ref, x_ref, w_ref, g_ref, b_ref, o_ref):
    y = jnp.dot(mixed_ref[...], w_ref[...], preferred_element_type=F32)
    o_ref[...] = _layer_norm(ALPHA * x_ref[...] + y, g_ref[...], b_ref[...])


def _out_ln(mixed, x2d, w_out, g, b, *, tm):
    t, d = x2d.shape
    row = pl.BlockSpec((tm, d), lambda i: (i, 0))
    vec = pl.BlockSpec((1, d), lambda i: (0, 0))
    return pl.pallas_call(
        _out_ln_kernel,
        grid=(t // tm,),
        in_specs=[row, row, pl.BlockSpec((d, d), lambda i: (0, 0)), vec, vec],
        out_specs=row,
        out_shape=jax.ShapeDtypeStruct((t, d), F32),
        compiler_params=_params("parallel"),
    )(mixed, x2d, w_out, g, b)


def _ffn_kernel(x_ref, wg_ref, wu_ref, wd_ref, g_ref, b_ref, o_ref, xb_sc, acc_sc):
    f = pl.program_id(1)

    @pl.when(f == 0)
    def _():
        xb_sc[...] = x_ref[...].astype(BF16)
        acc_sc[...] = jnp.zeros(acc_sc.shape, F32)

    xb = xb_sc[...]
    gate = jnp.dot(xb, wg_ref[...], preferred_element_type=F32)
    up = jnp.dot(xb, wu_ref[...], preferred_element_type=F32)
    h = gate * (1.0 / (1.0 + jnp.exp(-gate))) * up
    acc_sc[...] += jnp.dot(h.astype(BF16), wd_ref[...], preferred_element_type=F32)

    @pl.when(f == pl.num_programs(1) - 1)
    def _():
        o_ref[...] = _layer_norm(ALPHA * x_ref[...] + acc_sc[...], g_ref[...], b_ref[...])


def _ffn(x1, wg, wu, wd, g, b, *, tm, tf):
    t, d = x1.shape
    hidden = wg.shape[1]
    row = pl.BlockSpec((tm, d), lambda i, f: (i, 0))
    vec = pl.BlockSpec((1, d), lambda i, f: (0, 0))
    return pl.pallas_call(
        _ffn_kernel,
        grid=(t // tm, hidden // tf),
        in_specs=[row,
                  pl.BlockSpec((d, tf), lambda i, f: (0, f)),
                  pl.BlockSpec((d, tf), lambda i, f: (0, f)),
                  pl.BlockSpec((tf, d), lambda i, f: (f, 0)),
                  vec, vec],
        out_specs=row,
        out_shape=jax.ShapeDtypeStruct((t, d), F32),
        scratch_shapes=[pltpu.VMEM((tm, d), BF16), pltpu.VMEM((tm, d), F32)],
        compiler_params=_params("parallel", "arbitrary"),
    )(x1, wg, wu, wd, g, b)


def _rope_tables(seq_len):
    pos = jnp.arange(seq_len)
    n_freq = ROPE_AXIS_DIM // 2
    freqs = ROPE_THETA ** (-jnp.arange(n_freq, dtype=F32) / n_freq)
    ang_r = (pos // GRID_W).astype(F32)[:, None] * freqs[None, :]
    ang_c = (pos % GRID_W).astype(F32)[:, None] * freqs[None, :]
    cos = jnp.concatenate([jnp.cos(ang_r), jnp.cos(ang_c)] * 2, axis=-1)
    sin = jnp.concatenate([-jnp.sin(ang_r), -jnp.sin(ang_c), jnp.sin(ang_r), jnp.sin(ang_c)], axis=-1)
    return cos, sin


def _layer(x, w_in, b_gate, q_norm_a, k_norm_a, w_proj_a, w_proj_b, w_out,
           ln1_g, ln1_b, w_ffn_gate, w_ffn_up, w_ffn_down, ln2_g, ln2_b, rope):
    bsz, seq_len, d = x.shape
    t = bsz * seq_len
    x2d = x.reshape(t, d)
    xb = x2d.astype(BF16)
    w_in_b = w_in.astype(BF16)
    vec = lambda a: a.reshape(1, -1).astype(F32)

    order = jnp.asarray(ROPE_LANE_ORDER)
    head_cols = lambda col0, n_heads: (col0 + HEAD_DIM * jnp.arange(n_heads)[:, None] + order[None, :]).reshape(-1)
    w_q = w_in_b[:, head_cols(COL_QA, A_Q_HEADS)]
    w_kv = jnp.concatenate([w_in_b[:, head_cols(COL_KA, A_KV_HEADS)], w_in_b[:, COL_VA:COL_VA + A_KV_W]], axis=1)
    q_gain = jnp.tile(q_norm_a[order] * (Q_SCALE * LOG2_E), A_Q_HEADS)
    kv_gain = jnp.concatenate([jnp.tile(k_norm_a[order], A_KV_HEADS), jnp.ones((A_KV_W,), F32)])
    proj = functools.partial(_in_proj, tm=1024, seq_len=seq_len)
    qa = proj(_proj_q_kernel, xb, w_q, 0, A_Q_W, [vec(q_gain)], rope, tn=512)
    kv = proj(_proj_kv_kernel, xb, w_kv, 0, 2 * A_KV_W, [vec(kv_gain)], rope, tn=2 * A_KV_W, head_major=True)
    gates = proj(_proj_gate_kernel, xb, w_in_b, COL_GA, 2 * D_MODEL, [vec(b_gate)], tn=1024)

    out_a = _gqa(qa.reshape(bsz, seq_len, A_Q_W), kv.reshape(2 * A_KV_HEADS, bsz, seq_len, HEAD_DIM), tq=256, tk=1024)
    groups = [_dilated_group(_proj_deint(xb, w_in_b, g, tm=1024, bsz=bsz, seq_len=seq_len), g)
              for g in range(B_GROUPS)]
    mixed = _mix(out_a.reshape(t, A_Q_W), [o for o, _ in groups], [l for _, l in groups], gates,
                 w_proj_a.astype(BF16), w_proj_b.astype(BF16), tm=512)
    x1 = _out_ln(mixed, x2d, w_out.astype(BF16), vec(ln1_g), vec(ln1_b), tm=512)
    x2 = _ffn(x1, w_ffn_gate.astype(BF16), w_ffn_up.astype(BF16), w_ffn_down.astype(BF16),
              vec(ln2_g), vec(ln2_b), tm=512, tf=512)
    return x2.reshape(bsz, seq_len, d)


def kernel(x, w_in, b_gate, q_norm_a, k_norm_a, w_proj_a, w_proj_b, w_out, ln1_g, ln1_b, w_ffn_gate, w_ffn_up, w_ffn_down, ln2_g, ln2_b):
    rope = _rope_tables(x.shape[1])
    for l in range(w_in.shape[0]):
        x = _layer(x, w_in[l], b_gate[l], q_norm_a[l], k_norm_a[l], w_proj_a[l], w_proj_b[l], w_out[l],
                   ln1_g[l], ln1_b[l], w_ffn_gate[l], w_ffn_up[l], w_ffn_down[l], ln2_g[l], ln2_b[l], rope)
    return x
```

```python
import functools

import jax
import jax.numpy as jnp
from jax import lax
from jax.experimental import pallas as pl
from jax.experimental.pallas import tpu as pltpu

F32 = jnp.float32
BF16 = jnp.bfloat16

D_MODEL = 2048
HEAD_DIM = 128
A_Q_HEADS = 8
A_KV_HEADS = 2
A_REP = A_Q_HEADS // A_KV_HEADS
ROPE_THETA = 10000.0
ROPE_AXIS_DIM = HEAD_DIM // 2
GRID_W = 64
B_DILATIONS = (1, 4, 16)
B_GROUPS = 3
B_HEADS_PER_GROUP = 4
B_HEADS = B_GROUPS * B_HEADS_PER_GROUP
B_SIDE = 64
A_Q_W = A_Q_HEADS * HEAD_DIM
A_KV_W = A_KV_HEADS * HEAD_DIM
B_W = B_HEADS * HEAD_DIM
B_OUT_W = B_HEADS_PER_GROUP * HEAD_DIM
FFN_HIDDEN = 5632
ALPHA = 2.0 ** 0.25
RMS_EPS = 1e-6
LN_EPS = 1e-5
Q_SCALE = HEAD_DIM ** -0.5

COL_QA = 0
COL_KA = A_Q_W
COL_VA = A_Q_W + A_KV_W
COL_QB = COL_VA + A_KV_W
COL_GA = COL_QB + 3 * B_W

VMEM_LIMIT = 60 * 1024 * 1024

NT_DIMS = (((1,), (1,)), ((), ()))
TN_DIMS = (((0,), (0,)), ((), ()))
LANES = 128
LOG2_E = 1.4426950408889634


def _params(*sem):
    return pltpu.CompilerParams(dimension_semantics=sem, vmem_limit_bytes=VMEM_LIMIT)


def _layer_norm(z, g, b):
    mu = jnp.mean(z, axis=-1, keepdims=True)
    zc = z - mu
    var = jnp.mean(zc * zc, axis=-1, keepdims=True)
    return zc * lax.rsqrt(var + LN_EPS) * g + b


ROPE_LANE_ORDER = tuple(list(range(0, 32)) + list(range(64, 96)) + list(range(32, 64)) + list(range(96, 128)))


def _norm_rope(yh, gain, cos, sin):
    mean_w = jnp.full((HEAD_DIM, HEAD_DIM), 1.0 / HEAD_DIM, BF16)
    sq = yh * yh
    hi = sq.astype(BF16)
    lo = (sq - hi.astype(F32)).astype(BF16)
    ms = jnp.dot(hi, mean_w, preferred_element_type=F32) + jnp.dot(lo, mean_w, preferred_element_type=F32)
    yn = yh * lax.rsqrt(ms + RMS_EPS) * gain
    return yn * cos + pltpu.roll(yn, HEAD_DIM // 2, 1) * sin


def _proj_q_kernel(x_ref, w_ref, gain_ref, cos_ref, sin_ref, o_ref, xb_ref):
    @pl.when(pl.program_id(1) == 0)
    def _():
        xb_ref[...] = x_ref[...].astype(xb_ref.dtype)

    y = jnp.dot(xb_ref[...], w_ref[...], preferred_element_type=F32)
    cos = cos_ref[...]
    sin = sin_ref[...]
    for h in range(y.shape[1] // HEAD_DIM):
        sl = slice(h * HEAD_DIM, (h + 1) * HEAD_DIM)
        o_ref[:, sl] = _norm_rope(y[:, sl], gain_ref[:, sl], cos, sin).astype(o_ref.dtype)


def _proj_q(x2d, w_q, gain, rope, *, tm, tn, seq_len):
    t, d = x2d.shape
    n_cols = w_q.shape[1]
    pos_blocks = seq_len // tm
    tab = pl.BlockSpec((tm, HEAD_DIM), lambda i, j: (i % pos_blocks, 0))
    return pl.pallas_call(
        _proj_q_kernel,
        grid=(t // tm, n_cols // tn),
        in_specs=[pl.BlockSpec((tm, d), lambda i, j: (i, 0)),
                  pl.BlockSpec((d, tn), lambda i, j: (0, j)),
                  pl.BlockSpec((1, tn), lambda i, j: (0, j)), tab, tab],
        out_specs=[pl.BlockSpec((tm, tn), lambda i, j: (i, j)), pl.BlockSpec((tm, d), lambda i, j: (i, 0))],
        out_shape=[jax.ShapeDtypeStruct((t, n_cols), BF16), jax.ShapeDtypeStruct((t, d), BF16)],
        compiler_params=_params("parallel", "arbitrary"),
    )(x2d, w_q, gain, *rope)


def _proj_kv_kernel(x_ref, w_ref, gain_ref, cos_ref, sin_ref, o_ref):
    y = jnp.dot(x_ref[...], w_ref[...], preferred_element_type=F32)
    cos = cos_ref[...]
    sin = sin_ref[...]
    for h in range(A_KV_HEADS):
        sl = slice(h * HEAD_DIM, (h + 1) * HEAD_DIM)
        o_ref[h] = _norm_rope(y[:, sl], gain_ref[:, sl], cos, sin).astype(o_ref.dtype)
    for h in range(A_KV_HEADS, 2 * A_KV_HEADS):
        o_ref[h] = y[:, h * HEAD_DIM:(h + 1) * HEAD_DIM].astype(o_ref.dtype)


def _proj_gate_kernel(x_ref, w_ref, b_ref, o_ref):
    y = jnp.dot(x_ref[...], w_ref[...], preferred_element_type=F32)
    o_ref[...] = (1.0 / (1.0 + jnp.exp(-(y + b_ref[...])))).astype(o_ref.dtype)


def _in_proj(kern, xb, w, col_off, n_cols, row_vecs, tabs=(), *, tm, tn, seq_len, head_major=False):
    t, d = xb.shape
    off = col_off // tn
    pos_blocks = seq_len // tm
    in_specs = [pl.BlockSpec((tm, d), lambda i, j: (i, 0)),
                pl.BlockSpec((d, tn), lambda i, j: (0, off + j))]
    in_specs += [pl.BlockSpec((1, tn), lambda i, j: (0, j)) for _ in row_vecs]
    in_specs += [pl.BlockSpec((tm, HEAD_DIM), lambda i, j: (i % pos_blocks, 0)) for _ in tabs]
    if head_major:
        assert n_cols == tn
        out_spec = pl.BlockSpec((tn // HEAD_DIM, tm, HEAD_DIM), lambda i, j: (0, i, 0))
        out_shape = jax.ShapeDtypeStruct((tn // HEAD_DIM, t, HEAD_DIM), BF16)
    else:
        out_spec = pl.BlockSpec((tm, tn), lambda i, j: (i, j))
        out_shape = jax.ShapeDtypeStruct((t, n_cols), BF16)
    return pl.pallas_call(
        kern,
        grid=(t // tm, n_cols // tn),
        in_specs=in_specs,
        out_specs=out_spec,
        out_shape=out_shape,
        compiler_params=_params("parallel", "arbitrary"),
    )(xb, w, *row_vecs, *tabs)


def _gqa_kernel(q_ref, k_ref, v_ref, o_ref, m_sc, l_sc, acc_sc, s0_sc, s1_sc, *, tk):
    tq = q_ref.shape[1]
    n_chunks = k_ref.shape[1] // tk
    s_sc = (s0_sc, s1_sc)
    q = jnp.concatenate([q_ref[0, :, r * HEAD_DIM:(r + 1) * HEAD_DIM] for r in range(A_REP)], axis=0)
    m_sc[...] = jnp.full(m_sc.shape, -jnp.inf, F32)
    l_sc[...] = jnp.zeros(l_sc.shape, F32)
    acc_sc[...] = jnp.zeros(acc_sc.shape, F32)

    def scores(c, slot):
        s_sc[slot][...] = lax.dot_general(k_ref[0, c * tk:(c + 1) * tk, :], q, NT_DIMS, preferred_element_type=F32)

    def softmax_pv(c, slot):
        v = v_ref[0, c * tk:(c + 1) * tk, :]
        st = s_sc[slot][...]
        m_prev = m_sc[...]
        m_new = jnp.maximum(m_prev, jnp.max(st, axis=0, keepdims=True))
        alpha = jnp.exp2(m_prev - m_new)
        pt = jnp.exp2(st - m_new)
        l_sc[...] = alpha * l_sc[...] + jnp.sum(pt, axis=0, keepdims=True)
        pv = lax.dot_general(v, pt.astype(BF16), TN_DIMS, preferred_element_type=F32)
        acc_sc[...] = alpha * acc_sc[...] + pv
        m_sc[...] = m_new

    scores(0, 0)
    for c in range(n_chunks - 1):
        scores(c + 1, (c + 1) % 2)
        softmax_pv(c, c % 2)
    softmax_pv(n_chunks - 1, (n_chunks - 1) % 2)
    ot = acc_sc[...] / l_sc[...]
    for r in range(A_REP):
        o_ref[0, :, r * HEAD_DIM:(r + 1) * HEAD_DIM] = ot[:, r * tq:(r + 1) * tq].T.astype(o_ref.dtype)


def _gqa(qa, kv, *, tq, tk):
    b, s, _ = qa.shape
    assert s % tk == 0 and s % tq == 0
    gw = A_REP * HEAD_DIM
    rows = A_REP * tq
    return pl.pallas_call(
        functools.partial(_gqa_kernel, tk=tk),
        grid=(b, A_KV_HEADS, s // tq),
        in_specs=[pl.BlockSpec((1, tq, gw), lambda bi, g, i: (bi, i, g)),
                  pl.BlockSpec((None, 1, s, HEAD_DIM), lambda bi, g, i: (g, bi, 0, 0)),
                  pl.BlockSpec((None, 1, s, HEAD_DIM), lambda bi, g, i: (A_KV_HEADS + g, bi, 0, 0))],
        out_specs=pl.BlockSpec((1, tq, gw), lambda bi, g, i: (bi, i, g)),
        out_shape=jax.ShapeDtypeStruct((b, s, A_Q_W), BF16),
        scratch_shapes=[pltpu.VMEM((1, rows), F32), pltpu.VMEM((1, rows), F32), pltpu.VMEM((HEAD_DIM, rows), F32),
                        pltpu.VMEM((tk, rows), F32), pltpu.VMEM((tk, rows), F32)],
        compiler_params=_params("parallel", "parallel", "arbitrary"),
    )(qa, kv, kv)


def _proj_deint_kernel(x_ref, wq_ref, wk_ref, wv_ref, o_ref, *stage, dil):
    x = x_ref[...]
    tm = x.shape[0]
    for which, w_ref in enumerate((wq_ref, wk_ref, wv_ref)):
        y = jnp.dot(x, w_ref[...], preferred_element_type=F32)
        if which == 0:
            y = y * Q_SCALE
        tn = y.shape[1]
        if dil == 1:
            o_ref[0, 0, :, which * tn:(which + 1) * tn] = y.astype(o_ref.dtype)
            continue
        y_sc = stage[which]
        for s in range(tn // LANES):
            y_sc[s] = y[:, s * LANES:(s + 1) * LANES]
        for r in range(dil):
            for s in range(tn // LANES):
                c0 = which * tn + s * LANES
                o_ref[0, r, :, c0:c0 + LANES] = y_sc[s, pl.ds(r, tm // dil, stride=dil), :].astype(o_ref.dtype)


def _proj_deint(xb, w, g, *, tm, bsz, seq_len):
    t, d = xb.shape
    dil = B_DILATIONS[g]
    tn = B_OUT_W
    tiles_per_seq = seq_len // tm
    col0 = COL_QB // tn
    w_spec = lambda which: pl.BlockSpec((d, tn), lambda i: (0, col0 + B_GROUPS * which + g))
    return pl.pallas_call(
        functools.partial(_proj_deint_kernel, dil=dil),
        grid=(t // tm,),
        in_specs=[pl.BlockSpec((tm, d), lambda i: (i, 0)), w_spec(0), w_spec(1), w_spec(2)],
        out_specs=pl.BlockSpec((1, dil, tm // dil, 3 * tn), lambda i: (i // tiles_per_seq, 0, i % tiles_per_seq, 0)),
        out_shape=jax.ShapeDtypeStruct((bsz, dil, seq_len // dil, 3 * tn), BF16),
        scratch_shapes=[pltpu.VMEM((tn // LANES, tm, LANES), F32)] * (3 if dil > 1 else 0),
        compiler_params=_params("parallel"),
    )(xb, w, w, w)


def _dilated_kernel(q_ref, k_ref, v_ref, o_ref, lse_ref, *token_order_scratch, dil, slopes, n_blocks, n_res):
    qb = 2 * B_SIDE
    kw = 4 * B_SIDE
    n_rows = k_ref.shape[2]
    row = lax.broadcasted_iota(jnp.int32, (qb, kw), 0)
    col = lax.broadcasted_iota(jnp.int32, (qb, kw), 1)

    def attend(r, blk, h, ks, valid, token_dist):
        sl = slice(h * HEAD_DIM, (h + 1) * HEAD_DIM)
        q = q_ref[0, r, blk * qb:(blk + 1) * qb, sl]
        k = k_ref[0, r, pl.ds(ks, kw), sl]
        v = v_ref[0, r, pl.ds(ks, kw), sl]
        s = lax.dot_general(q, k, NT_DIMS, preferred_element_type=F32) - slopes[h] * token_dist
        s = jnp.where(valid, s, -jnp.inf)
        m = jnp.max(s, axis=-1, keepdims=True)
        p = jnp.exp(s - m)
        l = jnp.sum(p, axis=-1, keepdims=True)
        o = jnp.dot(p.astype(BF16), v, preferred_element_type=F32) / l
        lse = jnp.broadcast_to(m + jnp.log(l), (qb, HEAD_DIM))
        if dil == 1:
            rows = slice(blk * qb, (blk + 1) * qb)
            o_ref[0, rows, sl] = o.astype(o_ref.dtype)
            lse_ref[0, rows, sl] = lse
        else:
            o_sc, l_sc = token_order_scratch
            o_sc[h, pl.ds(blk * qb * dil + r, qb, stride=dil), :] = o
            l_sc[h, pl.ds(blk * qb * dil + r, qb, stride=dil), :] = lse

    def residues(it, carry):
        for blk in range(n_blocks):
            q0 = (pl.program_id(1) * n_blocks + blk) * qb
            ks = pl.multiple_of(jnp.clip(q0 - B_SIDE, 0, n_rows - kw), B_SIDE)
            dist = jnp.abs(col - row + (ks - q0))
            valid = dist <= B_SIDE
            token_dist = dist.astype(F32) * float(dil)
            for u in range(n_res):
                for h in range(B_HEADS_PER_GROUP):
                    attend(it * n_res + u, blk, h, ks, valid, token_dist)
        return carry

    if dil == 1:
        residues(0, 0)
    else:
        o_sc, l_sc = token_order_scratch
        lax.fori_loop(0, dil // n_res, residues, 0)
        for h in range(B_HEADS_PER_GROUP):
            sl = slice(h * HEAD_DIM, (h + 1) * HEAD_DIM)
            o_ref[0, :, sl] = o_sc[h].astype(o_ref.dtype)
            lse_ref[0, :, sl] = l_sc[h]


DILATED_BATCHING = ((4, 1), (4, 1), (1, 4))


def _dilated_group(qkv, g):
    b, dil, n_rows, _ = qkv.shape
    s = n_rows * dil
    n_blocks, n_res = DILATED_BATCHING[g]
    rows = n_blocks * 2 * B_SIDE
    assert n_rows >= 4 * B_SIDE and n_rows % rows == 0 and dil % n_res == 0
    slopes = tuple(2.0 ** (-8.0 * (g * B_HEADS_PER_GROUP + h + 1) / B_HEADS) for h in range(B_HEADS_PER_GROUP))
    kv_spec = lambda which: pl.BlockSpec((1, dil, n_rows, B_OUT_W), lambda bi, i: (bi, 0, 0, which))
    out_spec = pl.BlockSpec((1, rows * dil, B_OUT_W), lambda bi, i: (bi, i, 0))
    o, lse = pl.pallas_call(
        functools.partial(_dilated_kernel, dil=dil, slopes=slopes, n_blocks=n_blocks, n_res=n_res),
        grid=(b, n_rows // rows),
        in_specs=[pl.BlockSpec((1, dil, rows, B_OUT_W), lambda bi, i: (bi, 0, i, 0)), kv_spec(1), kv_spec(2)],
        out_specs=[out_spec, out_spec],
        out_shape=[jax.ShapeDtypeStruct((b, s, B_OUT_W), BF16), jax.ShapeDtypeStruct((b, s, B_OUT_W), F32)],
        scratch_shapes=[pltpu.VMEM((B_HEADS_PER_GROUP, rows * dil, LANES), F32)] * (2 if dil > 1 else 0),
        compiler_params=_params("parallel", "arbitrary"),
    )(qkv, qkv, qkv)
    return o.reshape(b * s, B_OUT_W), lse.reshape(b * s, B_OUT_W)


def _mix_kernel(oa_ref, o0_ref, o1_ref, o2_ref, l0_ref, l1_ref, l2_ref, ga_ref, gb_ref, wpa_ref, wpb_ref, out_ref):
    l0, l1, l2 = l0_ref[...], l1_ref[...], l2_ref[...]
    mx = jnp.maximum(jnp.maximum(l0, l1), l2)
    e0, e1, e2 = jnp.exp(l0 - mx), jnp.exp(l1 - mx), jnp.exp(l2 - mx)
    ob = (e0 * o0_ref[...] + e1 * o1_ref[...] + e2 * o2_ref[...]) / (e0 + e1 + e2)
    pa = jnp.dot(oa_ref[...], wpa_ref[...], preferred_element_type=F32)
    pb = jnp.dot(ob.astype(BF16), wpb_ref[...], preferred_element_type=F32)
    out_ref[...] = (ga_ref[...].astype(F32) * pa + gb_ref[...].astype(F32) * pb).astype(out_ref.dtype)


def _mix(out_a, o_groups, lse_groups, gates, wpa, wpb, *, tm):
    t = out_a.shape[0]
    row = lambda w, cb=0: pl.BlockSpec((tm, w), lambda i: (i, cb))
    full = lambda a: pl.BlockSpec(a.shape, lambda i: (0, 0))
    return pl.pallas_call(
        _mix_kernel,
        grid=(t // tm,),
        in_specs=[row(A_Q_W)] + [row(B_OUT_W)] * 6 + [row(D_MODEL, 0), row(D_MODEL, 1), full(wpa), full(wpb)],
        out_specs=row(D_MODEL),
        out_shape=jax.ShapeDtypeStruct((t, D_MODEL), BF16),
        compiler_params=_params("parallel"),
    )(out_a, *o_groups, *lse_groups, gates, gates, wpa, wpb)


def _out_ln_kernel(mixed_ref, x_ref, w_ref, g_ref, b_ref, o_ref):
    y = jnp.dot(mixed_ref[...], w_ref[...], preferred_element_type=F32)
    o_ref[...] = _layer_norm(ALPHA * x_ref[...] + y, g_ref[...], b_ref[...])


def _out_ln(mixed, x2d, w_out, g, b, *, tm):
    t, d = x2d.shape
    row = pl.BlockSpec((tm, d), lambda i: (i, 0))
    vec = pl.BlockSpec((1, d), lambda i: (0, 0))
    return pl.pallas_call(
        _out_ln_kernel,
        grid=(t // tm,),
        in_specs=[row, row, pl.BlockSpec((d, d), lambda i: (0, 0)), vec, vec],
        out_specs=row,
        out_shape=jax.ShapeDtypeStruct((t, d), F32),
        compiler_params=_params("parallel"),
    )(mixed, x2d, w_out, g, b)


def _ffn_kernel(x_ref, wg_ref, wu_ref, wd_ref, g_ref, b_ref, o_ref, xb_sc, acc_sc):
    f = pl.program_id(1)

    @pl.when(f == 0)
    def _():
        xb_sc[...] = x_ref[...].astype(BF16)
        acc_sc[...] = jnp.zeros(acc_sc.shape, F32)

    xb = xb_sc[...]
    gate = jnp.dot(xb, wg_ref[...], preferred_element_type=F32)
    up = jnp.dot(xb, wu_ref[...], preferred_element_type=F32)
    h = gate * (1.0 / (1.0 + jnp.exp(-gate))) * up
    acc_sc[...] += jnp.dot(h.astype(BF16), wd_ref[...], preferred_element_type=F32)

    @pl.when(f == pl.num_programs(1) - 1)
    def _():
        o_ref[...] = _layer_norm(ALPHA * x_ref[...] + acc_sc[...], g_ref[...], b_ref[...])


def _ffn(x1, wg, wu, wd, g, b, *, tm, tf):
    t, d = x1.shape
    hidden = wg.shape[1]
    row = pl.BlockSpec((tm, d), lambda i, f: (i, 0))
    vec = pl.BlockSpec((1, d), lambda i, f: (0, 0))
    return pl.pallas_call(
        _ffn_kernel,
        grid=(t // tm, hidden // tf),
        in_specs=[row,
                  pl.BlockSpec((d, tf), lambda i, f: (0, f)),
                  pl.BlockSpec((d, tf), lambda i, f: (0, f)),
                  pl.BlockSpec((tf, d), lambda i, f: (f, 0)),
                  vec, vec],
        out_specs=row,
        out_shape=jax.ShapeDtypeStruct((t, d), F32),
        scratch_shapes=[pltpu.VMEM((tm, d), BF16), pltpu.VMEM((tm, d), F32)],
        compiler_params=_params("parallel", "arbitrary"),
    )(x1, wg, wu, wd, g, b)


def _rope_tables(seq_len):
    pos = jnp.arange(seq_len)
    n_freq = ROPE_AXIS_DIM // 2
    freqs = ROPE_THETA ** (-jnp.arange(n_freq, dtype=F32) / n_freq)
    ang_r = (pos // GRID_W).astype(F32)[:, None] * freqs[None, :]
    ang_c = (pos % GRID_W).astype(F32)[:, None] * freqs[None, :]
    cos = jnp.concatenate([jnp.cos(ang_r), jnp.cos(ang_c)] * 2, axis=-1)
    sin = jnp.concatenate([-jnp.sin(ang_r), -jnp.sin(ang_c), jnp.sin(ang_r), jnp.sin(ang_c)], axis=-1)
    return cos, sin


def _layer(x, w_in, b_gate, q_norm_a, k_norm_a, w_proj_a, w_proj_b, w_out,
           ln1_g, ln1_b, w_ffn_gate, w_ffn_up, w_ffn_down, ln2_g, ln2_b, rope):
    bsz, seq_len, d = x.shape
    t = bsz * seq_len
    x2d = x.reshape(t, d)
    w_in_b = w_in.astype(BF16)
    vec = lambda a: a.reshape(1, -1).astype(F32)

    order = jnp.asarray(ROPE_LANE_ORDER)
    head_cols = lambda col0, n_heads: (col0 + HEAD_DIM * jnp.arange(n_heads)[:, None] + order[None, :]).reshape(-1)
    w_q = w_in_b[:, head_cols(COL_QA, A_Q_HEADS)]
    w_kv = jnp.concatenate([w_in_b[:, head_cols(COL_KA, A_KV_HEADS)], w_in_b[:, COL_VA:COL_VA + A_KV_W]], axis=1)
    q_gain = jnp.tile(q_norm_a[order] * (Q_SCALE * LOG2_E), A_Q_HEADS)
    kv_gain = jnp.concatenate([jnp.tile(k_norm_a[order], A_KV_HEADS), jnp.ones((A_KV_W,), F32)])
    proj = functools.partial(_in_proj, tm=1024, seq_len=seq_len)
    qa, xb = _proj_q(x2d, w_q, vec(q_gain), rope, tm=1024, tn=512, seq_len=seq_len)
    kv = proj(_proj_kv_kernel, xb, w_kv, 0, 2 * A_KV_W, [vec(kv_gain)], rope, tn=2 * A_KV_W, head_major=True)
    gates = proj(_proj_gate_kernel, xb, w_in_b, COL_GA, 2 * D_MODEL, [vec(b_gate)], tn=1024)

    out_a = _gqa(qa.reshape(bsz, seq_len, A_Q_W), kv.reshape(2 * A_KV_HEADS, bsz, seq_len, HEAD_DIM), tq=256, tk=1024)
    groups = [_dilated_group(_proj_deint(xb, w_in_b, g, tm=1024, bsz=bsz, seq_len=seq_len), g)
              for g in range(B_GROUPS)]
    mixed = _mix(out_a.reshape(t, A_Q_W), [o for o, _ in groups], [l for _, l in groups], gates,
                 w_proj_a.astype(BF16), w_proj_b.astype(BF16), tm=512)
    x1 = _out_ln(mixed, x2d, w_out.astype(BF16), vec(ln1_g), vec(ln1_b), tm=512)
    x2 = _ffn(x1, w_ffn_gate.astype(BF16), w_ffn_up.astype(BF16), w_ffn_down.astype(BF16),
              vec(ln2_g), vec(ln2_b), tm=512, tf=512)
    return x2.reshape(bsz, seq_len, d)


def kernel(x, w_in, b_gate, q_norm_a, k_norm_a, w_proj_a, w_proj_b, w_out, ln1_g, ln1_b, w_ffn_gate, w_ffn_up, w_ffn_down, ln2_g, ln2_b):
    rope = _rope_tables(x.shape[1])
    for l in range(w_in.shape[0]):
        x = _layer(x, w_in[l], b_gate[l], q_norm_a[l], k_norm_a[l], w_proj_a[l], w_proj_b[l], w_out[l],
                   ln1_g[l], ln1_b[l], w_ffn_gate[l], w_ffn_up[l], w_ffn_down[l], ln2_g[l], ln2_b[l], rope)
    return x
```

```python
import functools

import jax
import jax.numpy as jnp
import numpy as np
from jax import lax
from jax.experimental import pallas as pl
from jax.experimental.pallas import tpu as pltpu

F32 = jnp.float32
BF16 = jnp.bfloat16

D_MODEL = 2048
HEAD_DIM = 128
A_Q_HEADS = 8
A_KV_HEADS = 2
A_REP = A_Q_HEADS // A_KV_HEADS
ROPE_THETA = 10000.0
ROPE_AXIS_DIM = HEAD_DIM // 2
GRID_W = 64
B_DILATIONS = (1, 4, 16)
B_GROUPS = 3
B_HEADS_PER_GROUP = 4
B_HEADS = B_GROUPS * B_HEADS_PER_GROUP
B_SIDE = 64
A_Q_W = A_Q_HEADS * HEAD_DIM
A_KV_W = A_KV_HEADS * HEAD_DIM
B_W = B_HEADS * HEAD_DIM
B_OUT_W = B_HEADS_PER_GROUP * HEAD_DIM
FFN_HIDDEN = 5632
ALPHA = 2.0 ** 0.25
RMS_EPS = 1e-6
LN_EPS = 1e-5
Q_SCALE = HEAD_DIM ** -0.5

COL_QA = 0
COL_KA = A_Q_W
COL_VA = A_Q_W + A_KV_W
COL_QB = COL_VA + A_KV_W
COL_GA = COL_QB + 3 * B_W

VMEM_LIMIT = 60 * 1024 * 1024

NT_DIMS = (((1,), (1,)), ((), ()))
TN_DIMS = (((0,), (0,)), ((), ()))
LANES = 128
LOG2_E = 1.4426950408889634


def _params(*sem):
    return pltpu.CompilerParams(dimension_semantics=sem, vmem_limit_bytes=VMEM_LIMIT)


def _layer_norm(z, g, b):
    mu = jnp.mean(z, axis=-1, keepdims=True)
    zc = z - mu
    var = jnp.mean(zc * zc, axis=-1, keepdims=True)
    return zc * lax.rsqrt(var + LN_EPS) * g + b


ROPE_LANE_ORDER = tuple(list(range(0, 32)) + list(range(64, 96)) + list(range(32, 64)) + list(range(96, 128)))


def _norm_rope(yh, gain, cos, sin):
    mean_w = jnp.full((HEAD_DIM, HEAD_DIM), 1.0 / HEAD_DIM, BF16)
    sq = yh * yh
    hi = sq.astype(BF16)
    lo = (sq - hi.astype(F32)).astype(BF16)
    ms = jnp.dot(hi, mean_w, preferred_element_type=F32) + jnp.dot(lo, mean_w, preferred_element_type=F32)
    yn = yh * lax.rsqrt(ms + RMS_EPS) * gain
    return yn * cos + pltpu.roll(yn, HEAD_DIM // 2, 1) * sin


def _proj_q_kernel(x_ref, w_ref, gain_ref, cos_ref, sin_ref, o_ref):
    y = jnp.dot(x_ref[...], w_ref[...], preferred_element_type=F32)
    cos = cos_ref[...]
    sin = sin_ref[...]
    for h in range(y.shape[1] // HEAD_DIM):
        sl = slice(h * HEAD_DIM, (h + 1) * HEAD_DIM)
        o_ref[:, sl] = _norm_rope(y[:, sl], gain_ref[:, sl], cos, sin).astype(o_ref.dtype)


def _proj_kv_kernel(x_ref, w_ref, gain_ref, cos_ref, sin_ref, o_ref):
    y = jnp.dot(x_ref[...], w_ref[...], preferred_element_type=F32)
    cos = cos_ref[...]
    sin = sin_ref[...]
    for h in range(A_KV_HEADS):
        sl = slice(h * HEAD_DIM, (h + 1) * HEAD_DIM)
        o_ref[h] = _norm_rope(y[:, sl], gain_ref[:, sl], cos, sin).astype(o_ref.dtype)
    for h in range(A_KV_HEADS, 2 * A_KV_HEADS):
        o_ref[h] = y[:, h * HEAD_DIM:(h + 1) * HEAD_DIM].astype(o_ref.dtype)


def _proj_gate_kernel(x_ref, w_ref, b_ref, o_ref, xb_ref):
    @pl.when(pl.program_id(1) == 0)
    def _():
        xb_ref[...] = x_ref[...].astype(xb_ref.dtype)

    y = jnp.dot(xb_ref[...], w_ref[...], preferred_element_type=F32)
    o_ref[...] = (1.0 / (1.0 + jnp.exp(-(y + b_ref[...])))).astype(o_ref.dtype)


def _proj_gate(x2d, w, col_off, n_cols, bias, *, tm, tn):
    t, d = x2d.shape
    off = col_off // tn
    return pl.pallas_call(
        _proj_gate_kernel,
        grid=(t // tm, n_cols // tn),
        in_specs=[pl.BlockSpec((tm, d), lambda i, j: (i, 0)),
                  pl.BlockSpec((d, tn), lambda i, j: (0, off + j)),
                  pl.BlockSpec((1, tn), lambda i, j: (0, j))],
        out_specs=[pl.BlockSpec((tm, tn), lambda i, j: (i, j)), pl.BlockSpec((tm, d), lambda i, j: (i, 0))],
        out_shape=[jax.ShapeDtypeStruct((t, n_cols), BF16), jax.ShapeDtypeStruct((t, d), BF16)],
        compiler_params=_params("parallel", "arbitrary"),
    )(x2d, w, bias)


def _in_proj(kern, xb, w, col_off, n_cols, row_vecs, tabs=(), *, tm, tn, seq_len, head_major=False):
    t, d = xb.shape
    off = col_off // tn
    pos_blocks = seq_len // tm
    in_specs = [pl.BlockSpec((tm, d), lambda i, j: (i, 0)),
                pl.BlockSpec((d, tn), lambda i, j: (0, off + j))]
    in_specs += [pl.BlockSpec((1, tn), lambda i, j: (0, j)) for _ in row_vecs]
    in_specs += [pl.BlockSpec((tm, HEAD_DIM), lambda i, j: (i % pos_blocks, 0)) for _ in tabs]
    if head_major:
        assert n_cols == tn
        out_spec = pl.BlockSpec((tn // HEAD_DIM, tm, HEAD_DIM), lambda i, j: (0, i, 0))
        out_shape = jax.ShapeDtypeStruct((tn // HEAD_DIM, t, HEAD_DIM), BF16)
    else:
        out_spec = pl.BlockSpec((tm, tn), lambda i, j: (i, j))
        out_shape = jax.ShapeDtypeStruct((t, n_cols), BF16)
    return pl.pallas_call(
        kern,
        grid=(t // tm, n_cols // tn),
        in_specs=in_specs,
        out_specs=out_spec,
        out_shape=out_shape,
        compiler_params=_params("parallel", "arbitrary"),
    )(xb, w, *row_vecs, *tabs)


def _gqa_kernel(q_ref, k_ref, v_ref, o_ref, m_sc, l_sc, acc_sc, s0_sc, s1_sc, *, tk):
    tq = q_ref.shape[1]
    n_chunks = k_ref.shape[1] // tk
    s_sc = (s0_sc, s1_sc)
    q = jnp.concatenate([q_ref[0, :, r * HEAD_DIM:(r + 1) * HEAD_DIM] for r in range(A_REP)], axis=0)
    m_sc[...] = jnp.full(m_sc.shape, -jnp.inf, F32)
    l_sc[...] = jnp.zeros(l_sc.shape, F32)
    acc_sc[...] = jnp.zeros(acc_sc.shape, F32)

    def scores(c, slot):
        s_sc[slot][...] = lax.dot_general(k_ref[0, c * tk:(c + 1) * tk, :], q, NT_DIMS, preferred_element_type=F32)

    def softmax_pv(c, slot):
        v = v_ref[0, c * tk:(c + 1) * tk, :]
        st = s_sc[slot][...]
        m_prev = m_sc[...]
        m_new = jnp.maximum(m_prev, jnp.max(st, axis=0, keepdims=True))
        alpha = jnp.exp2(m_prev - m_new)
        pt = jnp.exp2(st - m_new)
        l_sc[...] = alpha * l_sc[...] + jnp.sum(pt, axis=0, keepdims=True)
        pv = lax.dot_general(v, pt.astype(BF16), TN_DIMS, preferred_element_type=F32)
        acc_sc[...] = alpha * acc_sc[...] + pv
        m_sc[...] = m_new

    scores(0, 0)
    for c in range(n_chunks - 1):
        scores(c + 1, (c + 1) % 2)
        softmax_pv(c, c % 2)
    softmax_pv(n_chunks - 1, (n_chunks - 1) % 2)
    ot = acc_sc[...] / l_sc[...]
    for r in range(A_REP):
        o_ref[0, :, r * HEAD_DIM:(r + 1) * HEAD_DIM] = ot[:, r * tq:(r + 1) * tq].T.astype(o_ref.dtype)


def _gqa(qa, kv, *, tq, tk):
    b, s, _ = qa.shape
    assert s % tk == 0 and s % tq == 0
    gw = A_REP * HEAD_DIM
    rows = A_REP * tq
    return pl.pallas_call(
        functools.partial(_gqa_kernel, tk=tk),
        grid=(b, A_KV_HEADS, s // tq),
        in_specs=[pl.BlockSpec((1, tq, gw), lambda bi, g, i: (bi, i, g)),
                  pl.BlockSpec((None, 1, s, HEAD_DIM), lambda bi, g, i: (g, bi, 0, 0)),
                  pl.BlockSpec((None, 1, s, HEAD_DIM), lambda bi, g, i: (A_KV_HEADS + g, bi, 0, 0))],
        out_specs=pl.BlockSpec((1, tq, gw), lambda bi, g, i: (bi, i, g)),
        out_shape=jax.ShapeDtypeStruct((b, s, A_Q_W), BF16),
        scratch_shapes=[pltpu.VMEM((1, rows), F32), pltpu.VMEM((1, rows), F32), pltpu.VMEM((HEAD_DIM, rows), F32),
                        pltpu.VMEM((tk, rows), F32), pltpu.VMEM((tk, rows), F32)],
        compiler_params=_params("parallel", "parallel", "arbitrary"),
    )(qa, kv, kv)


def _proj_deint_kernel(x_ref, wq_ref, wk_ref, wv_ref, o_ref, *stage, dil):
    x = x_ref[...]
    tm = x.shape[0]
    for which, w_ref in enumerate((wq_ref, wk_ref, wv_ref)):
        y = jnp.dot(x, w_ref[...], preferred_element_type=F32)
        if which == 0:
            y = y * Q_SCALE
        tn = y.shape[1]
        if dil == 1:
            o_ref[0, 0, :, which * tn:(which + 1) * tn] = y.astype(o_ref.dtype)
            continue
        y_sc = stage[which]
        for s in range(tn // LANES):
            y_sc[s] = y[:, s * LANES:(s + 1) * LANES]
        for r in range(dil):
            for s in range(tn // LANES):
                c0 = which * tn + s * LANES
                o_ref[0, r, :, c0:c0 + LANES] = y_sc[s, pl.ds(r, tm // dil, stride=dil), :].astype(o_ref.dtype)


def _proj_deint(xb, w, g, *, tm, bsz, seq_len):
    t, d = xb.shape
    dil = B_DILATIONS[g]
    tn = B_OUT_W
    tiles_per_seq = seq_len // tm
    col0 = COL_QB // tn
    w_spec = lambda which: pl.BlockSpec((d, tn), lambda i: (0, col0 + B_GROUPS * which + g))
    return pl.pallas_call(
        functools.partial(_proj_deint_kernel, dil=dil),
        grid=(t // tm,),
        in_specs=[pl.BlockSpec((tm, d), lambda i: (i, 0)), w_spec(0), w_spec(1), w_spec(2)],
        out_specs=pl.BlockSpec((1, dil, tm // dil, 3 * tn), lambda i: (i // tiles_per_seq, 0, i % tiles_per_seq, 0)),
        out_shape=jax.ShapeDtypeStruct((bsz, dil, seq_len // dil, 3 * tn), BF16),
        scratch_shapes=[pltpu.VMEM((tn // LANES, tm, LANES), F32)] * (3 if dil > 1 else 0),
        compiler_params=_params("parallel"),
    )(xb, w, w, w)


def _dilated_kernel(q_ref, k_ref, v_ref, o_ref, lse_ref, *token_order_scratch, dil, slopes, n_blocks, n_res):
    qb = 2 * B_SIDE
    kw = 4 * B_SIDE
    n_rows = k_ref.shape[2]
    row = lax.broadcasted_iota(jnp.int32, (qb, kw), 0)
    col = lax.broadcasted_iota(jnp.int32, (qb, kw), 1)

    def attend(r, blk, h, ks, valid, token_dist):
        sl = slice(h * HEAD_DIM, (h + 1) * HEAD_DIM)
        q = q_ref[0, r, blk * qb:(blk + 1) * qb, sl]
        k = k_ref[0, r, pl.ds(ks, kw), sl]
        v = v_ref[0, r, pl.ds(ks, kw), sl]
        s = lax.dot_general(q, k, NT_DIMS, preferred_element_type=F32) - slopes[h] * token_dist
        s = jnp.where(valid, s, -jnp.inf)
        m = jnp.max(s, axis=-1, keepdims=True)
        p = jnp.exp(s - m)
        l = jnp.sum(p, axis=-1, keepdims=True)
        o = jnp.dot(p.astype(BF16), v, preferred_element_type=F32) / l
        lse = jnp.broadcast_to(m + jnp.log(l), (qb, HEAD_DIM))
        if dil == 1:
            rows = slice(blk * qb, (blk + 1) * qb)
            o_ref[0, rows, sl] = o.astype(o_ref.dtype)
            lse_ref[0, rows, sl] = lse
        else:
            o_sc, l_sc = token_order_scratch
            o_sc[h, pl.ds(blk * qb * dil + r, qb, stride=dil), :] = o
            l_sc[h, pl.ds(blk * qb * dil + r, qb, stride=dil), :] = lse

    def residues(it, carry):
        for blk in range(n_blocks):
            q0 = (pl.program_id(1) * n_blocks + blk) * qb
            ks = pl.multiple_of(jnp.clip(q0 - B_SIDE, 0, n_rows - kw), B_SIDE)
            dist = jnp.abs(col - row + (ks - q0))
            valid = dist <= B_SIDE
            token_dist = dist.astype(F32) * float(dil)
            for u in range(n_res):
                for h in range(B_HEADS_PER_GROUP):
                    attend(it * n_res + u, blk, h, ks, valid, token_dist)
        return carry

    if dil == 1:
        residues(0, 0)
    else:
        o_sc, l_sc = token_order_scratch
        lax.fori_loop(0, dil // n_res, residues, 0)
        for h in range(B_HEADS_PER_GROUP):
            sl = slice(h * HEAD_DIM, (h + 1) * HEAD_DIM)
            o_ref[0, :, sl] = o_sc[h].astype(o_ref.dtype)
            lse_ref[0, :, sl] = l_sc[h]


DILATED_BATCHING = ((4, 1), (4, 1), (1, 4))


def _dilated_group(qkv, g):
    b, dil, n_rows, _ = qkv.shape
    s = n_rows * dil
    n_blocks, n_res = DILATED_BATCHING[g]
    rows = n_blocks * 2 * B_SIDE
    assert n_rows >= 4 * B_SIDE and n_rows % rows == 0 and dil % n_res == 0
    slopes = tuple(2.0 ** (-8.0 * (g * B_HEADS_PER_GROUP + h + 1) / B_HEADS) for h in range(B_HEADS_PER_GROUP))
    kv_spec = lambda which: pl.BlockSpec((1, dil, n_rows, B_OUT_W), lambda bi, i: (bi, 0, 0, which))
    out_spec = pl.BlockSpec((1, rows * dil, B_OUT_W), lambda bi, i: (bi, i, 0))
    o, lse = pl.pallas_call(
        functools.partial(_dilated_kernel, dil=dil, slopes=slopes, n_blocks=n_blocks, n_res=n_res),
        grid=(b, n_rows // rows),
        in_specs=[pl.BlockSpec((1, dil, rows, B_OUT_W), lambda bi, i: (bi, 0, i, 0)), kv_spec(1), kv_spec(2)],
        out_specs=[out_spec, out_spec],
        out_shape=[jax.ShapeDtypeStruct((b, s, B_OUT_W), BF16), jax.ShapeDtypeStruct((b, s, B_OUT_W), F32)],
        scratch_shapes=[pltpu.VMEM((B_HEADS_PER_GROUP, rows * dil, LANES), F32)] * (2 if dil > 1 else 0),
        compiler_params=_params("parallel", "arbitrary"),
    )(qkv, qkv, qkv)
    return o.reshape(b * s, B_OUT_W), lse.reshape(b * s, B_OUT_W)


def _mix_out_kernel(oa_ref, o0_ref, o1_ref, o2_ref, l0_ref, l1_ref, l2_ref, ga_ref, gb_ref, x_ref,
                    wpa_ref, wpb_ref, wo_ref, g_ref, b_ref, out_ref):
    l0, l1, l2 = l0_ref[...], l1_ref[...], l2_ref[...]
    mx = jnp.maximum(jnp.maximum(l0, l1), l2)
    e0, e1, e2 = jnp.exp(l0 - mx), jnp.exp(l1 - mx), jnp.exp(l2 - mx)
    ob = (e0 * o0_ref[...] + e1 * o1_ref[...] + e2 * o2_ref[...]) / (e0 + e1 + e2)
    pa = jnp.dot(oa_ref[...], wpa_ref[...], preferred_element_type=F32)
    pb = jnp.dot(ob.astype(BF16), wpb_ref[...], preferred_element_type=F32)
    mixed = (ga_ref[...].astype(F32) * pa + gb_ref[...].astype(F32) * pb).astype(BF16)
    y = jnp.dot(mixed, wo_ref[...], preferred_element_type=F32)
    out_ref[...] = _layer_norm(ALPHA * x_ref[...] + y, g_ref[...], b_ref[...])


def _mix_out(out_a, o_groups, lse_groups, gates, x2d, wpa, wpb, wo, g, b, *, tm):
    t = out_a.shape[0]
    row = lambda w, cb=0: pl.BlockSpec((tm, w), lambda i: (i, cb))
    const = lambda a: pl.BlockSpec(a.shape, lambda i: (0, 0), pipeline_mode=pl.Buffered(1))
    return pl.pallas_call(
        _mix_out_kernel,
        grid=(t // tm,),
        in_specs=[row(A_Q_W)] + [row(B_OUT_W)] * 6 + [row(D_MODEL, 0), row(D_MODEL, 1), row(D_MODEL),
                  const(wpa), const(wpb), const(wo), const(g), const(b)],
        out_specs=row(D_MODEL),
        out_shape=jax.ShapeDtypeStruct((t, D_MODEL), F32),
        compiler_params=_params("parallel"),
    )(out_a, *o_groups, *lse_groups, gates, gates, x2d, wpa, wpb, wo, g, b)


def _ffn_kernel(x_ref, wg_ref, wu_ref, wd_ref, g_ref, b_ref, o_ref, xb_sc, acc_sc):
    f = pl.program_id(1)

    @pl.when(f == 0)
    def _():
        xb_sc[...] = x_ref[...].astype(BF16)
        acc_sc[...] = jnp.zeros(acc_sc.shape, F32)

    xb = xb_sc[...]
    gate = jnp.dot(xb, wg_ref[...], preferred_element_type=F32)
    up = jnp.dot(xb, wu_ref[...], preferred_element_type=F32)
    h = gate * (1.0 / (1.0 + jnp.exp(-gate))) * up
    acc_sc[...] += jnp.dot(h.astype(BF16), wd_ref[...], preferred_element_type=F32)

    @pl.when(f == pl.num_programs(1) - 1)
    def _():
        o_ref[...] = _layer_norm(ALPHA * x_ref[...] + acc_sc[...], g_ref[...], b_ref[...])


def _ffn(x1, wg, wu, wd, g, b, *, tm, tf):
    t, d = x1.shape
    hidden = wg.shape[1]
    row = pl.BlockSpec((tm, d), lambda i, f: (i, 0))
    vec = pl.BlockSpec((1, d), lambda i, f: (0, 0))
    return pl.pallas_call(
        _ffn_kernel,
        grid=(t // tm, hidden // tf),
        in_specs=[row,
                  pl.BlockSpec((d, tf), lambda i, f: (0, f)),
                  pl.BlockSpec((d, tf), lambda i, f: (0, f)),
                  pl.BlockSpec((tf, d), lambda i, f: (f, 0)),
                  vec, vec],
        out_specs=row,
        out_shape=jax.ShapeDtypeStruct((t, d), F32),
        scratch_shapes=[pltpu.VMEM((tm, d), BF16), pltpu.VMEM((tm, d), F32)],
        compiler_params=_params("parallel", "arbitrary"),
    )(x1, wg, wu, wd, g, b)


def _rope_tables(seq_len):
    pos = np.arange(seq_len)
    n_freq = ROPE_AXIS_DIM // 2
    freqs = (np.float32(ROPE_THETA) ** (-np.arange(n_freq, dtype=np.float32) / np.float32(n_freq))).astype(np.float32)
    ang_r = (pos // GRID_W).astype(np.float32)[:, None] * freqs[None, :]
    ang_c = (pos % GRID_W).astype(np.float32)[:, None] * freqs[None, :]
    cos = np.concatenate([np.cos(ang_r), np.cos(ang_c)] * 2, axis=-1).astype(np.float32)
    sin = np.concatenate([-np.sin(ang_r), -np.sin(ang_c), np.sin(ang_r), np.sin(ang_c)], axis=-1).astype(np.float32)
    return jnp.asarray(cos), jnp.asarray(sin)


def _layer(x, w_in, b_gate, q_norm_a, k_norm_a, w_proj_a, w_proj_b, w_out,
           ln1_g, ln1_b, w_ffn_gate, w_ffn_up, w_ffn_down, ln2_g, ln2_b, rope):
    bsz, seq_len, d = x.shape
    t = bsz * seq_len
    x2d = x.reshape(t, d)
    w_in_b = w_in.astype(BF16)
    vec = lambda a: a.reshape(1, -1).astype(F32)

    order = jnp.asarray(ROPE_LANE_ORDER)
    head_cols = lambda col0, n_heads: (col0 + HEAD_DIM * jnp.arange(n_heads)[:, None] + order[None, :]).reshape(-1)
    w_q = w_in_b[:, head_cols(COL_QA, A_Q_HEADS)]
    w_kv = jnp.concatenate([w_in_b[:, head_cols(COL_KA, A_KV_HEADS)], w_in_b[:, COL_VA:COL_VA + A_KV_W]], axis=1)
    q_gain = jnp.tile(q_norm_a[order] * (Q_SCALE * LOG2_E), A_Q_HEADS)
    kv_gain = jnp.concatenate([jnp.tile(k_norm_a[order], A_KV_HEADS), jnp.ones((A_KV_W,), F32)])
    proj = functools.partial(_in_proj, tm=1024, seq_len=seq_len)
    gates, xb = _proj_gate(x2d, w_in_b, COL_GA, 2 * D_MODEL, vec(b_gate), tm=1024, tn=1024)
    qa = proj(_proj_q_kernel, xb, w_q, 0, A_Q_W, [vec(q_gain)], rope, tn=512)
    kv = proj(_proj_kv_kernel, xb, w_kv, 0, 2 * A_KV_W, [vec(kv_gain)], rope, tn=2 * A_KV_W, head_major=True)

    out_a = _gqa(qa.reshape(bsz, seq_len, A_Q_W), kv.reshape(2 * A_KV_HEADS, bsz, seq_len, HEAD_DIM), tq=256, tk=1024)
    groups = [_dilated_group(_proj_deint(xb, w_in_b, g, tm=1024, bsz=bsz, seq_len=seq_len), g)
              for g in range(B_GROUPS)]
    x1 = _mix_out(out_a.reshape(t, A_Q_W), [o for o, _ in groups], [l for _, l in groups], gates, x2d,
                  w_proj_a.astype(BF16), w_proj_b.astype(BF16), w_out.astype(BF16), vec(ln1_g), vec(ln1_b), tm=512)
    x2 = _ffn(x1, w_ffn_gate.astype(BF16), w_ffn_up.astype(BF16), w_ffn_down.astype(BF16),
              vec(ln2_g), vec(ln2_b), tm=512, tf=512)
    return x2.reshape(bsz, seq_len, d)


def kernel(x, w_in, b_gate, q_norm_a, k_norm_a, w_proj_a, w_proj_b, w_out, ln1_g, ln1_b, w_ffn_gate, w_ffn_up, w_ffn_down, ln2_g, ln2_b):
    rope = _rope_tables(x.shape[1])
    for l in range(w_in.shape[0]):
        x = _layer(x, w_in[l], b_gate[l], q_norm_a[l], k_norm_a[l], w_proj_a[l], w_proj_b[l], w_out[l],
                   ln1_g[l], ln1_b[l], w_ffn_gate[l], w_ffn_up[l], w_ffn_down[l], ln2_g[l], ln2_b[l], rope)
    return x
```

```python
import functools

import jax
import jax.numpy as jnp
import numpy as np
from jax import lax
from jax.experimental import pallas as pl
from jax.experimental.pallas import tpu as pltpu

F32 = jnp.float32
BF16 = jnp.bfloat16

D_MODEL = 2048
HEAD_DIM = 128
A_Q_HEADS = 8
A_KV_HEADS = 2
A_REP = A_Q_HEADS // A_KV_HEADS
ROPE_THETA = 10000.0
ROPE_AXIS_DIM = HEAD_DIM // 2
GRID_W = 64
B_DILATIONS = (1, 4, 16)
B_GROUPS = 3
B_HEADS_PER_GROUP = 4
B_HEADS = B_GROUPS * B_HEADS_PER_GROUP
B_SIDE = 64
A_Q_W = A_Q_HEADS * HEAD_DIM
A_KV_W = A_KV_HEADS * HEAD_DIM
B_W = B_HEADS * HEAD_DIM
B_OUT_W = B_HEADS_PER_GROUP * HEAD_DIM
FFN_HIDDEN = 5632
ALPHA = 2.0 ** 0.25
RMS_EPS = 1e-6
LN_EPS = 1e-5
Q_SCALE = HEAD_DIM ** -0.5

COL_QA = 0
COL_KA = A_Q_W
COL_VA = A_Q_W + A_KV_W
COL_QB = COL_VA + A_KV_W
COL_GA = COL_QB + 3 * B_W

VMEM_LIMIT = 60 * 1024 * 1024

NT_DIMS = (((1,), (1,)), ((), ()))
TN_DIMS = (((0,), (0,)), ((), ()))
LANES = 128
FAST_STRIDE = 4
LOG2_E = 1.4426950408889634


def _params(*sem):
    return pltpu.CompilerParams(dimension_semantics=sem, vmem_limit_bytes=VMEM_LIMIT)


def _layer_norm(z, g, b):
    mu = jnp.mean(z, axis=-1, keepdims=True)
    zc = z - mu
    var = jnp.mean(zc * zc, axis=-1, keepdims=True)
    return zc * lax.rsqrt(var + LN_EPS) * g + b


ROPE_LANE_ORDER = tuple(list(range(0, 32)) + list(range(64, 96)) + list(range(32, 64)) + list(range(96, 128)))


def _norm_rope(yh, gain, cos, sin):
    mean_w = jnp.full((HEAD_DIM, HEAD_DIM), 1.0 / HEAD_DIM, BF16)
    sq = yh * yh
    hi = sq.astype(BF16)
    lo = (sq - hi.astype(F32)).astype(BF16)
    ms = jnp.dot(hi, mean_w, preferred_element_type=F32) + jnp.dot(lo, mean_w, preferred_element_type=F32)
    yn = yh * lax.rsqrt(ms + RMS_EPS) * gain
    return yn * cos + pltpu.roll(yn, HEAD_DIM // 2, 1) * sin


def _proj_q_kernel(x_ref, w_ref, gain_ref, cos_ref, sin_ref, o_ref):
    y = jnp.dot(x_ref[...], w_ref[...], preferred_element_type=F32)
    cos = cos_ref[...]
    sin = sin_ref[...]
    for h in range(y.shape[1] // HEAD_DIM):
        sl = slice(h * HEAD_DIM, (h + 1) * HEAD_DIM)
        o_ref[:, sl] = _norm_rope(y[:, sl], gain_ref[:, sl], cos, sin).astype(o_ref.dtype)


def _proj_kv_kernel(x_ref, w_ref, gain_ref, cos_ref, sin_ref, o_ref):
    y = jnp.dot(x_ref[...], w_ref[...], preferred_element_type=F32)
    cos = cos_ref[...]
    sin = sin_ref[...]
    for h in range(A_KV_HEADS):
        sl = slice(h * HEAD_DIM, (h + 1) * HEAD_DIM)
        o_ref[h] = _norm_rope(y[:, sl], gain_ref[:, sl], cos, sin).astype(o_ref.dtype)
    for h in range(A_KV_HEADS, 2 * A_KV_HEADS):
        o_ref[h] = y[:, h * HEAD_DIM:(h + 1) * HEAD_DIM].astype(o_ref.dtype)


def _proj_gate_kernel(x_ref, w_ref, b_ref, o_ref, xb_ref):
    @pl.when(pl.program_id(1) == 0)
    def _():
        xb_ref[...] = x_ref[...].astype(xb_ref.dtype)

    y = jnp.dot(xb_ref[...], w_ref[...], preferred_element_type=F32)
    o_ref[...] = (0.5 * jnp.tanh(0.5 * (y + b_ref[...])) + 0.5).astype(o_ref.dtype)


def _proj_gate(x2d, w, col_off, n_cols, bias, *, tm, tn):
    t, d = x2d.shape
    off = col_off // tn
    return pl.pallas_call(
        _proj_gate_kernel,
        grid=(t // tm, n_cols // tn),
        in_specs=[pl.BlockSpec((tm, d), lambda i, j: (i, 0)),
                  pl.BlockSpec((d, tn), lambda i, j: (0, off + j)),
                  pl.BlockSpec((1, tn), lambda i, j: (0, j))],
        out_specs=[pl.BlockSpec((tm, tn), lambda i, j: (i, j)), pl.BlockSpec((tm, d), lambda i, j: (i, 0))],
        out_shape=[jax.ShapeDtypeStruct((t, n_cols), BF16), jax.ShapeDtypeStruct((t, d), BF16)],
        compiler_params=_params("parallel", "arbitrary"),
    )(x2d, w, bias)


def _in_proj(kern, xb, w, col_off, n_cols, row_vecs, tabs=(), *, tm, tn, seq_len, head_major=False):
    t, d = xb.shape
    off = col_off // tn
    pos_blocks = seq_len // tm
    in_specs = [pl.BlockSpec((tm, d), lambda i, j: (i, 0)),
                pl.BlockSpec((d, tn), lambda i, j: (0, off + j))]
    in_specs += [pl.BlockSpec((1, tn), lambda i, j: (0, j)) for _ in row_vecs]
    in_specs += [pl.BlockSpec((tm, HEAD_DIM), lambda i, j: (i % pos_blocks, 0)) for _ in tabs]
    if head_major:
        assert n_cols == tn
        out_spec = pl.BlockSpec((tn // HEAD_DIM, tm, HEAD_DIM), lambda i, j: (0, i, 0))
        out_shape = jax.ShapeDtypeStruct((tn // HEAD_DIM, t, HEAD_DIM), BF16)
    else:
        out_spec = pl.BlockSpec((tm, tn), lambda i, j: (i, j))
        out_shape = jax.ShapeDtypeStruct((t, n_cols), BF16)
    return pl.pallas_call(
        kern,
        grid=(t // tm, n_cols // tn),
        in_specs=in_specs,
        out_specs=out_spec,
        out_shape=out_shape,
        compiler_params=_params("parallel", "arbitrary"),
    )(xb, w, *row_vecs, *tabs)


def _gqa_kernel(q_ref, k_ref, v_ref, o_ref, m_sc, l_sc, acc_sc, s0_sc, s1_sc, *, tk):
    tq = q_ref.shape[1]
    n_chunks = k_ref.shape[1] // tk
    s_sc = (s0_sc, s1_sc)
    q = jnp.concatenate([q_ref[0, :, r * HEAD_DIM:(r + 1) * HEAD_DIM] for r in range(A_REP)], axis=0)
    m_sc[...] = jnp.full(m_sc.shape, -jnp.inf, F32)
    l_sc[...] = jnp.zeros(l_sc.shape, F32)
    acc_sc[...] = jnp.zeros(acc_sc.shape, F32)

    def scores(c, slot):
        s_sc[slot][...] = lax.dot_general(k_ref[0, c * tk:(c + 1) * tk, :], q, NT_DIMS, preferred_element_type=F32)

    def softmax_pv(c, slot):
        v = v_ref[0, c * tk:(c + 1) * tk, :]
        st = s_sc[slot][...]
        m_prev = m_sc[...]
        m_new = jnp.maximum(m_prev, jnp.max(st, axis=0, keepdims=True))
        alpha = jnp.exp2(m_prev - m_new)
        pt = jnp.exp2(st - m_new)
        l_sc[...] = alpha * l_sc[...] + jnp.sum(pt, axis=0, keepdims=True)
        pv = lax.dot_general(v, pt.astype(BF16), TN_DIMS, preferred_element_type=F32)
        acc_sc[...] = alpha * acc_sc[...] + pv
        m_sc[...] = m_new

    scores(0, 0)
    for c in range(n_chunks - 1):
        scores(c + 1, (c + 1) % 2)
        softmax_pv(c, c % 2)
    softmax_pv(n_chunks - 1, (n_chunks - 1) % 2)
    ot = acc_sc[...] / l_sc[...]
    for r in range(A_REP):
        o_ref[0, :, r * HEAD_DIM:(r + 1) * HEAD_DIM] = ot[:, r * tq:(r + 1) * tq].T.astype(o_ref.dtype)


def _gqa(qa, kv, *, tq, tk):
    b, s, _ = qa.shape
    assert s % tk == 0 and s % tq == 0
    gw = A_REP * HEAD_DIM
    rows = A_REP * tq
    return pl.pallas_call(
        functools.partial(_gqa_kernel, tk=tk),
        grid=(b, A_KV_HEADS, s // tq),
        in_specs=[pl.BlockSpec((1, tq, gw), lambda bi, g, i: (bi, i, g)),
                  pl.BlockSpec((None, 1, s, HEAD_DIM), lambda bi, g, i: (g, bi, 0, 0)),
                  pl.BlockSpec((None, 1, s, HEAD_DIM), lambda bi, g, i: (A_KV_HEADS + g, bi, 0, 0))],
        out_specs=pl.BlockSpec((1, tq, gw), lambda bi, g, i: (bi, i, g)),
        out_shape=jax.ShapeDtypeStruct((b, s, A_Q_W), BF16),
        scratch_shapes=[pltpu.VMEM((1, rows), F32), pltpu.VMEM((1, rows), F32), pltpu.VMEM((HEAD_DIM, rows), F32),
                        pltpu.VMEM((tk, rows), F32), pltpu.VMEM((tk, rows), F32)],
        compiler_params=_params("parallel", "parallel", "arbitrary"),
    )(qa, kv, kv)


def _proj_deint_kernel(x_ref, wq_ref, wk_ref, wv_ref, o_ref, *stage, dil):
    x = x_ref[...]
    tm = x.shape[0]
    for which, w_ref in enumerate((wq_ref, wk_ref, wv_ref)):
        y = jnp.dot(x, w_ref[...], preferred_element_type=F32)
        if which == 0:
            y = y * Q_SCALE
        tn = y.shape[1]
        if dil == 1:
            o_ref[0, 0, :, which * tn:(which + 1) * tn] = y.astype(o_ref.dtype)
            continue
        y_sc = stage[which]
        n_slabs = tn // LANES
        for s in range(n_slabs):
            y_sc[s] = y[:, s * LANES:(s + 1) * LANES]
        if dil == FAST_STRIDE ** 2:
            z_sc = stage[3 + which]
            part = tm // FAST_STRIDE
            for lo in range(FAST_STRIDE):
                for s in range(n_slabs):
                    z_sc[s, lo * part:(lo + 1) * part, :] = y_sc[s, pl.ds(lo, part, stride=FAST_STRIDE), :]
            for lo in range(FAST_STRIDE):
                for hi in range(FAST_STRIDE):
                    for s in range(n_slabs):
                        c0 = which * tn + s * LANES
                        rows = z_sc[s, pl.ds(lo * part + hi, tm // dil, stride=FAST_STRIDE), :]
                        o_ref[0, lo + FAST_STRIDE * hi, :, c0:c0 + LANES] = rows.astype(o_ref.dtype)
            continue
        for r in range(dil):
            for s in range(n_slabs):
                c0 = which * tn + s * LANES
                o_ref[0, r, :, c0:c0 + LANES] = y_sc[s, pl.ds(r, tm // dil, stride=dil), :].astype(o_ref.dtype)


def _proj_deint(xb, w, g, *, tm, bsz, seq_len):
    t, d = xb.shape
    dil = B_DILATIONS[g]
    tn = B_OUT_W
    tiles_per_seq = seq_len // tm
    col0 = COL_QB // tn
    w_spec = lambda which: pl.BlockSpec((d, tn), lambda i: (0, col0 + B_GROUPS * which + g))
    return pl.pallas_call(
        functools.partial(_proj_deint_kernel, dil=dil),
        grid=(t // tm,),
        in_specs=[pl.BlockSpec((tm, d), lambda i: (i, 0)), w_spec(0), w_spec(1), w_spec(2)],
        out_specs=pl.BlockSpec((1, dil, tm // dil, 3 * tn), lambda i: (i // tiles_per_seq, 0, i % tiles_per_seq, 0)),
        out_shape=jax.ShapeDtypeStruct((bsz, dil, seq_len // dil, 3 * tn), BF16),
        scratch_shapes=[pltpu.VMEM((tn // LANES, tm, LANES), F32)] * {1: 0, FAST_STRIDE ** 2: 6}.get(dil, 3),
        compiler_params=_params("parallel"),
    )(xb, w, w, w)


def _dilated_kernel(q_ref, k_ref, v_ref, o_ref, lse_ref, *token_order_scratch, dil, slopes, n_blocks, n_res):
    qb = 2 * B_SIDE
    kw = 4 * B_SIDE
    n_rows = k_ref.shape[2]
    row = lax.broadcasted_iota(jnp.int32, (qb, kw), 0)
    col = lax.broadcasted_iota(jnp.int32, (qb, kw), 1)

    def attend(r, blk, h, ks, valid, token_dist):
        sl = slice(h * HEAD_DIM, (h + 1) * HEAD_DIM)
        q = q_ref[0, r, blk * qb:(blk + 1) * qb, sl]
        k = k_ref[0, r, pl.ds(ks, kw), sl]
        v = v_ref[0, r, pl.ds(ks, kw), sl]
        s = lax.dot_general(q, k, NT_DIMS, preferred_element_type=F32) - slopes[h] * token_dist
        s = jnp.where(valid, s, -jnp.inf)
        m = jnp.max(s, axis=-1, keepdims=True)
        p = jnp.exp(s - m)
        l = jnp.sum(p, axis=-1, keepdims=True)
        o = jnp.dot(p.astype(BF16), v, preferred_element_type=F32) / l
        lse = jnp.broadcast_to(m + jnp.log(l), (qb, HEAD_DIM))
        if dil == 1:
            rows = slice(blk * qb, (blk + 1) * qb)
            o_ref[0, rows, sl] = o.astype(o_ref.dtype)
            lse_ref[0, rows, sl] = lse
        else:
            o_sc, l_sc = token_order_scratch
            o_sc[h, pl.ds(blk * qb * dil + r, qb, stride=dil), :] = o
            l_sc[h, pl.ds(blk * qb * dil + r, qb, stride=dil), :] = lse

    def residues(it, carry):
        for blk in range(n_blocks):
            q0 = (pl.program_id(1) * n_blocks + blk) * qb
            ks = pl.multiple_of(jnp.clip(q0 - B_SIDE, 0, n_rows - kw), B_SIDE)
            dist = jnp.abs(col - row + (ks - q0))
            valid = dist <= B_SIDE
            token_dist = dist.astype(F32) * float(dil)
            for u in range(n_res):
                for h in range(B_HEADS_PER_GROUP):
                    attend(it * n_res + u, blk, h, ks, valid, token_dist)
        return carry

    if dil == 1:
        residues(0, 0)
    else:
        o_sc, l_sc = token_order_scratch
        lax.fori_loop(0, dil // n_res, residues, 0)
        for h in range(B_HEADS_PER_GROUP):
            sl = slice(h * HEAD_DIM, (h + 1) * HEAD_DIM)
            o_ref[0, :, sl] = o_sc[h].astype(o_ref.dtype)
            lse_ref[0, :, sl] = l_sc[h]


DILATED_BATCHING = ((4, 1), (4, 1), (1, 4))


def _dilated_group(qkv, g):
    b, dil, n_rows, _ = qkv.shape
    s = n_rows * dil
    n_blocks, n_res = DILATED_BATCHING[g]
    rows = n_blocks * 2 * B_SIDE
    assert n_rows >= 4 * B_SIDE and n_rows % rows == 0 and dil % n_res == 0
    slopes = tuple(2.0 ** (-8.0 * (g * B_HEADS_PER_GROUP + h + 1) / B_HEADS) for h in range(B_HEADS_PER_GROUP))
    kv_spec = lambda which: pl.BlockSpec((1, dil, n_rows, B_OUT_W), lambda bi, i: (bi, 0, 0, which))
    out_spec = pl.BlockSpec((1, rows * dil, B_OUT_W), lambda bi, i: (bi, i, 0))
    o, lse = pl.pallas_call(
        functools.partial(_dilated_kernel, dil=dil, slopes=slopes, n_blocks=n_blocks, n_res=n_res),
        grid=(b, n_rows // rows),
        in_specs=[pl.BlockSpec((1, dil, rows, B_OUT_W), lambda bi, i: (bi, 0, i, 0)), kv_spec(1), kv_spec(2)],
        out_specs=[out_spec, out_spec],
        out_shape=[jax.ShapeDtypeStruct((b, s, B_OUT_W), BF16), jax.ShapeDtypeStruct((b, s, B_OUT_W), F32)],
        scratch_shapes=[pltpu.VMEM((B_HEADS_PER_GROUP, rows * dil, LANES), F32)] * (2 if dil > 1 else 0),
        compiler_params=_params("parallel", "arbitrary"),
    )(qkv, qkv, qkv)
    return o.reshape(b * s, B_OUT_W), lse.reshape(b * s, B_OUT_W)


def _mix_out_kernel(oa_ref, o0_ref, o1_ref, o2_ref, l0_ref, l1_ref, l2_ref, ga_ref, gb_ref, x_ref,
                    wpa_ref, wpb_ref, wo_ref, g_ref, b_ref, out_ref):
    l0, l1, l2 = l0_ref[...], l1_ref[...], l2_ref[...]
    mx = jnp.maximum(jnp.maximum(l0, l1), l2)
    e0, e1, e2 = jnp.exp(l0 - mx), jnp.exp(l1 - mx), jnp.exp(l2 - mx)
    ob = (e0 * o0_ref[...] + e1 * o1_ref[...] + e2 * o2_ref[...]) / (e0 + e1 + e2)
    pa = jnp.dot(oa_ref[...], wpa_ref[...], preferred_element_type=F32)
    pb = jnp.dot(ob.astype(BF16), wpb_ref[...], preferred_element_type=F32)
    mixed = (ga_ref[...].astype(F32) * pa + gb_ref[...].astype(F32) * pb).astype(BF16)
    y = jnp.dot(mixed, wo_ref[...], preferred_element_type=F32)
    out_ref[...] = _layer_norm(ALPHA * x_ref[...] + y, g_ref[...], b_ref[...])


def _mix_out(out_a, o_groups, lse_groups, gates, x2d, wpa, wpb, wo, g, b, *, tm):
    t = out_a.shape[0]
    row = lambda w, cb=0: pl.BlockSpec((tm, w), lambda i: (i, cb))
    const = lambda a: pl.BlockSpec(a.shape, lambda i: (0, 0), pipeline_mode=pl.Buffered(1))
    return pl.pallas_call(
        _mix_out_kernel,
        grid=(t // tm,),
        in_specs=[row(A_Q_W)] + [row(B_OUT_W)] * 6 + [row(D_MODEL, 0), row(D_MODEL, 1), row(D_MODEL),
                  const(wpa), const(wpb), const(wo), const(g), const(b)],
        out_specs=row(D_MODEL),
        out_shape=jax.ShapeDtypeStruct((t, D_MODEL), F32),
        compiler_params=_params("parallel"),
    )(out_a, *o_groups, *lse_groups, gates, gates, x2d, wpa, wpb, wo, g, b)


def _ffn_kernel(x_ref, wg_ref, wu_ref, wd_ref, g_ref, b_ref, o_ref, xb_sc, acc_sc):
    f = pl.program_id(1)

    @pl.when(f == 0)
    def _():
        xb_sc[...] = x_ref[...].astype(BF16)
        acc_sc[...] = jnp.zeros(acc_sc.shape, F32)

    xb = xb_sc[...]
    gate = jnp.dot(xb, wg_ref[...], preferred_element_type=F32)
    up = jnp.dot(xb, wu_ref[...], preferred_element_type=F32)
    h = gate * (1.0 / (1.0 + jnp.exp(-gate))) * up
    acc_sc[...] += jnp.dot(h.astype(BF16), wd_ref[...], preferred_element_type=F32)

    @pl.when(f == pl.num_programs(1) - 1)
    def _():
        o_ref[...] = _layer_norm(ALPHA * x_ref[...] + acc_sc[...], g_ref[...], b_ref[...])


def _ffn(x1, wg, wu, wd, g, b, *, tm, tf):
    t, d = x1.shape
    hidden = wg.shape[1]
    row = pl.BlockSpec((tm, d), lambda i, f: (i, 0))
    vec = pl.BlockSpec((1, d), lambda i, f: (0, 0))
    return pl.pallas_call(
        _ffn_kernel,
        grid=(t // tm, hidden // tf),
        in_specs=[row,
                  pl.BlockSpec((d, tf), lambda i, f: (0, f)),
                  pl.BlockSpec((d, tf), lambda i, f: (0, f)),
                  pl.BlockSpec((tf, d), lambda i, f: (f, 0)),
                  vec, vec],
        out_specs=row,
        out_shape=jax.ShapeDtypeStruct((t, d), F32),
        scratch_shapes=[pltpu.VMEM((tm, d), BF16), pltpu.VMEM((tm, d), F32)],
        compiler_params=_params("parallel", "arbitrary"),
    )(x1, wg, wu, wd, g, b)


def _rope_tables(seq_len):
    pos = np.arange(seq_len)
    n_freq = ROPE_AXIS_DIM // 2
    freqs = (np.float32(ROPE_THETA) ** (-np.arange(n_freq, dtype=np.float32) / np.float32(n_freq))).astype(np.float32)
    ang_r = (pos // GRID_W).astype(np.float32)[:, None] * freqs[None, :]
    ang_c = (pos % GRID_W).astype(np.float32)[:, None] * freqs[None, :]
    cos = np.concatenate([np.cos(ang_r), np.cos(ang_c)] * 2, axis=-1).astype(np.float32)
    sin = np.concatenate([-np.sin(ang_r), -np.sin(ang_c), np.sin(ang_r), np.sin(ang_c)], axis=-1).astype(np.float32)
    return jnp.asarray(cos), jnp.asarray(sin)


def _layer(x, w_in, b_gate, q_norm_a, k_norm_a, w_proj_a, w_proj_b, w_out,
           ln1_g, ln1_b, w_ffn_gate, w_ffn_up, w_ffn_down, ln2_g, ln2_b, rope):
    bsz, seq_len, d = x.shape
    t = bsz * seq_len
    x2d = x.reshape(t, d)
    w_in_b = w_in.astype(BF16)
    vec = lambda a: a.reshape(1, -1).astype(F32)

    order = jnp.asarray(ROPE_LANE_ORDER)
    head_cols = lambda col0, n_heads: (col0 + HEAD_DIM * jnp.arange(n_heads)[:, None] + order[None, :]).reshape(-1)
    w_q = w_in_b[:, head_cols(COL_QA, A_Q_HEADS)]
    w_kv = jnp.concatenate([w_in_b[:, head_cols(COL_KA, A_KV_HEADS)], w_in_b[:, COL_VA:COL_VA + A_KV_W]], axis=1)
    q_gain = jnp.tile(q_norm_a[order] * (Q_SCALE * LOG2_E), A_Q_HEADS)
    kv_gain = jnp.concatenate([jnp.tile(k_norm_a[order], A_KV_HEADS), jnp.ones((A_KV_W,), F32)])
    proj = functools.partial(_in_proj, tm=1024, seq_len=seq_len)
    gates, xb = _proj_gate(x2d, w_in_b, COL_GA, 2 * D_MODEL, vec(b_gate), tm=1024, tn=1024)
    qa = proj(_proj_q_kernel, xb, w_q, 0, A_Q_W, [vec(q_gain)], rope, tn=512)
    kv = proj(_proj_kv_kernel, xb, w_kv, 0, 2 * A_KV_W, [vec(kv_gain)], rope, tn=2 * A_KV_W, head_major=True)

    out_a = _gqa(qa.reshape(bsz, seq_len, A_Q_W), kv.reshape(2 * A_KV_HEADS, bsz, seq_len, HEAD_DIM), tq=256, tk=1024)
    groups = [_dilated_group(_proj_deint(xb, w_in_b, g, tm=1024, bsz=bsz, seq_len=seq_len), g)
              for g in range(B_GROUPS)]
    x1 = _mix_out(out_a.reshape(t, A_Q_W), [o for o, _ in groups], [l for _, l in groups], gates, x2d,
                  w_proj_a.astype(BF16), w_proj_b.astype(BF16), w_out.astype(BF16), vec(ln1_g), vec(ln1_b), tm=512)
    x2 = _ffn(x1, w_ffn_gate.astype(BF16), w_ffn_up.astype(BF16), w_ffn_down.astype(BF16),
              vec(ln2_g), vec(ln2_b), tm=512, tf=512)
    return x2.reshape(bsz, seq_len, d)


def kernel(x, w_in, b_gate, q_norm_a, k_norm_a, w_proj_a, w_proj_b, w_out, ln1_g, ln1_b, w_ffn_gate, w_ffn_up, w_ffn_down, ln2_g, ln2_b):
    rope = _rope_tables(x.shape[1])
    for l in range(w_in.shape[0]):
        x = _layer(x, w_in[l], b_gate[l], q_norm_a[l], k_norm_a[l], w_proj_a[l], w_proj_b[l], w_out[l],
                   ln1_g[l], ln1_b[l], w_ffn_gate[l], w_ffn_up[l], w_ffn_down[l], ln2_g[l], ln2_b[l], rope)
    return x
```

```python
import functools

import jax
import jax.numpy as jnp
import numpy as np
from jax import lax
from jax.experimental import pallas as pl
from jax.experimental.pallas import tpu as pltpu

F32 = jnp.float32
BF16 = jnp.bfloat16

D_MODEL = 2048
HEAD_DIM = 128
A_Q_HEADS = 8
A_KV_HEADS = 2
A_REP = A_Q_HEADS // A_KV_HEADS
ROPE_THETA = 10000.0
ROPE_AXIS_DIM = HEAD_DIM // 2
GRID_W = 64
B_DILATIONS = (1, 4, 16)
B_GROUPS = 3
B_HEADS_PER_GROUP = 4
B_HEADS = B_GROUPS * B_HEADS_PER_GROUP
B_SIDE = 64
A_Q_W = A_Q_HEADS * HEAD_DIM
A_KV_W = A_KV_HEADS * HEAD_DIM
B_W = B_HEADS * HEAD_DIM
B_OUT_W = B_HEADS_PER_GROUP * HEAD_DIM
FFN_HIDDEN = 5632
ALPHA = 2.0 ** 0.25
RMS_EPS = 1e-6
LN_EPS = 1e-5
Q_SCALE = HEAD_DIM ** -0.5

COL_QA = 0
COL_KA = A_Q_W
COL_VA = A_Q_W + A_KV_W
COL_QB = COL_VA + A_KV_W
COL_GA = COL_QB + 3 * B_W

VMEM_LIMIT = 60 * 1024 * 1024
FFN_VMEM_LIMIT = 63 * 1024 * 1024

NT_DIMS = (((1,), (1,)), ((), ()))
TN_DIMS = (((0,), (0,)), ((), ()))
LANES = 128
FAST_STRIDE = 4
LOG2_E = 1.4426950408889634


def _params(*sem, vmem_limit=VMEM_LIMIT):
    return pltpu.CompilerParams(dimension_semantics=sem, vmem_limit_bytes=vmem_limit)


def _layer_norm(z, g, b):
    mu = jnp.mean(z, axis=-1, keepdims=True)
    zc = z - mu
    var = jnp.mean(zc * zc, axis=-1, keepdims=True)
    return zc * lax.rsqrt(var + LN_EPS) * g + b


ROPE_LANE_ORDER = tuple(list(range(0, 32)) + list(range(64, 96)) + list(range(32, 64)) + list(range(96, 128)))


def _norm_rope(yh, gain, cos, sin):
    mean_w = jnp.full((HEAD_DIM, HEAD_DIM), 1.0 / HEAD_DIM, BF16)
    sq = yh * yh
    hi = sq.astype(BF16)
    lo = (sq - hi.astype(F32)).astype(BF16)
    ms = jnp.dot(hi, mean_w, preferred_element_type=F32) + jnp.dot(lo, mean_w, preferred_element_type=F32)
    yn = yh * lax.rsqrt(ms + RMS_EPS) * gain
    return yn * cos + pltpu.roll(yn, HEAD_DIM // 2, 1) * sin


def _proj_q_kernel(x_ref, w_ref, gain_ref, cos_ref, sin_ref, o_ref):
    y = jnp.dot(x_ref[...], w_ref[...], preferred_element_type=F32)
    cos = cos_ref[...]
    sin = sin_ref[...]
    for h in range(y.shape[1] // HEAD_DIM):
        sl = slice(h * HEAD_DIM, (h + 1) * HEAD_DIM)
        o_ref[:, sl] = _norm_rope(y[:, sl], gain_ref[:, sl], cos, sin).astype(o_ref.dtype)


def _proj_kv_kernel(x_ref, w_ref, gain_ref, cos_ref, sin_ref, o_ref):
    y = jnp.dot(x_ref[...], w_ref[...], preferred_element_type=F32)
    cos = cos_ref[...]
    sin = sin_ref[...]
    for h in range(A_KV_HEADS):
        sl = slice(h * HEAD_DIM, (h + 1) * HEAD_DIM)
        o_ref[h] = _norm_rope(y[:, sl], gain_ref[:, sl], cos, sin).astype(o_ref.dtype)
    for h in range(A_KV_HEADS, 2 * A_KV_HEADS):
        o_ref[h] = y[:, h * HEAD_DIM:(h + 1) * HEAD_DIM].astype(o_ref.dtype)


def _proj_gate_kernel(x_ref, w_ref, b_ref, o_ref, xb_ref):
    @pl.when(pl.program_id(1) == 0)
    def _():
        xb_ref[...] = x_ref[...].astype(xb_ref.dtype)

    y = jnp.dot(xb_ref[...], w_ref[...], preferred_element_type=F32)
    o_ref[...] = (0.5 * jnp.tanh(0.5 * (y + b_ref[...])) + 0.5).astype(o_ref.dtype)


def _proj_gate(x2d, w, col_off, n_cols, bias, *, tm, tn):
    t, d = x2d.shape
    off = col_off // tn
    return pl.pallas_call(
        _proj_gate_kernel,
        grid=(t // tm, n_cols // tn),
        in_specs=[pl.BlockSpec((tm, d), lambda i, j: (i, 0)),
                  pl.BlockSpec((d, tn), lambda i, j: (0, off + j)),
                  pl.BlockSpec((1, tn), lambda i, j: (0, j))],
        out_specs=[pl.BlockSpec((tm, tn), lambda i, j: (i, j)), pl.BlockSpec((tm, d), lambda i, j: (i, 0))],
        out_shape=[jax.ShapeDtypeStruct((t, n_cols), BF16), jax.ShapeDtypeStruct((t, d), BF16)],
        compiler_params=_params("parallel", "arbitrary"),
    )(x2d, w, bias)


def _in_proj(kern, xb, w, col_off, n_cols, row_vecs, tabs=(), *, tm, tn, seq_len, head_major=False):
    t, d = xb.shape
    off = col_off // tn
    pos_blocks = seq_len // tm
    in_specs = [pl.BlockSpec((tm, d), lambda i, j: (i, 0)),
                pl.BlockSpec((d, tn), lambda i, j: (0, off + j))]
    in_specs += [pl.BlockSpec((1, tn), lambda i, j: (0, j)) for _ in row_vecs]
    in_specs += [pl.BlockSpec((tm, HEAD_DIM), lambda i, j: (i % pos_blocks, 0)) for _ in tabs]
    if head_major:
        assert n_cols == tn
        out_spec = pl.BlockSpec((tn // HEAD_DIM, tm, HEAD_DIM), lambda i, j: (0, i, 0))
        out_shape = jax.ShapeDtypeStruct((tn // HEAD_DIM, t, HEAD_DIM), BF16)
    else:
        out_spec = pl.BlockSpec((tm, tn), lambda i, j: (i, j))
        out_shape = jax.ShapeDtypeStruct((t, n_cols), BF16)
    return pl.pallas_call(
        kern,
        grid=(t // tm, n_cols // tn),
        in_specs=in_specs,
        out_specs=out_spec,
        out_shape=out_shape,
        compiler_params=_params("parallel", "arbitrary"),
    )(xb, w, *row_vecs, *tabs)


def _gqa_kernel(q_ref, k_ref, v_ref, o_ref, m_sc, l_sc, acc_sc, s0_sc, s1_sc, *, tk):
    tq = q_ref.shape[1]
    n_chunks = k_ref.shape[1] // tk
    s_sc = (s0_sc, s1_sc)
    q = jnp.concatenate([q_ref[0, :, r * HEAD_DIM:(r + 1) * HEAD_DIM] for r in range(A_REP)], axis=0)
    m_sc[...] = jnp.full(m_sc.shape, -jnp.inf, F32)
    l_sc[...] = jnp.zeros(l_sc.shape, F32)
    acc_sc[...] = jnp.zeros(acc_sc.shape, F32)

    def scores(c, slot):
        s_sc[slot][...] = lax.dot_general(k_ref[0, c * tk:(c + 1) * tk, :], q, NT_DIMS, preferred_element_type=F32)

    def softmax_pv(c, slot):
        v = v_ref[0, c * tk:(c + 1) * tk, :]
        st = s_sc[slot][...]
        m_prev = m_sc[...]
        m_new = jnp.maximum(m_prev, jnp.max(st, axis=0, keepdims=True))
        alpha = jnp.exp2(m_prev - m_new)
        pt = jnp.exp2(st - m_new)
        l_sc[...] = alpha * l_sc[...] + jnp.sum(pt, axis=0, keepdims=True)
        pv = lax.dot_general(v, pt.astype(BF16), TN_DIMS, preferred_element_type=F32)
        acc_sc[...] = alpha * acc_sc[...] + pv
        m_sc[...] = m_new

    scores(0, 0)
    for c in range(n_chunks - 1):
        scores(c + 1, (c + 1) % 2)
        softmax_pv(c, c % 2)
    softmax_pv(n_chunks - 1, (n_chunks - 1) % 2)
    ot = acc_sc[...] / l_sc[...]
    for r in range(A_REP):
        o_ref[0, :, r * HEAD_DIM:(r + 1) * HEAD_DIM] = ot[:, r * tq:(r + 1) * tq].T.astype(o_ref.dtype)


def _gqa(qa, kv, *, tq, tk):
    b, s, _ = qa.shape
    assert s % tk == 0 and s % tq == 0
    gw = A_REP * HEAD_DIM
    rows = A_REP * tq
    return pl.pallas_call(
        functools.partial(_gqa_kernel, tk=tk),
        grid=(b, A_KV_HEADS, s // tq),
        in_specs=[pl.BlockSpec((1, tq, gw), lambda bi, g, i: (bi, i, g)),
                  pl.BlockSpec((None, 1, s, HEAD_DIM), lambda bi, g, i: (g, bi, 0, 0)),
                  pl.BlockSpec((None, 1, s, HEAD_DIM), lambda bi, g, i: (A_KV_HEADS + g, bi, 0, 0))],
        out_specs=pl.BlockSpec((1, tq, gw), lambda bi, g, i: (bi, i, g)),
        out_shape=jax.ShapeDtypeStruct((b, s, A_Q_W), BF16),
        scratch_shapes=[pltpu.VMEM((1, rows), F32), pltpu.VMEM((1, rows), F32), pltpu.VMEM((HEAD_DIM, rows), F32),
                        pltpu.VMEM((tk, rows), F32), pltpu.VMEM((tk, rows), F32)],
        compiler_params=_params("parallel", "parallel", "arbitrary"),
    )(qa, kv, kv)


def _proj_deint_kernel(x_ref, wq_ref, wk_ref, wv_ref, o_ref, *stage, dil):
    x = x_ref[...]
    tm = x.shape[0]
    for which, w_ref in enumerate((wq_ref, wk_ref, wv_ref)):
        y = jnp.dot(x, w_ref[...], preferred_element_type=F32)
        if which == 0:
            y = y * Q_SCALE
        tn = y.shape[1]
        if dil == 1:
            o_ref[0, 0, :, which * tn:(which + 1) * tn] = y.astype(o_ref.dtype)
            continue
        y_sc = stage[which]
        n_slabs = tn // LANES
        for s in range(n_slabs):
            y_sc[s] = y[:, s * LANES:(s + 1) * LANES]
        if dil == FAST_STRIDE ** 2:
            z_sc = stage[3 + which]
            part = tm // FAST_STRIDE
            for lo in range(FAST_STRIDE):
                for s in range(n_slabs):
                    z_sc[s, lo * part:(lo + 1) * part, :] = y_sc[s, pl.ds(lo, part, stride=FAST_STRIDE), :]
            for lo in range(FAST_STRIDE):
                for hi in range(FAST_STRIDE):
                    for s in range(n_slabs):
                        c0 = which * tn + s * LANES
                        rows = z_sc[s, pl.ds(lo * part + hi, tm // dil, stride=FAST_STRIDE), :]
                        o_ref[0, lo + FAST_STRIDE * hi, :, c0:c0 + LANES] = rows.astype(o_ref.dtype)
            continue
        for r in range(dil):
            for s in range(n_slabs):
                c0 = which * tn + s * LANES
                o_ref[0, r, :, c0:c0 + LANES] = y_sc[s, pl.ds(r, tm // dil, stride=dil), :].astype(o_ref.dtype)


def _proj_deint(xb, w, g, *, tm, bsz, seq_len):
    t, d = xb.shape
    dil = B_DILATIONS[g]
    tn = B_OUT_W
    tiles_per_seq = seq_len // tm
    col0 = COL_QB // tn
    w_spec = lambda which: pl.BlockSpec((d, tn), lambda i: (0, col0 + B_GROUPS * which + g))
    return pl.pallas_call(
        functools.partial(_proj_deint_kernel, dil=dil),
        grid=(t // tm,),
        in_specs=[pl.BlockSpec((tm, d), lambda i: (i, 0)), w_spec(0), w_spec(1), w_spec(2)],
        out_specs=pl.BlockSpec((1, dil, tm // dil, 3 * tn), lambda i: (i // tiles_per_seq, 0, i % tiles_per_seq, 0)),
        out_shape=jax.ShapeDtypeStruct((bsz, dil, seq_len // dil, 3 * tn), BF16),
        scratch_shapes=[pltpu.VMEM((tn // LANES, tm, LANES), F32)] * {1: 0, FAST_STRIDE ** 2: 6}.get(dil, 3),
        compiler_params=_params("parallel"),
    )(xb, w, w, w)


def _dilated_kernel(q_ref, k_ref, v_ref, o_ref, lse_ref, *token_order_scratch, dil, slopes, n_blocks, n_res):
    qb = 2 * B_SIDE
    kw = 4 * B_SIDE
    n_rows = k_ref.shape[2]
    row = lax.broadcasted_iota(jnp.int32, (qb, kw), 0)
    col = lax.broadcasted_iota(jnp.int32, (qb, kw), 1)

    def attend(r, blk, h, ks, valid, token_dist):
        sl = slice(h * HEAD_DIM, (h + 1) * HEAD_DIM)
        q = q_ref[0, r, blk * qb:(blk + 1) * qb, sl]
        k = k_ref[0, r, pl.ds(ks, kw), sl]
        v = v_ref[0, r, pl.ds(ks, kw), sl]
        s = lax.dot_general(q, k, NT_DIMS, preferred_element_type=F32) - slopes[h] * token_dist
        s = jnp.where(valid, s, -jnp.inf)
        m = jnp.max(s, axis=-1, keepdims=True)
        p = jnp.exp(s - m)
        l = jnp.sum(p, axis=-1, keepdims=True)
        o = jnp.dot(p.astype(BF16), v, preferred_element_type=F32) / l
        lse = jnp.broadcast_to(m + jnp.log(l), (qb, HEAD_DIM))
        if dil == 1:
            rows = slice(blk * qb, (blk + 1) * qb)
            o_ref[0, rows, sl] = o.astype(o_ref.dtype)
            lse_ref[0, rows, sl] = lse
        else:
            o_sc, l_sc = token_order_scratch
            o_sc[h, pl.ds(blk * qb * dil + r, qb, stride=dil), :] = o
            l_sc[h, pl.ds(blk * qb * dil + r, qb, stride=dil), :] = lse

    def residues(it, carry):
        for blk in range(n_blocks):
            q0 = (pl.program_id(1) * n_blocks + blk) * qb
            ks = pl.multiple_of(jnp.clip(q0 - B_SIDE, 0, n_rows - kw), B_SIDE)
            dist = jnp.abs(col - row + (ks - q0))
            valid = dist <= B_SIDE
            token_dist = dist.astype(F32) * float(dil)
            for u in range(n_res):
                for h in range(B_HEADS_PER_GROUP):
                    attend(it * n_res + u, blk, h, ks, valid, token_dist)
        return carry

    if dil == 1:
        residues(0, 0)
    else:
        o_sc, l_sc = token_order_scratch
        lax.fori_loop(0, dil // n_res, residues, 0)
        for h in range(B_HEADS_PER_GROUP):
            sl = slice(h * HEAD_DIM, (h + 1) * HEAD_DIM)
            o_ref[0, :, sl] = o_sc[h].astype(o_ref.dtype)
            lse_ref[0, :, sl] = l_sc[h]


DILATED_BATCHING = ((4, 1), (4, 1), (1, 4))


def _dilated_group(qkv, g):
    b, dil, n_rows, _ = qkv.shape
    s = n_rows * dil
    n_blocks, n_res = DILATED_BATCHING[g]
    rows = n_blocks * 2 * B_SIDE
    assert n_rows >= 4 * B_SIDE and n_rows % rows == 0 and dil % n_res == 0
    slopes = tuple(2.0 ** (-8.0 * (g * B_HEADS_PER_GROUP + h + 1) / B_HEADS) for h in range(B_HEADS_PER_GROUP))
    kv_spec = lambda which: pl.BlockSpec((1, dil, n_rows, B_OUT_W), lambda bi, i: (bi, 0, 0, which))
    out_spec = pl.BlockSpec((1, rows * dil, B_OUT_W), lambda bi, i: (bi, i, 0))
    o, lse = pl.pallas_call(
        functools.partial(_dilated_kernel, dil=dil, slopes=slopes, n_blocks=n_blocks, n_res=n_res),
        grid=(b, n_rows // rows),
        in_specs=[pl.BlockSpec((1, dil, rows, B_OUT_W), lambda bi, i: (bi, 0, i, 0)), kv_spec(1), kv_spec(2)],
        out_specs=[out_spec, out_spec],
        out_shape=[jax.ShapeDtypeStruct((b, s, B_OUT_W), BF16), jax.ShapeDtypeStruct((b, s, B_OUT_W), F32)],
        scratch_shapes=[pltpu.VMEM((B_HEADS_PER_GROUP, rows * dil, LANES), F32)] * (2 if dil > 1 else 0),
        compiler_params=_params("parallel", "arbitrary"),
    )(qkv, qkv, qkv)
    return o.reshape(b * s, B_OUT_W), lse.reshape(b * s, B_OUT_W)


def _mix_out_kernel(oa_ref, o0_ref, o1_ref, o2_ref, l0_ref, l1_ref, l2_ref, ga_ref, gb_ref, x_ref,
                    wpa_ref, wpb_ref, wo_ref, g_ref, b_ref, out_ref):
    l0, l1, l2 = l0_ref[...], l1_ref[...], l2_ref[...]
    mx = jnp.maximum(jnp.maximum(l0, l1), l2)
    e0, e1, e2 = jnp.exp(l0 - mx), jnp.exp(l1 - mx), jnp.exp(l2 - mx)
    ob = (e0 * o0_ref[...] + e1 * o1_ref[...] + e2 * o2_ref[...]) / (e0 + e1 + e2)
    pa = jnp.dot(oa_ref[...], wpa_ref[...], preferred_element_type=F32)
    pb = jnp.dot(ob.astype(BF16), wpb_ref[...], preferred_element_type=F32)
    mixed = (ga_ref[...].astype(F32) * pa + gb_ref[...].astype(F32) * pb).astype(BF16)
    y = jnp.dot(mixed, wo_ref[...], preferred_element_type=F32)
    out_ref[...] = _layer_norm(ALPHA * x_ref[...] + y, g_ref[...], b_ref[...])


def _mix_out(out_a, o_groups, lse_groups, gates, x2d, wpa, wpb, wo, g, b, *, tm):
    t = out_a.shape[0]
    row = lambda w, cb=0: pl.BlockSpec((tm, w), lambda i: (i, cb))
    const = lambda a: pl.BlockSpec(a.shape, lambda i: (0, 0), pipeline_mode=pl.Buffered(1))
    return pl.pallas_call(
        _mix_out_kernel,
        grid=(t // tm,),
        in_specs=[row(A_Q_W)] + [row(B_OUT_W)] * 6 + [row(D_MODEL, 0), row(D_MODEL, 1), row(D_MODEL),
                  const(wpa), const(wpb), const(wo), const(g), const(b)],
        out_specs=row(D_MODEL),
        out_shape=jax.ShapeDtypeStruct((t, D_MODEL), F32),
        compiler_params=_params("parallel"),
    )(out_a, *o_groups, *lse_groups, gates, gates, x2d, wpa, wpb, wo, g, b)


def _ffn_kernel(x_ref, wg_ref, wu_ref, wd_ref, g_ref, b_ref, o_ref, xb_sc, acc_sc):
    f = pl.program_id(1)

    @pl.when(f == 0)
    def _():
        xb_sc[...] = x_ref[...].astype(BF16)
        acc_sc[...] = jnp.zeros(acc_sc.shape, F32)

    xb = xb_sc[...]
    gate = jnp.dot(xb, wg_ref[...], preferred_element_type=F32)
    up = jnp.dot(xb, wu_ref[...], preferred_element_type=F32)
    h = gate * (1.0 / (1.0 + jnp.exp(-gate))) * up
    acc_sc[...] += jnp.dot(h.astype(BF16), wd_ref[...], preferred_element_type=F32)

    @pl.when(f == pl.num_programs(1) - 1)
    def _():
        o_ref[...] = _layer_norm(ALPHA * x_ref[...] + acc_sc[...], g_ref[...], b_ref[...])


def _ffn(x1, wg, wu, wd, g, b, *, tm, tf):
    t, d = x1.shape
    hidden = wg.shape[1]
    row = pl.BlockSpec((tm, d), lambda i, f: (i, 0))
    row_in = pl.BlockSpec((tm, d), lambda i, f: (i, 0), pipeline_mode=pl.Buffered(1))
    vec = pl.BlockSpec((1, d), lambda i, f: (0, 0))
    return pl.pallas_call(
        _ffn_kernel,
        grid=(t // tm, hidden // tf),
        in_specs=[row_in,
                  pl.BlockSpec((d, tf), lambda i, f: (0, f)),
                  pl.BlockSpec((d, tf), lambda i, f: (0, f)),
                  pl.BlockSpec((tf, d), lambda i, f: (f, 0)),
                  vec, vec],
        out_specs=row,
        out_shape=jax.ShapeDtypeStruct((t, d), F32),
        scratch_shapes=[pltpu.VMEM((tm, d), BF16), pltpu.VMEM((tm, d), F32)],
        compiler_params=_params("parallel", "arbitrary", vmem_limit=FFN_VMEM_LIMIT),
    )(x1, wg, wu, wd, g, b)


def _rope_tables(seq_len):
    pos = np.arange(seq_len)
    n_freq = ROPE_AXIS_DIM // 2
    freqs = (np.float32(ROPE_THETA) ** (-np.arange(n_freq, dtype=np.float32) / np.float32(n_freq))).astype(np.float32)
    ang_r = (pos // GRID_W).astype(np.float32)[:, None] * freqs[None, :]
    ang_c = (pos % GRID_W).astype(np.float32)[:, None] * freqs[None, :]
    cos = np.concatenate([np.cos(ang_r), np.cos(ang_c)] * 2, axis=-1).astype(np.float32)
    sin = np.concatenate([-np.sin(ang_r), -np.sin(ang_c), np.sin(ang_r), np.sin(ang_c)], axis=-1).astype(np.float32)
    return jnp.asarray(cos), jnp.asarray(sin)


def _layer(x, w_in, b_gate, q_norm_a, k_norm_a, w_proj_a, w_proj_b, w_out,
           ln1_g, ln1_b, w_ffn_gate, w_ffn_up, w_ffn_down, ln2_g, ln2_b, rope):
    bsz, seq_len, d = x.shape
    t = bsz * seq_len
    x2d = x.reshape(t, d)
    w_in_b = w_in.astype(BF16)
    vec = lambda a: a.reshape(1, -1).astype(F32)

    order = jnp.asarray(ROPE_LANE_ORDER)
    head_cols = lambda col0, n_heads: (col0 + HEAD_DIM * jnp.arange(n_heads)[:, None] + order[None, :]).reshape(-1)
    w_q = w_in_b[:, head_cols(COL_QA, A_Q_HEADS)]
    w_kv = jnp.concatenate([w_in_b[:, head_cols(COL_KA, A_KV_HEADS)], w_in_b[:, COL_VA:COL_VA + A_KV_W]], axis=1)
    q_gain = jnp.tile(q_norm_a[order] * (Q_SCALE * LOG2_E), A_Q_HEADS)
    kv_gain = jnp.concatenate([jnp.tile(k_norm_a[order], A_KV_HEADS), jnp.ones((A_KV_W,), F32)])
    proj = functools.partial(_in_proj, tm=1024, seq_len=seq_len)
    gates, xb = _proj_gate(x2d, w_in_b, COL_GA, 2 * D_MODEL, vec(b_gate), tm=1024, tn=1024)
    qa = proj(_proj_q_kernel, xb, w_q, 0, A_Q_W, [vec(q_gain)], rope, tn=512)
    kv = proj(_proj_kv_kernel, xb, w_kv, 0, 2 * A_KV_W, [vec(kv_gain)], rope, tn=2 * A_KV_W, head_major=True)

    out_a = _gqa(qa.reshape(bsz, seq_len, A_Q_W), kv.reshape(2 * A_KV_HEADS, bsz, seq_len, HEAD_DIM), tq=256, tk=1024)
    groups = [_dilated_group(_proj_deint(xb, w_in_b, g, tm=1024, bsz=bsz, seq_len=seq_len), g)
              for g in range(B_GROUPS)]
    x1 = _mix_out(out_a.reshape(t, A_Q_W), [o for o, _ in groups], [l for _, l in groups], gates, x2d,
                  w_proj_a.astype(BF16), w_proj_b.astype(BF16), w_out.astype(BF16), vec(ln1_g), vec(ln1_b), tm=512)
    x2 = _ffn(x1, w_ffn_gate.astype(BF16), w_ffn_up.astype(BF16), w_ffn_down.astype(BF16),
              vec(ln2_g), vec(ln2_b), tm=1024, tf=512)
    return x2.reshape(bsz, seq_len, d)


def kernel(x, w_in, b_gate, q_norm_a, k_norm_a, w_proj_a, w_proj_b, w_out, ln1_g, ln1_b, w_ffn_gate, w_ffn_up, w_ffn_down, ln2_g, ln2_b):
    rope = _rope_tables(x.shape[1])
    for l in range(w_in.shape[0]):
        x = _layer(x, w_in[l], b_gate[l], q_norm_a[l], k_norm_a[l], w_proj_a[l], w_proj_b[l], w_out[l],
                   ln1_g[l], ln1_b[l], w_ffn_gate[l], w_ffn_up[l], w_ffn_down[l], ln2_g[l], ln2_b[l], rope)
    return x
```

```python
import functools

import jax
import jax.numpy as jnp
import numpy as np
from jax import lax
from jax.experimental import pallas as pl
from jax.experimental.pallas import tpu as pltpu

F32 = jnp.float32
BF16 = jnp.bfloat16

D_MODEL = 2048
HEAD_DIM = 128
A_Q_HEADS = 8
A_KV_HEADS = 2
A_REP = A_Q_HEADS // A_KV_HEADS
ROPE_THETA = 10000.0
ROPE_AXIS_DIM = HEAD_DIM // 2
GRID_W = 64
B_DILATIONS = (1, 4, 16)
B_GROUPS = 3
B_HEADS_PER_GROUP = 4
B_HEADS = B_GROUPS * B_HEADS_PER_GROUP
B_SIDE = 64
A_Q_W = A_Q_HEADS * HEAD_DIM
A_KV_W = A_KV_HEADS * HEAD_DIM
B_W = B_HEADS * HEAD_DIM
B_OUT_W = B_HEADS_PER_GROUP * HEAD_DIM
FFN_HIDDEN = 5632
ALPHA = 2.0 ** 0.25
RMS_EPS = 1e-6
LN_EPS = 1e-5
Q_SCALE = HEAD_DIM ** -0.5

COL_QA = 0
COL_KA = A_Q_W
COL_VA = A_Q_W + A_KV_W
COL_QB = COL_VA + A_KV_W
COL_GA = COL_QB + 3 * B_W

VMEM_LIMIT = 60 * 1024 * 1024

NT_DIMS = (((1,), (1,)), ((), ()))
TN_DIMS = (((0,), (0,)), ((), ()))
LANES = 128
BF16_ROWS = 16
FAST_STRIDE = 4
LOG2_E = 1.4426950408889634


def _params(*sem):
    return pltpu.CompilerParams(dimension_semantics=sem, vmem_limit_bytes=VMEM_LIMIT)


def _with_casts(kernel, n_in, n_out, n_casts):
    def wrapped(*refs):
        ins, rest = refs[:n_in], refs[n_in:]
        cast_in, rest = rest[:n_casts], rest[n_casts:]
        outs, rest = rest[:n_out], rest[n_out:]
        cast_out, scratch = rest[:n_casts], rest[n_casts:]
        kernel(*ins, *outs, *scratch)
        for src, dst in zip(cast_in, cast_out):
            dst[...] = src[...].astype(dst.dtype)
    return wrapped


def _cast_specs(weights, grid):
    n_steps = 1
    for n in grid:
        n_steps *= n

    def step(*idx):
        s = 0
        for i, n in zip(idx, grid):
            s = s * n + i
        return s

    specs, shapes = [], []
    for w, n_cols in weights:
        rows = w.shape[0] // n_steps
        assert rows * n_steps == w.shape[0] and rows % BF16_ROWS == 0 and n_cols % LANES == 0
        specs.append(pl.BlockSpec((rows, n_cols), lambda *idx: (step(*idx), 0)))
        shapes.append(jax.ShapeDtypeStruct((w.shape[0], n_cols), BF16))
    return specs, shapes


def _layer_norm(z, g, b):
    mu = jnp.mean(z, axis=-1, keepdims=True)
    zc = z - mu
    var = jnp.mean(zc * zc, axis=-1, keepdims=True)
    return zc * lax.rsqrt(var + LN_EPS) * g + b


ROPE_LANE_ORDER = tuple(list(range(0, 32)) + list(range(64, 96)) + list(range(32, 64)) + list(range(96, 128)))


def _norm_rope(yh, gain, cos, sin):
    mean_w = jnp.full((HEAD_DIM, HEAD_DIM), 1.0 / HEAD_DIM, BF16)
    sq = yh * yh
    hi = sq.astype(BF16)
    lo = (sq - hi.astype(F32)).astype(BF16)
    ms = jnp.dot(hi, mean_w, preferred_element_type=F32) + jnp.dot(lo, mean_w, preferred_element_type=F32)
    yn = yh * lax.rsqrt(ms + RMS_EPS) * gain
    return yn * cos + pltpu.roll(yn, HEAD_DIM // 2, 1) * sin


def _proj_q_kernel(x_ref, w_ref, gain_ref, cos_ref, sin_ref, o_ref):
    y = jnp.dot(x_ref[...], w_ref[...], preferred_element_type=F32)
    cos = cos_ref[...]
    sin = sin_ref[...]
    for h in range(y.shape[1] // HEAD_DIM):
        sl = slice(h * HEAD_DIM, (h + 1) * HEAD_DIM)
        o_ref[:, sl] = _norm_rope(y[:, sl], gain_ref[:, sl], cos, sin).astype(o_ref.dtype)


def _proj_kv_kernel(x_ref, w_ref, gain_ref, cos_ref, sin_ref, o_ref):
    y = jnp.dot(x_ref[...], w_ref[...], preferred_element_type=F32)
    cos = cos_ref[...]
    sin = sin_ref[...]
    for h in range(A_KV_HEADS):
        sl = slice(h * HEAD_DIM, (h + 1) * HEAD_DIM)
        o_ref[h] = _norm_rope(y[:, sl], gain_ref[:, sl], cos, sin).astype(o_ref.dtype)
    for h in range(A_KV_HEADS, 2 * A_KV_HEADS):
        o_ref[h] = y[:, h * HEAD_DIM:(h + 1) * HEAD_DIM].astype(o_ref.dtype)


def _proj_gate_kernel(x_ref, w_ref, b_ref, o_ref, xb_ref):
    @pl.when(pl.program_id(1) == 0)
    def _():
        xb_ref[...] = x_ref[...].astype(xb_ref.dtype)

    y = jnp.dot(xb_ref[...], w_ref[...], preferred_element_type=F32)
    o_ref[...] = (0.5 * jnp.tanh(0.5 * (y + b_ref[...])) + 0.5).astype(o_ref.dtype)


def _proj_gate(x2d, w, bias, casts, *, tm, tn):
    t, d = x2d.shape
    n_cols = w.shape[1]
    grid = (t // tm, n_cols // tn)
    cast_specs, cast_shapes = _cast_specs(casts, grid)
    return pl.pallas_call(
        _with_casts(_proj_gate_kernel, 3, 2, len(casts)),
        grid=grid,
        in_specs=[pl.BlockSpec((tm, d), lambda i, j: (i, 0)),
                  pl.BlockSpec((d, tn), lambda i, j: (0, j)),
                  pl.BlockSpec((1, tn), lambda i, j: (0, j))] + cast_specs,
        out_specs=[pl.BlockSpec((tm, tn), lambda i, j: (i, j)), pl.BlockSpec((tm, d), lambda i, j: (i, 0))] + cast_specs,
        out_shape=[jax.ShapeDtypeStruct((t, n_cols), BF16), jax.ShapeDtypeStruct((t, d), BF16)] + cast_shapes,
        compiler_params=_params("parallel", "arbitrary"),
    )(x2d, w, bias, *[w_f32 for w_f32, _ in casts])


def _in_proj(kern, xb, w, col_off, n_cols, row_vecs, tabs=(), *, tm, tn, seq_len, head_major=False):
    t, d = xb.shape
    off = col_off // tn
    pos_blocks = seq_len // tm
    in_specs = [pl.BlockSpec((tm, d), lambda i, j: (i, 0)),
                pl.BlockSpec((d, tn), lambda i, j: (0, off + j))]
    in_specs += [pl.BlockSpec((1, tn), lambda i, j: (0, j)) for _ in row_vecs]
    in_specs += [pl.BlockSpec((tm, HEAD_DIM), lambda i, j: (i % pos_blocks, 0)) for _ in tabs]
    if head_major:
        assert n_cols == tn
        out_spec = pl.BlockSpec((tn // HEAD_DIM, tm, HEAD_DIM), lambda i, j: (0, i, 0))
        out_shape = jax.ShapeDtypeStruct((tn // HEAD_DIM, t, HEAD_DIM), BF16)
    else:
        out_spec = pl.BlockSpec((tm, tn), lambda i, j: (i, j))
        out_shape = jax.ShapeDtypeStruct((t, n_cols), BF16)
    return pl.pallas_call(
        kern,
        grid=(t // tm, n_cols // tn),
        in_specs=in_specs,
        out_specs=out_spec,
        out_shape=out_shape,
        compiler_params=_params("parallel", "arbitrary"),
    )(xb, w, *row_vecs, *tabs)


def _gqa_kernel(q_ref, k_ref, v_ref, o_ref, m_sc, l_sc, acc_sc, s0_sc, s1_sc, *, tk):
    tq = q_ref.shape[1]
    n_chunks = k_ref.shape[1] // tk
    s_sc = (s0_sc, s1_sc)
    q = jnp.concatenate([q_ref[0, :, r * HEAD_DIM:(r + 1) * HEAD_DIM] for r in range(A_REP)], axis=0)
    m_sc[...] = jnp.full(m_sc.shape, -jnp.inf, F32)
    l_sc[...] = jnp.zeros(l_sc.shape, F32)
    acc_sc[...] = jnp.zeros(acc_sc.shape, F32)

    def scores(c, slot):
        s_sc[slot][...] = lax.dot_general(k_ref[0, c * tk:(c + 1) * tk, :], q, NT_DIMS, preferred_element_type=F32)

    def softmax_pv(c, slot):
        v = v_ref[0, c * tk:(c + 1) * tk, :]
        st = s_sc[slot][...]
        m_prev = m_sc[...]
        m_new = jnp.maximum(m_prev, jnp.max(st, axis=0, keepdims=True))
        alpha = jnp.exp2(m_prev - m_new)
        pt = jnp.exp2(st - m_new)
        l_sc[...] = alpha * l_sc[...] + jnp.sum(pt, axis=0, keepdims=True)
        pv = lax.dot_general(v, pt.astype(BF16), TN_DIMS, preferred_element_type=F32)
        acc_sc[...] = alpha * acc_sc[...] + pv
        m_sc[...] = m_new

    scores(0, 0)
    for c in range(n_chunks - 1):
        scores(c + 1, (c + 1) % 2)
        softmax_pv(c, c % 2)
    softmax_pv(n_chunks - 1, (n_chunks - 1) % 2)
    ot = acc_sc[...] / l_sc[...]
    for r in range(A_REP):
        o_ref[0, :, r * HEAD_DIM:(r + 1) * HEAD_DIM] = ot[:, r * tq:(r + 1) * tq].T.astype(o_ref.dtype)


def _gqa(qa, kv, casts, *, tq, tk):
    b, s, _ = qa.shape
    assert s % tk == 0 and s % tq == 0
    gw = A_REP * HEAD_DIM
    rows = A_REP * tq
    grid = (b, A_KV_HEADS, s // tq)
    cast_specs, cast_shapes = _cast_specs(casts, grid)
    return pl.pallas_call(
        _with_casts(functools.partial(_gqa_kernel, tk=tk), 3, 1, len(casts)),
        grid=grid,
        in_specs=[pl.BlockSpec((1, tq, gw), lambda bi, g, i: (bi, i, g)),
                  pl.BlockSpec((None, 1, s, HEAD_DIM), lambda bi, g, i: (g, bi, 0, 0)),
                  pl.BlockSpec((None, 1, s, HEAD_DIM), lambda bi, g, i: (A_KV_HEADS + g, bi, 0, 0))] + cast_specs,
        out_specs=[pl.BlockSpec((1, tq, gw), lambda bi, g, i: (bi, i, g))] + cast_specs,
        out_shape=[jax.ShapeDtypeStruct((b, s, A_Q_W), BF16)] + cast_shapes,
        scratch_shapes=[pltpu.VMEM((1, rows), F32), pltpu.VMEM((1, rows), F32), pltpu.VMEM((HEAD_DIM, rows), F32),
                        pltpu.VMEM((tk, rows), F32), pltpu.VMEM((tk, rows), F32)],
        compiler_params=_params("parallel", "parallel", "arbitrary"),
    )(qa, kv, kv, *[w_f32 for w_f32, _ in casts])


def _proj_deint_kernel(x_ref, wq_ref, wk_ref, wv_ref, o_ref, *stage, dil):
    x = x_ref[...]
    tm = x.shape[0]
    for which, w_ref in enumerate((wq_ref, wk_ref, wv_ref)):
        y = jnp.dot(x, w_ref[...], preferred_element_type=F32)
        if which == 0:
            y = y * Q_SCALE
        tn = y.shape[1]
        if dil == 1:
            o_ref[0, 0, :, which * tn:(which + 1) * tn] = y.astype(o_ref.dtype)
            continue
        y_sc = stage[which]
        n_slabs = tn // LANES
        for s in range(n_slabs):
            y_sc[s] = y[:, s * LANES:(s + 1) * LANES]
        if dil == FAST_STRIDE ** 2:
            z_sc = stage[3 + which]
            part = tm // FAST_STRIDE
            for lo in range(FAST_STRIDE):
                for s in range(n_slabs):
                    z_sc[s, lo * part:(lo + 1) * part, :] = y_sc[s, pl.ds(lo, part, stride=FAST_STRIDE), :]
            for lo in range(FAST_STRIDE):
                for hi in range(FAST_STRIDE):
                    for s in range(n_slabs):
                        c0 = which * tn + s * LANES
                        rows = z_sc[s, pl.ds(lo * part + hi, tm // dil, stride=FAST_STRIDE), :]
                        o_ref[0, lo + FAST_STRIDE * hi, :, c0:c0 + LANES] = rows.astype(o_ref.dtype)
            continue
        for r in range(dil):
            for s in range(n_slabs):
                c0 = which * tn + s * LANES
                o_ref[0, r, :, c0:c0 + LANES] = y_sc[s, pl.ds(r, tm // dil, stride=dil), :].astype(o_ref.dtype)


def _proj_deint(xb, w, g, *, tm, bsz, seq_len):
    t, d = xb.shape
    dil = B_DILATIONS[g]
    tn = B_OUT_W
    tiles_per_seq = seq_len // tm
    col0 = COL_QB // tn
    w_spec = lambda which: pl.BlockSpec((d, tn), lambda i: (0, col0 + B_GROUPS * which + g))
    return pl.pallas_call(
        functools.partial(_proj_deint_kernel, dil=dil),
        grid=(t // tm,),
        in_specs=[pl.BlockSpec((tm, d), lambda i: (i, 0)), w_spec(0), w_spec(1), w_spec(2)],
        out_specs=pl.BlockSpec((1, dil, tm // dil, 3 * tn), lambda i: (i // tiles_per_seq, 0, i % tiles_per_seq, 0)),
        out_shape=jax.ShapeDtypeStruct((bsz, dil, seq_len // dil, 3 * tn), BF16),
        scratch_shapes=[pltpu.VMEM((tn // LANES, tm, LANES), F32)] * {1: 0, FAST_STRIDE ** 2: 6}.get(dil, 3),
        compiler_params=_params("parallel"),
    )(xb, w, w, w)


def _dilated_kernel(q_ref, k_ref, v_ref, o_ref, lse_ref, *token_order_scratch, dil, slopes, n_blocks, n_res):
    qb = 2 * B_SIDE
    kw = 4 * B_SIDE
    n_rows = k_ref.shape[2]
    row = lax.broadcasted_iota(jnp.int32, (qb, kw), 0)
    col = lax.broadcasted_iota(jnp.int32, (qb, kw), 1)

    def attend(r, blk, h, ks, valid, token_dist):
        sl = slice(h * HEAD_DIM, (h + 1) * HEAD_DIM)
        q = q_ref[0, r, blk * qb:(blk + 1) * qb, sl]
        k = k_ref[0, r, pl.ds(ks, kw), sl]
        v = v_ref[0, r, pl.ds(ks, kw), sl]
        s = lax.dot_general(q, k, NT_DIMS, preferred_element_type=F32) - slopes[h] * token_dist
        s = jnp.where(valid, s, -jnp.inf)
        m = jnp.max(s, axis=-1, keepdims=True)
        p = jnp.exp(s - m)
        l = jnp.sum(p, axis=-1, keepdims=True)
        o = jnp.dot(p.astype(BF16), v, preferred_element_type=F32) / l
        lse = jnp.broadcast_to(m + jnp.log(l), (qb, HEAD_DIM))
        if dil == 1:
            rows = slice(blk * qb, (blk + 1) * qb)
            o_ref[0, rows, sl] = o.astype(o_ref.dtype)
            lse_ref[0, rows, sl] = lse
        else:
            o_sc, l_sc = token_order_scratch
            o_sc[h, pl.ds(blk * qb * dil + r, qb, stride=dil), :] = o
            l_sc[h, pl.ds(blk * qb * dil + r, qb, stride=dil), :] = lse

    def residues(it, carry):
        for blk in range(n_blocks):
            q0 = (pl.program_id(1) * n_blocks + blk) * qb
            ks = pl.multiple_of(jnp.clip(q0 - B_SIDE, 0, n_rows - kw), B_SIDE)
            dist = jnp.abs(col - row + (ks - q0))
            valid = dist <= B_SIDE
            token_dist = dist.astype(F32) * float(dil)
            for u in range(n_res):
                for h in range(B_HEADS_PER_GROUP):
                    attend(it * n_res + u, blk, h, ks, valid, token_dist)
        return carry

    if dil == 1:
        residues(0, 0)
    else:
        o_sc, l_sc = token_order_scratch
        lax.fori_loop(0, dil // n_res, residues, 0)
        for h in range(B_HEADS_PER_GROUP):
            sl = slice(h * HEAD_DIM, (h + 1) * HEAD_DIM)
            o_ref[0, :, sl] = o_sc[h].astype(o_ref.dtype)
            lse_ref[0, :, sl] = l_sc[h]


DILATED_BATCHING = ((4, 1), (4, 1), (1, 4))


def _dilated_group(qkv, g, casts=()):
    b, dil, n_rows, _ = qkv.shape
    s = n_rows * dil
    n_blocks, n_res = DILATED_BATCHING[g]
    rows = n_blocks * 2 * B_SIDE
    assert n_rows >= 4 * B_SIDE and n_rows % rows == 0 and dil % n_res == 0
    slopes = tuple(2.0 ** (-8.0 * (g * B_HEADS_PER_GROUP + h + 1) / B_HEADS) for h in range(B_HEADS_PER_GROUP))
    kv_spec = lambda which: pl.BlockSpec((1, dil, n_rows, B_OUT_W), lambda bi, i: (bi, 0, 0, which))
    out_spec = pl.BlockSpec((1, rows * dil, B_OUT_W), lambda bi, i: (bi, i, 0))
    grid = (b, n_rows // rows)
    cast_specs, cast_shapes = _cast_specs(casts, grid)
    kern = functools.partial(_dilated_kernel, dil=dil, slopes=slopes, n_blocks=n_blocks, n_res=n_res)
    o, lse, *cast_out = pl.pallas_call(
        _with_casts(kern, 3, 2, len(casts)),
        grid=grid,
        in_specs=[pl.BlockSpec((1, dil, rows, B_OUT_W), lambda bi, i: (bi, 0, i, 0)), kv_spec(1), kv_spec(2)] + cast_specs,
        out_specs=[out_spec, out_spec] + cast_specs,
        out_shape=[jax.ShapeDtypeStruct((b, s, B_OUT_W), BF16), jax.ShapeDtypeStruct((b, s, B_OUT_W), F32)] + cast_shapes,
        scratch_shapes=[pltpu.VMEM((B_HEADS_PER_GROUP, rows * dil, LANES), F32)] * (2 if dil > 1 else 0),
        compiler_params=_params("parallel", "arbitrary"),
    )(qkv, qkv, qkv, *[w_f32 for w_f32, _ in casts])
    return (o.reshape(b * s, B_OUT_W), lse.reshape(b * s, B_OUT_W)), cast_out


def _mix_out_kernel(oa_ref, o0_ref, o1_ref, o2_ref, l0_ref, l1_ref, l2_ref, ga_ref, gb_ref, x_ref,
                    wpa_ref, wpb_ref, wo_ref, g_ref, b_ref, out_ref):
    l0, l1, l2 = l0_ref[...], l1_ref[...], l2_ref[...]
    mx = jnp.maximum(jnp.maximum(l0, l1), l2)
    e0, e1, e2 = jnp.exp(l0 - mx), jnp.exp(l1 - mx), jnp.exp(l2 - mx)
    ob = (e0 * o0_ref[...] + e1 * o1_ref[...] + e2 * o2_ref[...]) / (e0 + e1 + e2)
    pa = jnp.dot(oa_ref[...], wpa_ref[...], preferred_element_type=F32)
    pb = jnp.dot(ob.astype(BF16), wpb_ref[...], preferred_element_type=F32)
    mixed = (ga_ref[...].astype(F32) * pa + gb_ref[...].astype(F32) * pb).astype(BF16)
    y = jnp.dot(mixed, wo_ref[...], preferred_element_type=F32)
    out_ref[...] = _layer_norm(ALPHA * x_ref[...] + y, g_ref[...], b_ref[...])


def _mix_out(out_a, o_groups, lse_groups, gates, x2d, wpa, wpb, wo, g, b, *, tm):
    t = out_a.shape[0]
    row = lambda w, cb=0: pl.BlockSpec((tm, w), lambda i: (i, cb))
    const = lambda a: pl.BlockSpec(a.shape, lambda i: (0, 0), pipeline_mode=pl.Buffered(1))
    return pl.pallas_call(
        _mix_out_kernel,
        grid=(t // tm,),
        in_specs=[row(A_Q_W)] + [row(B_OUT_W)] * 6 + [row(D_MODEL, 0), row(D_MODEL, 1), row(D_MODEL),
                  const(wpa), const(wpb), const(wo), const(g), const(b)],
        out_specs=row(D_MODEL),
        out_shape=jax.ShapeDtypeStruct((t, D_MODEL), F32),
        compiler_params=_params("parallel"),
    )(out_a, *o_groups, *lse_groups, gates, gates, x2d, wpa, wpb, wo, g, b)


def _ffn_kernel(x_ref, wg_ref, wu_ref, wd_ref, g_ref, b_ref, o_ref, xb_sc, acc_sc):
    f = pl.program_id(1)

    @pl.when(f == 0)
    def _():
        xb_sc[...] = x_ref[...].astype(BF16)
        acc_sc[...] = jnp.zeros(acc_sc.shape, F32)

    xb = xb_sc[...]
    gate = jnp.dot(xb, wg_ref[...], preferred_element_type=F32)
    up = jnp.dot(xb, wu_ref[...], preferred_element_type=F32)
    h = gate * (1.0 / (1.0 + jnp.exp(-gate))) * up
    acc_sc[...] += jnp.dot(h.astype(BF16), wd_ref[...], preferred_element_type=F32)

    @pl.when(f == pl.num_programs(1) - 1)
    def _():
        o_ref[...] = _layer_norm(ALPHA * x_ref[...] + acc_sc[...], g_ref[...], b_ref[...])


def _ffn(x1, wg, wu, wd, g, b, *, tm, tf):
    t, d = x1.shape
    hidden = wg.shape[1]
    row = pl.BlockSpec((tm, d), lambda i, f: (i, 0))
    vec = pl.BlockSpec((1, d), lambda i, f: (0, 0))
    return pl.pallas_call(
        _ffn_kernel,
        grid=(t // tm, hidden // tf),
        in_specs=[row,
                  pl.BlockSpec((d, tf), lambda i, f: (0, f)),
                  pl.BlockSpec((d, tf), lambda i, f: (0, f)),
                  pl.BlockSpec((tf, d), lambda i, f: (f, 0)),
                  vec, vec],
        out_specs=row,
        out_shape=jax.ShapeDtypeStruct((t, d), F32),
        scratch_shapes=[pltpu.VMEM((tm, d), BF16), pltpu.VMEM((tm, d), F32)],
        compiler_params=_params("parallel", "arbitrary"),
    )(x1, wg, wu, wd, g, b)


def _rope_tables(seq_len):
    pos = np.arange(seq_len)
    n_freq = ROPE_AXIS_DIM // 2
    freqs = (np.float32(ROPE_THETA) ** (-np.arange(n_freq, dtype=np.float32) / np.float32(n_freq))).astype(np.float32)
    ang_r = (pos // GRID_W).astype(np.float32)[:, None] * freqs[None, :]
    ang_c = (pos % GRID_W).astype(np.float32)[:, None] * freqs[None, :]
    cos = np.concatenate([np.cos(ang_r), np.cos(ang_c)] * 2, axis=-1).astype(np.float32)
    sin = np.concatenate([-np.sin(ang_r), -np.sin(ang_c), np.sin(ang_r), np.sin(ang_c)], axis=-1).astype(np.float32)
    return jnp.asarray(cos), jnp.asarray(sin)


def _layer(x, w_in, b_gate, q_norm_a, k_norm_a, w_proj_a, w_proj_b, w_out,
           ln1_g, ln1_b, w_ffn_gate, w_ffn_up, w_ffn_down, ln2_g, ln2_b, rope):
    bsz, seq_len, d = x.shape
    t = bsz * seq_len
    x2d = x.reshape(t, d)
    vec = lambda a: a.reshape(1, -1).astype(F32)

    gates, xb, w_in_b = _proj_gate(x2d, w_in[:, COL_GA:].astype(BF16), vec(b_gate), [(w_in, COL_GA)], tm=1024, tn=1024)

    order = jnp.asarray(ROPE_LANE_ORDER)
    head_cols = lambda col0, n_heads: (col0 + HEAD_DIM * jnp.arange(n_heads)[:, None] + order[None, :]).reshape(-1)
    w_q = w_in_b[:, head_cols(COL_QA, A_Q_HEADS)]
    w_kv = jnp.concatenate([w_in_b[:, head_cols(COL_KA, A_KV_HEADS)], w_in_b[:, COL_VA:COL_VA + A_KV_W]], axis=1)
    q_gain = jnp.tile(q_norm_a[order] * (Q_SCALE * LOG2_E), A_Q_HEADS)
    kv_gain = jnp.concatenate([jnp.tile(k_norm_a[order], A_KV_HEADS), jnp.ones((A_KV_W,), F32)])
    proj = functools.partial(_in_proj, tm=1024, seq_len=seq_len)
    qa = proj(_proj_q_kernel, xb, w_q, 0, A_Q_W, [vec(q_gain)], rope, tn=512)
    kv = proj(_proj_kv_kernel, xb, w_kv, 0, 2 * A_KV_W, [vec(kv_gain)], rope, tn=2 * A_KV_W, head_major=True)

    out_a, w_gate_b, w_up_b = _gqa(qa.reshape(bsz, seq_len, A_Q_W), kv.reshape(2 * A_KV_HEADS, bsz, seq_len, HEAD_DIM),
                                   [(w_ffn_gate, FFN_HIDDEN), (w_ffn_up, FFN_HIDDEN)], tq=256, tk=1024)
    groups = []
    for g in range(B_GROUPS):
        casts = [(w_ffn_down, D_MODEL)] if g == 0 else []
        group, cast_out = _dilated_group(_proj_deint(xb, w_in_b, g, tm=1024, bsz=bsz, seq_len=seq_len), g, casts)
        groups.append(group)
        if g == 0:
            w_down_b, = cast_out
    x1 = _mix_out(out_a.reshape(t, A_Q_W), [o for o, _ in groups], [l for _, l in groups], gates, x2d,
                  w_proj_a.astype(BF16), w_proj_b.astype(BF16), w_out.astype(BF16), vec(ln1_g), vec(ln1_b), tm=512)
    x2 = _ffn(x1, w_gate_b, w_up_b, w_down_b, vec(ln2_g), vec(ln2_b), tm=512, tf=512)
    return x2.reshape(bsz, seq_len, d)


def kernel(x, w_in, b_gate, q_norm_a, k_norm_a, w_proj_a, w_proj_b, w_out, ln1_g, ln1_b, w_ffn_gate, w_ffn_up, w_ffn_down, ln2_g, ln2_b):
    rope = _rope_tables(x.shape[1])
    for l in range(w_in.shape[0]):
        x = _layer(x, w_in[l], b_gate[l], q_norm_a[l], k_norm_a[l], w_proj_a[l], w_proj_b[l], w_out[l],
                   ln1_g[l], ln1_b[l], w_ffn_gate[l], w_ffn_up[l], w_ffn_down[l], ln2_g[l], ln2_b[l], rope)
    return x
```

```python
import functools

import jax
import jax.numpy as jnp
import numpy as np
from jax import lax
from jax.experimental import pallas as pl
from jax.experimental.pallas import tpu as pltpu

F32 = jnp.float32
BF16 = jnp.bfloat16

D_MODEL = 2048
HEAD_DIM = 128
A_Q_HEADS = 8
A_KV_HEADS = 2
A_REP = A_Q_HEADS // A_KV_HEADS
ROPE_THETA = 10000.0
ROPE_AXIS_DIM = HEAD_DIM // 2
GRID_W = 64
B_DILATIONS = (1, 4, 16)
B_GROUPS = 3
B_HEADS_PER_GROUP = 4
B_HEADS = B_GROUPS * B_HEADS_PER_GROUP
B_SIDE = 64
A_Q_W = A_Q_HEADS * HEAD_DIM
A_KV_W = A_KV_HEADS * HEAD_DIM
B_W = B_HEADS * HEAD_DIM
B_OUT_W = B_HEADS_PER_GROUP * HEAD_DIM
FFN_HIDDEN = 5632
ALPHA = 2.0 ** 0.25
RMS_EPS = 1e-6
LN_EPS = 1e-5
Q_SCALE = HEAD_DIM ** -0.5

COL_QA = 0
COL_KA = A_Q_W
COL_VA = A_Q_W + A_KV_W
COL_QB = COL_VA + A_KV_W
COL_GA = COL_QB + 3 * B_W

VMEM_LIMIT = 60 * 1024 * 1024

NT_DIMS = (((1,), (1,)), ((), ()))
TN_DIMS = (((0,), (0,)), ((), ()))
LANES = 128
BF16_ROWS = 16
FAST_STRIDE = 4
LOG2_E = 1.4426950408889634


def _params(*sem):
    return pltpu.CompilerParams(dimension_semantics=sem, vmem_limit_bytes=VMEM_LIMIT)


def _with_casts(kernel, n_in, n_out, n_casts):
    def wrapped(*refs):
        ins, rest = refs[:n_in], refs[n_in:]
        cast_in, rest = rest[:n_casts], rest[n_casts:]
        outs, rest = rest[:n_out], rest[n_out:]
        cast_out, scratch = rest[:n_casts], rest[n_casts:]
        kernel(*ins, *outs, *scratch)
        for src, dst in zip(cast_in, cast_out):
            dst[...] = src[...].astype(dst.dtype)
    return wrapped


def _cast_specs(weights, grid):
    n_steps = 1
    for n in grid:
        n_steps *= n

    def step(*idx):
        s = 0
        for i, n in zip(idx, grid):
            s = s * n + i
        return s

    specs, shapes = [], []
    for w, n_cols in weights:
        rows = w.shape[0] // n_steps
        assert rows * n_steps == w.shape[0] and rows % BF16_ROWS == 0 and n_cols % LANES == 0
        specs.append(pl.BlockSpec((rows, n_cols), lambda *idx: (step(*idx), 0)))
        shapes.append(jax.ShapeDtypeStruct((w.shape[0], n_cols), BF16))
    return specs, shapes


def _layer_norm(z, g, b):
    mu = jnp.mean(z, axis=-1, keepdims=True)
    zc = z - mu
    var = jnp.mean(zc * zc, axis=-1, keepdims=True)
    return zc * lax.rsqrt(var + LN_EPS) * g + b


def _norm_rope(yh, gain, cos, sin):
    mean_w = jnp.full((HEAD_DIM, HEAD_DIM), 1.0 / HEAD_DIM, BF16)
    sq = yh * yh
    hi = sq.astype(BF16)
    lo = (sq - hi.astype(F32)).astype(BF16)
    ms = jnp.dot(hi, mean_w, preferred_element_type=F32) + jnp.dot(lo, mean_w, preferred_element_type=F32)
    yn = yh * lax.rsqrt(ms + RMS_EPS) * gain
    half = ROPE_AXIS_DIM // 2
    lane = lax.broadcasted_iota(jnp.int32, yn.shape, 1)
    partner = jnp.where(lane % ROPE_AXIS_DIM < half, pltpu.roll(yn, HEAD_DIM - half, 1), pltpu.roll(yn, half, 1))
    return yn * cos + partner * sin


def _proj_q_kernel(x_ref, w_ref, gain_ref, cos_ref, sin_ref, o_ref):
    y = jnp.dot(x_ref[...], w_ref[...], preferred_element_type=F32)
    cos = cos_ref[...]
    sin = sin_ref[...]
    for h in range(y.shape[1] // HEAD_DIM):
        sl = slice(h * HEAD_DIM, (h + 1) * HEAD_DIM)
        o_ref[:, sl] = _norm_rope(y[:, sl], gain_ref[:, sl], cos, sin).astype(o_ref.dtype)


def _proj_kv_kernel(x_ref, w_ref, gain_ref, cos_ref, sin_ref, o_ref):
    y = jnp.dot(x_ref[...], w_ref[...], preferred_element_type=F32)
    cos = cos_ref[...]
    sin = sin_ref[...]
    for h in range(A_KV_HEADS):
        sl = slice(h * HEAD_DIM, (h + 1) * HEAD_DIM)
        o_ref[h] = _norm_rope(y[:, sl], gain_ref[:, sl], cos, sin).astype(o_ref.dtype)
    for h in range(A_KV_HEADS, 2 * A_KV_HEADS):
        o_ref[h] = y[:, h * HEAD_DIM:(h + 1) * HEAD_DIM].astype(o_ref.dtype)


def _proj_gate_kernel(x_ref, w_ref, b_ref, o_ref, xb_ref):
    @pl.when(pl.program_id(1) == 0)
    def _():
        xb_ref[...] = x_ref[...].astype(xb_ref.dtype)

    y = jnp.dot(xb_ref[...], w_ref[...], preferred_element_type=F32)
    o_ref[...] = (0.5 * jnp.tanh(0.5 * (y + b_ref[...])) + 0.5).astype(o_ref.dtype)


def _proj_gate(x2d, w, bias, casts, *, tm, tn):
    t, d = x2d.shape
    n_cols = w.shape[1]
    grid = (t // tm, n_cols // tn)
    cast_specs, cast_shapes = _cast_specs(casts, grid)
    return pl.pallas_call(
        _with_casts(_proj_gate_kernel, 3, 2, len(casts)),
        grid=grid,
        in_specs=[pl.BlockSpec((tm, d), lambda i, j: (i, 0)),
                  pl.BlockSpec((d, tn), lambda i, j: (0, j)),
                  pl.BlockSpec((1, tn), lambda i, j: (0, j))] + cast_specs,
        out_specs=[pl.BlockSpec((tm, tn), lambda i, j: (i, j)), pl.BlockSpec((tm, d), lambda i, j: (i, 0))] + cast_specs,
        out_shape=[jax.ShapeDtypeStruct((t, n_cols), BF16), jax.ShapeDtypeStruct((t, d), BF16)] + cast_shapes,
        compiler_params=_params("parallel", "arbitrary"),
    )(x2d, w, bias, *[w_f32 for w_f32, _ in casts])


def _in_proj(kern, xb, w, col_off, n_cols, row_vecs, tabs=(), *, tm, tn, seq_len, head_major=False):
    t, d = xb.shape
    off = col_off // tn
    pos_blocks = seq_len // tm
    in_specs = [pl.BlockSpec((tm, d), lambda i, j: (i, 0)),
                pl.BlockSpec((d, tn), lambda i, j: (0, off + j))]
    in_specs += [pl.BlockSpec((1, tn), lambda i, j: (0, j)) for _ in row_vecs]
    in_specs += [pl.BlockSpec((tm, HEAD_DIM), lambda i, j: (i % pos_blocks, 0)) for _ in tabs]
    if head_major:
        assert n_cols == tn
        out_spec = pl.BlockSpec((tn // HEAD_DIM, tm, HEAD_DIM), lambda i, j: (0, i, 0))
        out_shape = jax.ShapeDtypeStruct((tn // HEAD_DIM, t, HEAD_DIM), BF16)
    else:
        out_spec = pl.BlockSpec((tm, tn), lambda i, j: (i, j))
        out_shape = jax.ShapeDtypeStruct((t, n_cols), BF16)
    return pl.pallas_call(
        kern,
        grid=(t // tm, n_cols // tn),
        in_specs=in_specs,
        out_specs=out_spec,
        out_shape=out_shape,
        compiler_params=_params("parallel", "arbitrary"),
    )(xb, w, *row_vecs, *tabs)


def _gqa_kernel(q_ref, k_ref, v_ref, o_ref, m_sc, l_sc, acc_sc, s0_sc, s1_sc, *, tk):
    tq = q_ref.shape[1]
    n_chunks = k_ref.shape[1] // tk
    s_sc = (s0_sc, s1_sc)
    q = jnp.concatenate([q_ref[0, :, r * HEAD_DIM:(r + 1) * HEAD_DIM] for r in range(A_REP)], axis=0)
    m_sc[...] = jnp.full(m_sc.shape, -jnp.inf, F32)
    l_sc[...] = jnp.zeros(l_sc.shape, F32)
    acc_sc[...] = jnp.zeros(acc_sc.shape, F32)

    def scores(c, slot):
        s_sc[slot][...] = lax.dot_general(k_ref[0, c * tk:(c + 1) * tk, :], q, NT_DIMS, preferred_element_type=F32)

    def softmax_pv(c, slot):
        v = v_ref[0, c * tk:(c + 1) * tk, :]
        st = s_sc[slot][...]
        m_prev = m_sc[...]
        m_new = jnp.maximum(m_prev, jnp.max(st, axis=0, keepdims=True))
        alpha = jnp.exp2(m_prev - m_new)
        pt = jnp.exp2(st - m_new)
        l_sc[...] = alpha * l_sc[...] + jnp.sum(pt, axis=0, keepdims=True)
        pv = lax.dot_general(v, pt.astype(BF16), TN_DIMS, preferred_element_type=F32)
        acc_sc[...] = alpha * acc_sc[...] + pv
        m_sc[...] = m_new

    scores(0, 0)
    for c in range(n_chunks - 1):
        scores(c + 1, (c + 1) % 2)
        softmax_pv(c, c % 2)
    softmax_pv(n_chunks - 1, (n_chunks - 1) % 2)
    ot = acc_sc[...] / l_sc[...]
    for r in range(A_REP):
        o_ref[0, :, r * HEAD_DIM:(r + 1) * HEAD_DIM] = ot[:, r * tq:(r + 1) * tq].T.astype(o_ref.dtype)


def _gqa(qa, kv, casts, *, tq, tk):
    b, s, _ = qa.shape
    assert s % tk == 0 and s % tq == 0
    gw = A_REP * HEAD_DIM
    rows = A_REP * tq
    grid = (b, A_KV_HEADS, s // tq)
    cast_specs, cast_shapes = _cast_specs(casts, grid)
    return pl.pallas_call(
        _with_casts(functools.partial(_gqa_kernel, tk=tk), 3, 1, len(casts)),
        grid=grid,
        in_specs=[pl.BlockSpec((1, tq, gw), lambda bi, g, i: (bi, i, g)),
                  pl.BlockSpec((None, 1, s, HEAD_DIM), lambda bi, g, i: (g, bi, 0, 0)),
                  pl.BlockSpec((None, 1, s, HEAD_DIM), lambda bi, g, i: (A_KV_HEADS + g, bi, 0, 0))] + cast_specs,
        out_specs=[pl.BlockSpec((1, tq, gw), lambda bi, g, i: (bi, i, g))] + cast_specs,
        out_shape=[jax.ShapeDtypeStruct((b, s, A_Q_W), BF16)] + cast_shapes,
        scratch_shapes=[pltpu.VMEM((1, rows), F32), pltpu.VMEM((1, rows), F32), pltpu.VMEM((HEAD_DIM, rows), F32),
                        pltpu.VMEM((tk, rows), F32), pltpu.VMEM((tk, rows), F32)],
        compiler_params=_params("parallel", "parallel", "arbitrary"),
    )(qa, kv, kv, *[w_f32 for w_f32, _ in casts])


def _proj_deint_kernel(x_ref, wq_ref, wk_ref, wv_ref, o_ref, *stage, dil):
    x = x_ref[...]
    tm = x.shape[0]
    for which, w_ref in enumerate((wq_ref, wk_ref, wv_ref)):
        y = jnp.dot(x, w_ref[...], preferred_element_type=F32)
        if which == 0:
            y = y * Q_SCALE
        tn = y.shape[1]
        if dil == 1:
            o_ref[0, 0, :, which * tn:(which + 1) * tn] = y.astype(o_ref.dtype)
            continue
        y_sc = stage[which]
        n_slabs = tn // LANES
        for s in range(n_slabs):
            y_sc[s] = y[:, s * LANES:(s + 1) * LANES]
        if dil == FAST_STRIDE ** 2:
            z_sc = stage[3 + which]
            part = tm // FAST_STRIDE
            for lo in range(FAST_STRIDE):
                for s in range(n_slabs):
                    z_sc[s, lo * part:(lo + 1) * part, :] = y_sc[s, pl.ds(lo, part, stride=FAST_STRIDE), :]
            for lo in range(FAST_STRIDE):
                for hi in range(FAST_STRIDE):
                    for s in range(n_slabs):
                        c0 = which * tn + s * LANES
                        rows = z_sc[s, pl.ds(lo * part + hi, tm // dil, stride=FAST_STRIDE), :]
                        o_ref[0, lo + FAST_STRIDE * hi, :, c0:c0 + LANES] = rows.astype(o_ref.dtype)
            continue
        for r in range(dil):
            for s in range(n_slabs):
                c0 = which * tn + s * LANES
                o_ref[0, r, :, c0:c0 + LANES] = y_sc[s, pl.ds(r, tm // dil, stride=dil), :].astype(o_ref.dtype)


def _proj_deint(xb, w, g, casts=(), *, tm, bsz, seq_len):
    t, d = xb.shape
    dil = B_DILATIONS[g]
    tn = B_OUT_W
    tiles_per_seq = seq_len // tm
    col0 = COL_QB // tn
    w_spec = lambda which: pl.BlockSpec((d, tn), lambda i: (0, col0 + B_GROUPS * which + g))
    grid = (t // tm,)
    cast_specs, cast_shapes = _cast_specs(casts, grid)
    out_spec = pl.BlockSpec((1, dil, tm // dil, 3 * tn), lambda i: (i // tiles_per_seq, 0, i % tiles_per_seq, 0))
    qkv, *cast_out = pl.pallas_call(
        _with_casts(functools.partial(_proj_deint_kernel, dil=dil), 4, 1, len(casts)),
        grid=grid,
        in_specs=[pl.BlockSpec((tm, d), lambda i: (i, 0)), w_spec(0), w_spec(1), w_spec(2)] + cast_specs,
        out_specs=[out_spec] + cast_specs,
        out_shape=[jax.ShapeDtypeStruct((bsz, dil, seq_len // dil, 3 * tn), BF16)] + cast_shapes,
        scratch_shapes=[pltpu.VMEM((tn // LANES, tm, LANES), F32)] * {1: 0, FAST_STRIDE ** 2: 6}.get(dil, 3),
        compiler_params=_params("parallel"),
    )(xb, w, w, w, *[w_f32 for w_f32, _ in casts])
    return qkv, cast_out


def _dilated_kernel(q_ref, k_ref, v_ref, o_ref, lse_ref, *token_order_scratch, dil, slopes, n_blocks, n_res):
    qb = 2 * B_SIDE
    kw = 4 * B_SIDE
    n_rows = k_ref.shape[2]
    row = lax.broadcasted_iota(jnp.int32, (qb, kw), 0)
    col = lax.broadcasted_iota(jnp.int32, (qb, kw), 1)

    def attend(r, blk, h, ks, valid, token_dist):
        sl = slice(h * HEAD_DIM, (h + 1) * HEAD_DIM)
        q = q_ref[0, r, blk * qb:(blk + 1) * qb, sl]
        k = k_ref[0, r, pl.ds(ks, kw), sl]
        v = v_ref[0, r, pl.ds(ks, kw), sl]
        s = lax.dot_general(q, k, NT_DIMS, preferred_element_type=F32) - slopes[h] * token_dist
        s = jnp.where(valid, s, -jnp.inf)
        m = jnp.max(s, axis=-1, keepdims=True)
        p = jnp.exp(s - m)
        l = jnp.sum(p, axis=-1, keepdims=True)
        o = jnp.dot(p.astype(BF16), v, preferred_element_type=F32) / l
        lse = jnp.broadcast_to(m + jnp.log(l), (qb, HEAD_DIM))
        if dil == 1:
            rows = slice(blk * qb, (blk + 1) * qb)
            o_ref[0, rows, sl] = o.astype(o_ref.dtype)
            lse_ref[0, rows, sl] = lse
        else:
            o_sc, l_sc = token_order_scratch
            o_sc[h, pl.ds(blk * qb * dil + r, qb, stride=dil), :] = o
            l_sc[h, pl.ds(blk * qb * dil + r, qb, stride=dil), :] = lse

    def residues(it, carry):
        for blk in range(n_blocks):
            q0 = (pl.program_id(1) * n_blocks + blk) * qb
            ks = pl.multiple_of(jnp.clip(q0 - B_SIDE, 0, n_rows - kw), B_SIDE)
            dist = jnp.abs(col - row + (ks - q0))
            valid = dist <= B_SIDE
            token_dist = dist.astype(F32) * float(dil)
            for u in range(n_res):
                for h in range(B_HEADS_PER_GROUP):
                    attend(it * n_res + u, blk, h, ks, valid, token_dist)
        return carry

    if dil == 1:
        residues(0, 0)
    else:
        o_sc, l_sc = token_order_scratch
        lax.fori_loop(0, dil // n_res, residues, 0)
        for h in range(B_HEADS_PER_GROUP):
            sl = slice(h * HEAD_DIM, (h + 1) * HEAD_DIM)
            o_ref[0, :, sl] = o_sc[h].astype(o_ref.dtype)
            lse_ref[0, :, sl] = l_sc[h]


DILATED_BATCHING = ((4, 1), (4, 1), (1, 4))


def _dilated_group(qkv, g):
    b, dil, n_rows, _ = qkv.shape
    s = n_rows * dil
    n_blocks, n_res = DILATED_BATCHING[g]
    rows = n_blocks * 2 * B_SIDE
    assert n_rows >= 4 * B_SIDE and n_rows % rows == 0 and dil % n_res == 0
    slopes = tuple(2.0 ** (-8.0 * (g * B_HEADS_PER_GROUP + h + 1) / B_HEADS) for h in range(B_HEADS_PER_GROUP))
    kv_spec = lambda which: pl.BlockSpec((1, dil, n_rows, B_OUT_W), lambda bi, i: (bi, 0, 0, which))
    out_spec = pl.BlockSpec((1, rows * dil, B_OUT_W), lambda bi, i: (bi, i, 0))
    o, lse = pl.pallas_call(
        functools.partial(_dilated_kernel, dil=dil, slopes=slopes, n_blocks=n_blocks, n_res=n_res),
        grid=(b, n_rows // rows),
        in_specs=[pl.BlockSpec((1, dil, rows, B_OUT_W), lambda bi, i: (bi, 0, i, 0)), kv_spec(1), kv_spec(2)],
        out_specs=[out_spec, out_spec],
        out_shape=[jax.ShapeDtypeStruct((b, s, B_OUT_W), BF16), jax.ShapeDtypeStruct((b, s, B_OUT_W), F32)],
        scratch_shapes=[pltpu.VMEM((B_HEADS_PER_GROUP, rows * dil, LANES), F32)] * (2 if dil > 1 else 0),
        compiler_params=_params("parallel", "arbitrary"),
    )(qkv, qkv, qkv)
    return o.reshape(b * s, B_OUT_W), lse.reshape(b * s, B_OUT_W)


def _mix_out_kernel(oa_ref, o0_ref, o1_ref, o2_ref, l0_ref, l1_ref, l2_ref, ga_ref, gb_ref, x_ref,
                    wpa_ref, wpb_ref, wo_ref, g_ref, b_ref, out_ref):
    l0, l1, l2 = l0_ref[...], l1_ref[...], l2_ref[...]
    mx = jnp.maximum(jnp.maximum(l0, l1), l2)
    e0, e1, e2 = jnp.exp(l0 - mx), jnp.exp(l1 - mx), jnp.exp(l2 - mx)
    ob = (e0 * o0_ref[...] + e1 * o1_ref[...] + e2 * o2_ref[...]) / (e0 + e1 + e2)
    pa = jnp.dot(oa_ref[...], wpa_ref[...], preferred_element_type=F32)
    pb = jnp.dot(ob.astype(BF16), wpb_ref[...], preferred_element_type=F32)
    mixed = (ga_ref[...].astype(F32) * pa + gb_ref[...].astype(F32) * pb).astype(BF16)
    y = jnp.dot(mixed, wo_ref[...], preferred_element_type=F32)
    out_ref[...] = _layer_norm(ALPHA * x_ref[...] + y, g_ref[...], b_ref[...])


def _mix_out(out_a, o_groups, lse_groups, gates, x2d, wpa, wpb, wo, g, b, *, tm):
    t = out_a.shape[0]
    row = lambda w, cb=0: pl.BlockSpec((tm, w), lambda i: (i, cb))
    const = lambda a: pl.BlockSpec(a.shape, lambda i: (0, 0), pipeline_mode=pl.Buffered(1))
    return pl.pallas_call(
        _mix_out_kernel,
        grid=(t // tm,),
        in_specs=[row(A_Q_W)] + [row(B_OUT_W)] * 6 + [row(D_MODEL, 0), row(D_MODEL, 1), row(D_MODEL),
                  const(wpa), const(wpb), const(wo), const(g), const(b)],
        out_specs=row(D_MODEL),
        out_shape=jax.ShapeDtypeStruct((t, D_MODEL), F32),
        compiler_params=_params("parallel"),
    )(out_a, *o_groups, *lse_groups, gates, gates, x2d, wpa, wpb, wo, g, b)


def _ffn_kernel(x_ref, wg_ref, wu_ref, wd_ref, g_ref, b_ref, o_ref, xb_sc, acc_sc):
    f = pl.program_id(1)

    @pl.when(f == 0)
    def _():
        xb_sc[...] = x_ref[...].astype(BF16)
        acc_sc[...] = jnp.zeros(acc_sc.shape, F32)

    xb = xb_sc[...]
    gate = jnp.dot(xb, wg_ref[...], preferred_element_type=F32)
    up = jnp.dot(xb, wu_ref[...], preferred_element_type=F32)
    h = gate * (0.5 * jnp.tanh(0.5 * gate) + 0.5) * up
    acc_sc[...] += jnp.dot(h.astype(BF16), wd_ref[...], preferred_element_type=F32)

    @pl.when(f == pl.num_programs(1) - 1)
    def _():
        o_ref[...] = _layer_norm(ALPHA * x_ref[...] + acc_sc[...], g_ref[...], b_ref[...])


def _ffn(x1, wg, wu, wd, g, b, *, tm, tf):
    t, d = x1.shape
    hidden = wg.shape[1]
    row = pl.BlockSpec((tm, d), lambda i, f: (i, 0))
    vec = pl.BlockSpec((1, d), lambda i, f: (0, 0))
    return pl.pallas_call(
        _ffn_kernel,
        grid=(t // tm, hidden // tf),
        in_specs=[row,
                  pl.BlockSpec((d, tf), lambda i, f: (0, f)),
                  pl.BlockSpec((d, tf), lambda i, f: (0, f)),
                  pl.BlockSpec((tf, d), lambda i, f: (f, 0)),
                  vec, vec],
        out_specs=row,
        out_shape=jax.ShapeDtypeStruct((t, d), F32),
        scratch_shapes=[pltpu.VMEM((tm, d), BF16), pltpu.VMEM((tm, d), F32)],
        compiler_params=_params("parallel", "arbitrary"),
    )(x1, wg, wu, wd, g, b)


def _rope_tables(seq_len):
    pos = np.arange(seq_len)
    n_freq = ROPE_AXIS_DIM // 2
    freqs = (np.float32(ROPE_THETA) ** (-np.arange(n_freq, dtype=np.float32) / np.float32(n_freq))).astype(np.float32)
    ang_r = (pos // GRID_W).astype(np.float32)[:, None] * freqs[None, :]
    ang_c = (pos % GRID_W).astype(np.float32)[:, None] * freqs[None, :]
    cos = np.concatenate([np.cos(ang_r)] * 2 + [np.cos(ang_c)] * 2, axis=-1).astype(np.float32)
    sin = np.concatenate([-np.sin(ang_r), np.sin(ang_r), -np.sin(ang_c), np.sin(ang_c)], axis=-1).astype(np.float32)
    return jnp.asarray(cos), jnp.asarray(sin)


def _layer(x, w_in, b_gate, q_norm_a, k_norm_a, w_proj_a, w_proj_b, w_out,
           ln1_g, ln1_b, w_ffn_gate, w_ffn_up, w_ffn_down, ln2_g, ln2_b, rope):
    bsz, seq_len, d = x.shape
    t = bsz * seq_len
    x2d = x.reshape(t, d)
    vec = lambda a: a.reshape(1, -1).astype(F32)

    gates, xb, w_in_b = _proj_gate(x2d, w_in[:, COL_GA:].astype(BF16), vec(b_gate), [(w_in, COL_GA)], tm=1024, tn=1024)

    q_gain = jnp.tile(q_norm_a * (Q_SCALE * LOG2_E), A_Q_HEADS)
    kv_gain = jnp.concatenate([jnp.tile(k_norm_a, A_KV_HEADS), jnp.ones((A_KV_W,), F32)])
    proj = functools.partial(_in_proj, tm=1024, seq_len=seq_len)
    qa = proj(_proj_q_kernel, xb, w_in_b, COL_QA, A_Q_W, [vec(q_gain)], rope, tn=512)
    kv = proj(_proj_kv_kernel, xb, w_in_b, COL_KA, 2 * A_KV_W, [vec(kv_gain)], rope, tn=2 * A_KV_W, head_major=True)

    out_a, w_gate_b, w_up_b = _gqa(qa.reshape(bsz, seq_len, A_Q_W), kv.reshape(2 * A_KV_HEADS, bsz, seq_len, HEAD_DIM),
                                   [(w_ffn_gate, FFN_HIDDEN), (w_ffn_up, FFN_HIDDEN)], tq=256, tk=1024)
    groups = []
    for g in range(B_GROUPS):
        casts = [(w_ffn_down, D_MODEL)] if g == 0 else []
        qkv, cast_out = _proj_deint(xb, w_in_b, g, casts, tm=1024, bsz=bsz, seq_len=seq_len)
        groups.append(_dilated_group(qkv, g))
        if g == 0:
            w_down_b, = cast_out
    x1 = _mix_out(out_a.reshape(t, A_Q_W), [o for o, _ in groups], [l for _, l in groups], gates, x2d,
                  w_proj_a.astype(BF16), w_proj_b.astype(BF16), w_out.astype(BF16), vec(ln1_g), vec(ln1_b), tm=512)
    x2 = _ffn(x1, w_gate_b, w_up_b, w_down_b, vec(ln2_g), vec(ln2_b), tm=512, tf=512)
    return x2.reshape(bsz, seq_len, d)


def kernel(x, w_in, b_gate, q_norm_a, k_norm_a, w_proj_a, w_proj_b, w_out, ln1_g, ln1_b, w_ffn_gate, w_ffn_up, w_ffn_down, ln2_g, ln2_b):
    rope = _rope_tables(x.shape[1])
    for l in range(w_in.shape[0]):
        x = _layer(x, w_in[l], b_gate[l], q_norm_a[l], k_norm_a[l], w_proj_a[l], w_proj_b[l], w_out[l],
                   ln1_g[l], ln1_b[l], w_ffn_gate[l], w_ffn_up[l], w_ffn_down[l], ln2_g[l], ln2_b[l], rope)
    return x
```

```python
import functools

import jax
import jax.numpy as jnp
import numpy as np
from jax import lax
from jax.experimental import pallas as pl
from jax.experimental.pallas import tpu as pltpu

F32 = jnp.float32
BF16 = jnp.bfloat16

D_MODEL = 2048
HEAD_DIM = 128
A_Q_HEADS = 8
A_KV_HEADS = 2
A_REP = A_Q_HEADS // A_KV_HEADS
ROPE_THETA = 10000.0
ROPE_AXIS_DIM = HEAD_DIM // 2
GRID_W = 64
B_DILATIONS = (1, 4, 16)
B_GROUPS = 3
B_HEADS_PER_GROUP = 4
B_HEADS = B_GROUPS * B_HEADS_PER_GROUP
B_SIDE = 64
A_Q_W = A_Q_HEADS * HEAD_DIM
A_KV_W = A_KV_HEADS * HEAD_DIM
B_W = B_HEADS * HEAD_DIM
B_OUT_W = B_HEADS_PER_GROUP * HEAD_DIM
LSE_LANES = HEAD_DIM // B_HEADS_PER_GROUP
FFN_HIDDEN = 5632
ALPHA = 2.0 ** 0.25
RMS_EPS = 1e-6
LN_EPS = 1e-5
Q_SCALE = HEAD_DIM ** -0.5

COL_QA = 0
COL_KA = A_Q_W
COL_VA = A_Q_W + A_KV_W
COL_QB = COL_VA + A_KV_W
COL_GA = COL_QB + 3 * B_W

VMEM_LIMIT = 60 * 1024 * 1024

NT_DIMS = (((1,), (1,)), ((), ()))
TN_DIMS = (((0,), (0,)), ((), ()))
LANES = 128
BF16_ROWS = 16
FAST_STRIDE = 4
LOG2_E = 1.4426950408889634


def _params(*sem):
    return pltpu.CompilerParams(dimension_semantics=sem, vmem_limit_bytes=VMEM_LIMIT)


def _with_casts(kernel, n_in, n_out, n_casts):
    def wrapped(*refs):
        ins, rest = refs[:n_in], refs[n_in:]
        cast_in, rest = rest[:n_casts], rest[n_casts:]
        outs, rest = rest[:n_out], rest[n_out:]
        cast_out, scratch = rest[:n_casts], rest[n_casts:]
        kernel(*ins, *outs, *scratch)
        for src, dst in zip(cast_in, cast_out):
            dst[...] = src[...].astype(dst.dtype)
    return wrapped


def _cast_specs(weights, grid):
    n_steps = 1
    for n in grid:
        n_steps *= n

    def step(*idx):
        s = 0
        for i, n in zip(idx, grid):
            s = s * n + i
        return s

    specs, shapes = [], []
    for w, n_cols in weights:
        rows = w.shape[0] // n_steps
        assert rows * n_steps == w.shape[0] and rows % BF16_ROWS == 0 and n_cols % LANES == 0
        specs.append(pl.BlockSpec((rows, n_cols), lambda *idx: (step(*idx), 0)))
        shapes.append(jax.ShapeDtypeStruct((w.shape[0], n_cols), BF16))
    return specs, shapes


def _layer_norm(z, g, b):
    mu = jnp.mean(z, axis=-1, keepdims=True)
    zc = z - mu
    var = jnp.mean(zc * zc, axis=-1, keepdims=True)
    return zc * lax.rsqrt(var + LN_EPS) * g + b


def _norm_rope(yh, gain, cos, sin):
    mean_w = jnp.full((HEAD_DIM, HEAD_DIM), 1.0 / HEAD_DIM, BF16)
    sq = yh * yh
    hi = sq.astype(BF16)
    lo = (sq - hi.astype(F32)).astype(BF16)
    ms = jnp.dot(hi, mean_w, preferred_element_type=F32) + jnp.dot(lo, mean_w, preferred_element_type=F32)
    yn = yh * lax.rsqrt(ms + RMS_EPS) * gain
    half = ROPE_AXIS_DIM // 2
    lane = lax.broadcasted_iota(jnp.int32, yn.shape, 1)
    partner = jnp.where(lane % ROPE_AXIS_DIM < half, pltpu.roll(yn, HEAD_DIM - half, 1), pltpu.roll(yn, half, 1))
    return yn * cos + partner * sin


def _proj_q_kernel(x_ref, w_ref, gain_ref, cos_ref, sin_ref, o_ref):
    y = jnp.dot(x_ref[...], w_ref[...], preferred_element_type=F32)
    cos = cos_ref[...]
    sin = sin_ref[...]
    for h in range(y.shape[1] // HEAD_DIM):
        sl = slice(h * HEAD_DIM, (h + 1) * HEAD_DIM)
        o_ref[:, sl] = _norm_rope(y[:, sl], gain_ref[:, sl], cos, sin).astype(o_ref.dtype)


def _proj_kv_kernel(x_ref, w_ref, gain_ref, cos_ref, sin_ref, o_ref):
    y = jnp.dot(x_ref[...], w_ref[...], preferred_element_type=F32)
    cos = cos_ref[...]
    sin = sin_ref[...]
    for h in range(A_KV_HEADS):
        sl = slice(h * HEAD_DIM, (h + 1) * HEAD_DIM)
        o_ref[h] = _norm_rope(y[:, sl], gain_ref[:, sl], cos, sin).astype(o_ref.dtype)
    for h in range(A_KV_HEADS, 2 * A_KV_HEADS):
        o_ref[h] = y[:, h * HEAD_DIM:(h + 1) * HEAD_DIM].astype(o_ref.dtype)


def _proj_gate_kernel(x_ref, w_ref, b_ref, o_ref, xb_ref):
    @pl.when(pl.program_id(1) == 0)
    def _():
        xb_ref[...] = x_ref[...].astype(xb_ref.dtype)

    y = jnp.dot(xb_ref[...], w_ref[...], preferred_element_type=F32)
    o_ref[...] = (0.5 * jnp.tanh(0.5 * (y + b_ref[...])) + 0.5).astype(o_ref.dtype)


def _proj_gate(x2d, w, bias, casts, *, tm, tn):
    t, d = x2d.shape
    n_cols = w.shape[1]
    grid = (t // tm, n_cols // tn)
    cast_specs, cast_shapes = _cast_specs(casts, grid)
    return pl.pallas_call(
        _with_casts(_proj_gate_kernel, 3, 2, len(casts)),
        grid=grid,
        in_specs=[pl.BlockSpec((tm, d), lambda i, j: (i, 0)),
                  pl.BlockSpec((d, tn), lambda i, j: (0, j)),
                  pl.BlockSpec((1, tn), lambda i, j: (0, j))] + cast_specs,
        out_specs=[pl.BlockSpec((tm, tn), lambda i, j: (i, j)), pl.BlockSpec((tm, d), lambda i, j: (i, 0))] + cast_specs,
        out_shape=[jax.ShapeDtypeStruct((t, n_cols), BF16), jax.ShapeDtypeStruct((t, d), BF16)] + cast_shapes,
        compiler_params=_params("parallel", "arbitrary"),
    )(x2d, w, bias, *[w_f32 for w_f32, _ in casts])


def _in_proj(kern, xb, w, col_off, n_cols, row_vecs, tabs=(), *, tm, tn, seq_len, head_major=False):
    t, d = xb.shape
    off = col_off // tn
    pos_blocks = seq_len // tm
    in_specs = [pl.BlockSpec((tm, d), lambda i, j: (i, 0)),
                pl.BlockSpec((d, tn), lambda i, j: (0, off + j))]
    in_specs += [pl.BlockSpec((1, tn), lambda i, j: (0, j)) for _ in row_vecs]
    in_specs += [pl.BlockSpec((tm, HEAD_DIM), lambda i, j: (i % pos_blocks, 0)) for _ in tabs]
    if head_major:
        assert n_cols == tn
        out_spec = pl.BlockSpec((tn // HEAD_DIM, tm, HEAD_DIM), lambda i, j: (0, i, 0))
        out_shape = jax.ShapeDtypeStruct((tn // HEAD_DIM, t, HEAD_DIM), BF16)
    else:
        out_spec = pl.BlockSpec((tm, tn), lambda i, j: (i, j))
        out_shape = jax.ShapeDtypeStruct((t, n_cols), BF16)
    return pl.pallas_call(
        kern,
        grid=(t // tm, n_cols // tn),
        in_specs=in_specs,
        out_specs=out_spec,
        out_shape=out_shape,
        compiler_params=_params("parallel", "arbitrary"),
    )(xb, w, *row_vecs, *tabs)


def _gqa_kernel(q_ref, k_ref, v_ref, o_ref, m_sc, l_sc, acc_sc, s0_sc, s1_sc, *, tk):
    tq = q_ref.shape[1]
    n_chunks = k_ref.shape[1] // tk
    s_sc = (s0_sc, s1_sc)
    q = jnp.concatenate([q_ref[0, :, r * HEAD_DIM:(r + 1) * HEAD_DIM] for r in range(A_REP)], axis=0)
    m_sc[...] = jnp.full(m_sc.shape, -jnp.inf, F32)
    l_sc[...] = jnp.zeros(l_sc.shape, F32)
    acc_sc[...] = jnp.zeros(acc_sc.shape, F32)

    def scores(c, slot):
        s_sc[slot][...] = lax.dot_general(k_ref[0, c * tk:(c + 1) * tk, :], q, NT_DIMS, preferred_element_type=F32)

    def softmax_pv(c, slot):
        v = v_ref[0, c * tk:(c + 1) * tk, :]
        st = s_sc[slot][...]
        m_prev = m_sc[...]
        m_new = jnp.maximum(m_prev, jnp.max(st, axis=0, keepdims=True))
        alpha = jnp.exp2(m_prev - m_new)
        pt = jnp.exp2(st - m_new)
        l_sc[...] = alpha * l_sc[...] + jnp.sum(pt, axis=0, keepdims=True)
        pv = lax.dot_general(v, pt.astype(BF16), TN_DIMS, preferred_element_type=F32)
        acc_sc[...] = alpha * acc_sc[...] + pv
        m_sc[...] = m_new

    scores(0, 0)
    for c in range(n_chunks - 1):
        scores(c + 1, (c + 1) % 2)
        softmax_pv(c, c % 2)
    softmax_pv(n_chunks - 1, (n_chunks - 1) % 2)
    ot = acc_sc[...] / l_sc[...]
    for r in range(A_REP):
        o_ref[0, :, r * HEAD_DIM:(r + 1) * HEAD_DIM] = ot[:, r * tq:(r + 1) * tq].T.astype(o_ref.dtype)


def _gqa(qa, kv, casts, *, tq, tk):
    b, s, _ = qa.shape
    assert s % tk == 0 and s % tq == 0
    gw = A_REP * HEAD_DIM
    rows = A_REP * tq
    grid = (b, A_KV_HEADS, s // tq)
    cast_specs, cast_shapes = _cast_specs(casts, grid)
    return pl.pallas_call(
        _with_casts(functools.partial(_gqa_kernel, tk=tk), 3, 1, len(casts)),
        grid=grid,
        in_specs=[pl.BlockSpec((1, tq, gw), lambda bi, g, i: (bi, i, g)),
                  pl.BlockSpec((None, 1, s, HEAD_DIM), lambda bi, g, i: (g, bi, 0, 0)),
                  pl.BlockSpec((None, 1, s, HEAD_DIM), lambda bi, g, i: (A_KV_HEADS + g, bi, 0, 0))] + cast_specs,
        out_specs=[pl.BlockSpec((1, tq, gw), lambda bi, g, i: (bi, i, g))] + cast_specs,
        out_shape=[jax.ShapeDtypeStruct((b, s, A_Q_W), BF16)] + cast_shapes,
        scratch_shapes=[pltpu.VMEM((1, rows), F32), pltpu.VMEM((1, rows), F32), pltpu.VMEM((HEAD_DIM, rows), F32),
                        pltpu.VMEM((tk, rows), F32), pltpu.VMEM((tk, rows), F32)],
        compiler_params=_params("parallel", "parallel", "arbitrary"),
    )(qa, kv, kv, *[w_f32 for w_f32, _ in casts])


def _proj_deint_kernel(x_ref, wq_ref, wk_ref, wv_ref, o_ref, *stage, dil):
    x = x_ref[...]
    tm = x.shape[0]
    for which, w_ref in enumerate((wq_ref, wk_ref, wv_ref)):
        y = jnp.dot(x, w_ref[...], preferred_element_type=F32)
        if which == 0:
            y = y * Q_SCALE
        tn = y.shape[1]
        if dil == 1:
            o_ref[0, 0, :, which * tn:(which + 1) * tn] = y.astype(o_ref.dtype)
            continue
        y_sc = stage[which]
        n_slabs = tn // LANES
        for s in range(n_slabs):
            y_sc[s] = y[:, s * LANES:(s + 1) * LANES]
        if dil == FAST_STRIDE ** 2:
            z_sc = stage[3 + which]
            part = tm // FAST_STRIDE
            for lo in range(FAST_STRIDE):
                for s in range(n_slabs):
                    z_sc[s, lo * part:(lo + 1) * part, :] = y_sc[s, pl.ds(lo, part, stride=FAST_STRIDE), :]
            for lo in range(FAST_STRIDE):
                for hi in range(FAST_STRIDE):
                    for s in range(n_slabs):
                        c0 = which * tn + s * LANES
                        rows = z_sc[s, pl.ds(lo * part + hi, tm // dil, stride=FAST_STRIDE), :]
                        o_ref[0, lo + FAST_STRIDE * hi, :, c0:c0 + LANES] = rows.astype(o_ref.dtype)
            continue
        for r in range(dil):
            for s in range(n_slabs):
                c0 = which * tn + s * LANES
                o_ref[0, r, :, c0:c0 + LANES] = y_sc[s, pl.ds(r, tm // dil, stride=dil), :].astype(o_ref.dtype)


def _proj_deint(xb, w, g, casts=(), *, tm, bsz, seq_len):
    t, d = xb.shape
    dil = B_DILATIONS[g]
    tn = B_OUT_W
    tiles_per_seq = seq_len // tm
    col0 = COL_QB // tn
    w_spec = lambda which: pl.BlockSpec((d, tn), lambda i: (0, col0 + B_GROUPS * which + g))
    grid = (t // tm,)
    cast_specs, cast_shapes = _cast_specs(casts, grid)
    out_spec = pl.BlockSpec((1, dil, tm // dil, 3 * tn), lambda i: (i // tiles_per_seq, 0, i % tiles_per_seq, 0))
    qkv, *cast_out = pl.pallas_call(
        _with_casts(functools.partial(_proj_deint_kernel, dil=dil), 4, 1, len(casts)),
        grid=grid,
        in_specs=[pl.BlockSpec((tm, d), lambda i: (i, 0)), w_spec(0), w_spec(1), w_spec(2)] + cast_specs,
        out_specs=[out_spec] + cast_specs,
        out_shape=[jax.ShapeDtypeStruct((bsz, dil, seq_len // dil, 3 * tn), BF16)] + cast_shapes,
        scratch_shapes=[pltpu.VMEM((tn // LANES, tm, LANES), F32)] * {1: 0, FAST_STRIDE ** 2: 6}.get(dil, 3),
        compiler_params=_params("parallel"),
    )(xb, w, w, w, *[w_f32 for w_f32, _ in casts])
    return qkv, cast_out


def _dilated_kernel(q_ref, k_ref, v_ref, o_ref, lse_ref, *token_order_scratch, dil, slopes, n_blocks, n_res):
    qb = 2 * B_SIDE
    kw = 4 * B_SIDE
    n_rows = k_ref.shape[2]
    row = lax.broadcasted_iota(jnp.int32, (qb, kw), 0)
    col = lax.broadcasted_iota(jnp.int32, (qb, kw), 1)

    def attend(r, blk, h, ks, valid, token_dist):
        sl = slice(h * HEAD_DIM, (h + 1) * HEAD_DIM)
        q = q_ref[0, r, blk * qb:(blk + 1) * qb, sl]
        k = k_ref[0, r, pl.ds(ks, kw), sl]
        v = v_ref[0, r, pl.ds(ks, kw), sl]
        s = lax.dot_general(q, k, NT_DIMS, preferred_element_type=F32) - slopes[h] * token_dist
        s = jnp.where(valid, s, -jnp.inf)
        m = jnp.max(s, axis=-1, keepdims=True)
        p = jnp.exp(s - m)
        l = jnp.sum(p, axis=-1, keepdims=True)
        o = jnp.dot(p.astype(BF16), v, preferred_element_type=F32) / l
        if dil == 1:
            o_ref[0, blk * qb:(blk + 1) * qb, sl] = o.astype(o_ref.dtype)
        else:
            o_sc, _ = token_order_scratch
            o_sc[h, pl.ds(blk * qb * dil + r, qb, stride=dil), :] = o
        return m + jnp.log(l)

    lane = lax.broadcasted_iota(jnp.int32, (qb, LANES), 1)

    def residues(it, carry):
        for blk in range(n_blocks):
            q0 = (pl.program_id(1) * n_blocks + blk) * qb
            ks = pl.multiple_of(jnp.clip(q0 - B_SIDE, 0, n_rows - kw), B_SIDE)
            dist = jnp.abs(col - row + (ks - q0))
            valid = dist <= B_SIDE
            token_dist = dist.astype(F32) * float(dil)
            for u in range(n_res):
                r = it * n_res + u
                lse = None
                for h in reversed(range(B_HEADS_PER_GROUP)):
                    col_h = jnp.broadcast_to(attend(r, blk, h, ks, valid, token_dist), (qb, LANES))
                    lse = col_h if lse is None else jnp.where(lane < (h + 1) * LSE_LANES, col_h, lse)
                if dil == 1:
                    lse_ref[0, blk * qb:(blk + 1) * qb, :] = lse
                else:
                    _, l_sc = token_order_scratch
                    l_sc[pl.ds(blk * qb * dil + r, qb, stride=dil), :] = lse
        return carry

    if dil == 1:
        residues(0, 0)
    else:
        o_sc, l_sc = token_order_scratch
        lax.fori_loop(0, dil // n_res, residues, 0)
        for h in range(B_HEADS_PER_GROUP):
            sl = slice(h * HEAD_DIM, (h + 1) * HEAD_DIM)
            o_ref[0, :, sl] = o_sc[h].astype(o_ref.dtype)
        lse_ref[0] = l_sc[...]


DILATED_BATCHING = ((4, 1), (4, 1), (1, 4))


def _dilated_group(qkv, g):
    b, dil, n_rows, _ = qkv.shape
    s = n_rows * dil
    n_blocks, n_res = DILATED_BATCHING[g]
    rows = n_blocks * 2 * B_SIDE
    assert n_rows >= 4 * B_SIDE and n_rows % rows == 0 and dil % n_res == 0
    slopes = tuple(2.0 ** (-8.0 * (g * B_HEADS_PER_GROUP + h + 1) / B_HEADS) for h in range(B_HEADS_PER_GROUP))
    kv_spec = lambda which: pl.BlockSpec((1, dil, n_rows, B_OUT_W), lambda bi, i: (bi, 0, 0, which))
    out_spec = lambda w: pl.BlockSpec((1, rows * dil, w), lambda bi, i: (bi, i, 0))
    staging = [pltpu.VMEM((B_HEADS_PER_GROUP, rows * dil, LANES), F32), pltpu.VMEM((rows * dil, LANES), F32)]
    o, lse = pl.pallas_call(
        functools.partial(_dilated_kernel, dil=dil, slopes=slopes, n_blocks=n_blocks, n_res=n_res),
        grid=(b, n_rows // rows),
        in_specs=[pl.BlockSpec((1, dil, rows, B_OUT_W), lambda bi, i: (bi, 0, i, 0)), kv_spec(1), kv_spec(2)],
        out_specs=[out_spec(B_OUT_W), out_spec(LANES)],
        out_shape=[jax.ShapeDtypeStruct((b, s, B_OUT_W), BF16), jax.ShapeDtypeStruct((b, s, LANES), F32)],
        scratch_shapes=staging if dil > 1 else [],
        compiler_params=_params("parallel", "arbitrary"),
    )(qkv, qkv, qkv)
    return o.reshape(b * s, B_OUT_W), lse.reshape(b * s, LANES)


def _mix_out_kernel(oa_ref, o0_ref, o1_ref, o2_ref, l0_ref, l1_ref, l2_ref, ga_ref, gb_ref, x_ref,
                    wpa_ref, wpb_ref, wo_ref, g_ref, b_ref, out_ref):
    l0, l1, l2 = l0_ref[...], l1_ref[...], l2_ref[...]
    mx = jnp.maximum(jnp.maximum(l0, l1), l2)
    e0, e1, e2 = jnp.exp(l0 - mx), jnp.exp(l1 - mx), jnp.exp(l2 - mx)
    inv = 1.0 / (e0 + e1 + e2)
    weights = (e0 * inv, e1 * inv, e2 * inv)
    rows = l0.shape[0]
    merged = []
    for h in range(B_HEADS_PER_GROUP):
        sl = slice(h * HEAD_DIM, (h + 1) * HEAD_DIM)
        acc = None
        for w, o_ref in zip(weights, (o0_ref, o1_ref, o2_ref)):
            term = jnp.broadcast_to(w[:, h * LSE_LANES:h * LSE_LANES + 1], (rows, HEAD_DIM)) * o_ref[:, sl]
            acc = term if acc is None else acc + term
        merged.append(acc.astype(BF16))
    ob = jnp.concatenate(merged, axis=1)
    pa = jnp.dot(oa_ref[...], wpa_ref[...], preferred_element_type=F32)
    pb = jnp.dot(ob, wpb_ref[...], preferred_element_type=F32)
    mixed = (ga_ref[...].astype(F32) * pa + gb_ref[...].astype(F32) * pb).astype(BF16)
    y = jnp.dot(mixed, wo_ref[...], preferred_element_type=F32)
    out_ref[...] = _layer_norm(ALPHA * x_ref[...] + y, g_ref[...], b_ref[...])


def _mix_out(out_a, o_groups, lse_groups, gates, x2d, wpa, wpb, wo, g, b, *, tm):
    t = out_a.shape[0]
    row = lambda w, cb=0: pl.BlockSpec((tm, w), lambda i: (i, cb))
    const = lambda a: pl.BlockSpec(a.shape, lambda i: (0, 0), pipeline_mode=pl.Buffered(1))
    return pl.pallas_call(
        _mix_out_kernel,
        grid=(t // tm,),
        in_specs=[row(A_Q_W)] + [row(B_OUT_W)] * 3 + [row(LANES)] * 3 + [row(D_MODEL, 0), row(D_MODEL, 1), row(D_MODEL),
                  const(wpa), const(wpb), const(wo), const(g), const(b)],
        out_specs=row(D_MODEL),
        out_shape=jax.ShapeDtypeStruct((t, D_MODEL), F32),
        compiler_params=_params("parallel"),
    )(out_a, *o_groups, *lse_groups, gates, gates, x2d, wpa, wpb, wo, g, b)


def _ffn_kernel(x_ref, wg_ref, wu_ref, wd_ref, g_ref, b_ref, o_ref, xb_sc, acc_sc):
    f = pl.program_id(1)

    @pl.when(f == 0)
    def _():
        xb_sc[...] = x_ref[...].astype(BF16)
        acc_sc[...] = jnp.zeros(acc_sc.shape, F32)

    xb = xb_sc[...]
    gate = jnp.dot(xb, wg_ref[...], preferred_element_type=F32)
    up = jnp.dot(xb, wu_ref[...], preferred_element_type=F32)
    h = gate * (0.5 * jnp.tanh(0.5 * gate) + 0.5) * up
    acc_sc[...] += jnp.dot(h.astype(BF16), wd_ref[...], preferred_element_type=F32)

    @pl.when(f == pl.num_programs(1) - 1)
    def _():
        o_ref[...] = _layer_norm(ALPHA * x_ref[...] + acc_sc[...], g_ref[...], b_ref[...])


def _ffn(x1, wg, wu, wd, g, b, *, tm, tf):
    t, d = x1.shape
    hidden = wg.shape[1]
    row = pl.BlockSpec((tm, d), lambda i, f: (i, 0))
    vec = pl.BlockSpec((1, d), lambda i, f: (0, 0))
    return pl.pallas_call(
        _ffn_kernel,
        grid=(t // tm, hidden // tf),
        in_specs=[row,
                  pl.BlockSpec((d, tf), lambda i, f: (0, f)),
                  pl.BlockSpec((d, tf), lambda i, f: (0, f)),
                  pl.BlockSpec((tf, d), lambda i, f: (f, 0)),
                  vec, vec],
        out_specs=row,
        out_shape=jax.ShapeDtypeStruct((t, d), F32),
        scratch_shapes=[pltpu.VMEM((tm, d), BF16), pltpu.VMEM((tm, d), F32)],
        compiler_params=_params("parallel", "arbitrary"),
    )(x1, wg, wu, wd, g, b)


def _rope_tables(seq_len):
    pos = np.arange(seq_len)
    n_freq = ROPE_AXIS_DIM // 2
    freqs = (np.float32(ROPE_THETA) ** (-np.arange(n_freq, dtype=np.float32) / np.float32(n_freq))).astype(np.float32)
    ang_r = (pos // GRID_W).astype(np.float32)[:, None] * freqs[None, :]
    ang_c = (pos % GRID_W).astype(np.float32)[:, None] * freqs[None, :]
    cos = np.concatenate([np.cos(ang_r)] * 2 + [np.cos(ang_c)] * 2, axis=-1).astype(np.float32)
    sin = np.concatenate([-np.sin(ang_r), np.sin(ang_r), -np.sin(ang_c), np.sin(ang_c)], axis=-1).astype(np.float32)
    return jnp.asarray(cos), jnp.asarray(sin)


def _layer(x, w_in, b_gate, q_norm_a, k_norm_a, w_proj_a, w_proj_b, w_out,
           ln1_g, ln1_b, w_ffn_gate, w_ffn_up, w_ffn_down, ln2_g, ln2_b, rope):
    bsz, seq_len, d = x.shape
    t = bsz * seq_len
    x2d = x.reshape(t, d)
    vec = lambda a: a.reshape(1, -1).astype(F32)

    gates, xb, w_in_b = _proj_gate(x2d, w_in[:, COL_GA:].astype(BF16), vec(b_gate), [(w_in, COL_GA)], tm=1024, tn=1024)

    q_gain = jnp.tile(q_norm_a * (Q_SCALE * LOG2_E), A_Q_HEADS)
    kv_gain = jnp.concatenate([jnp.tile(k_norm_a, A_KV_HEADS), jnp.ones((A_KV_W,), F32)])
    proj = functools.partial(_in_proj, tm=1024, seq_len=seq_len)
    qa = proj(_proj_q_kernel, xb, w_in_b, COL_QA, A_Q_W, [vec(q_gain)], rope, tn=512)
    kv = proj(_proj_kv_kernel, xb, w_in_b, COL_KA, 2 * A_KV_W, [vec(kv_gain)], rope, tn=2 * A_KV_W, head_major=True)

    out_a, w_gate_b, w_up_b = _gqa(qa.reshape(bsz, seq_len, A_Q_W), kv.reshape(2 * A_KV_HEADS, bsz, seq_len, HEAD_DIM),
                                   [(w_ffn_gate, FFN_HIDDEN), (w_ffn_up, FFN_HIDDEN)], tq=256, tk=1024)
    groups = []
    for g in range(B_GROUPS):
        casts = [(w_ffn_down, D_MODEL)] if g == 0 else []
        qkv, cast_out = _proj_deint(xb, w_in_b, g, casts, tm=1024, bsz=bsz, seq_len=seq_len)
        groups.append(_dilated_group(qkv, g))
        if g == 0:
            w_down_b, = cast_out
    x1 = _mix_out(out_a.reshape(t, A_Q_W), [o for o, _ in groups], [l for _, l in groups], gates, x2d,
                  w_proj_a.astype(BF16), w_proj_b.astype(BF16), w_out.astype(BF16), vec(ln1_g), vec(ln1_b), tm=512)
    x2 = _ffn(x1, w_gate_b, w_up_b, w_down_b, vec(ln2_g), vec(ln2_b), tm=512, tf=512)
    return x2.reshape(bsz, seq_len, d)


def kernel(x, w_in, b_gate, q_norm_a, k_norm_a, w_proj_a, w_proj_b, w_out, ln1_g, ln1_b, w_ffn_gate, w_ffn_up, w_ffn_down, ln2_g, ln2_b):
    rope = _rope_tables(x.shape[1])
    for l in range(w_in.shape[0]):
        x = _layer(x, w_in[l], b_gate[l], q_norm_a[l], k_norm_a[l], w_proj_a[l], w_proj_b[l], w_out[l],
                   ln1_g[l], ln1_b[l], w_ffn_gate[l], w_ffn_up[l], w_ffn_down[l], ln2_g[l], ln2_b[l], rope)
    return x
```

```python
import functools

import jax
import jax.numpy as jnp
import numpy as np
from jax import lax
from jax.experimental import pallas as pl
from jax.experimental.pallas import tpu as pltpu

F32 = jnp.float32
BF16 = jnp.bfloat16

D_MODEL = 2048
HEAD_DIM = 128
A_Q_HEADS = 8
A_KV_HEADS = 2
A_REP = A_Q_HEADS // A_KV_HEADS
ROPE_THETA = 10000.0
ROPE_AXIS_DIM = HEAD_DIM // 2
GRID_W = 64
B_DILATIONS = (1, 4, 16)
B_GROUPS = 3
B_HEADS_PER_GROUP = 4
B_HEADS = B_GROUPS * B_HEADS_PER_GROUP
B_SIDE = 64
A_Q_W = A_Q_HEADS * HEAD_DIM
A_KV_W = A_KV_HEADS * HEAD_DIM
B_W = B_HEADS * HEAD_DIM
B_OUT_W = B_HEADS_PER_GROUP * HEAD_DIM
LSE_LANES = HEAD_DIM // B_HEADS_PER_GROUP
FFN_HIDDEN = 5632
ALPHA = 2.0 ** 0.25
RMS_EPS = 1e-6
LN_EPS = 1e-5
Q_SCALE = HEAD_DIM ** -0.5

COL_QA = 0
COL_KA = A_Q_W
COL_VA = A_Q_W + A_KV_W
COL_QB = COL_VA + A_KV_W
COL_GA = COL_QB + 3 * B_W

VMEM_LIMIT = 60 * 1024 * 1024

NT_DIMS = (((1,), (1,)), ((), ()))
TN_DIMS = (((0,), (0,)), ((), ()))
LANES = 128
BF16_ROWS = 16
FAST_STRIDE = 4
LOG2_E = 1.4426950408889634


def _params(*sem):
    return pltpu.CompilerParams(dimension_semantics=sem, vmem_limit_bytes=VMEM_LIMIT)


def _with_casts(kernel, n_in, n_out, n_casts):
    def wrapped(*refs):
        ins, rest = refs[:n_in], refs[n_in:]
        cast_in, rest = rest[:n_casts], rest[n_casts:]
        outs, rest = rest[:n_out], rest[n_out:]
        cast_out, scratch = rest[:n_casts], rest[n_casts:]
        kernel(*ins, *outs, *scratch)
        for src, dst in zip(cast_in, cast_out):
            dst[...] = src[...].astype(dst.dtype)
    return wrapped


def _cast_specs(weights, grid):
    n_steps = 1
    for n in grid:
        n_steps *= n

    def step(*idx):
        s = 0
        for i, n in zip(idx, grid):
            s = s * n + i
        return s

    specs, shapes = [], []
    for w, n_cols in weights:
        rows = w.shape[0] // n_steps
        assert rows * n_steps == w.shape[0] and rows % BF16_ROWS == 0 and n_cols % LANES == 0
        specs.append(pl.BlockSpec((rows, n_cols), lambda *idx: (step(*idx), 0)))
        shapes.append(jax.ShapeDtypeStruct((w.shape[0], n_cols), BF16))
    return specs, shapes


def _layer_norm(z, g, b):
    mu = jnp.mean(z, axis=-1, keepdims=True)
    zc = z - mu
    var = jnp.mean(zc * zc, axis=-1, keepdims=True)
    return zc * lax.rsqrt(var + LN_EPS) * g + b


def _norm_rope(yh, gain, cos, sin):
    mean_w = jnp.full((HEAD_DIM, HEAD_DIM), 1.0 / HEAD_DIM, BF16)
    sq = yh * yh
    hi = sq.astype(BF16)
    lo = (sq - hi.astype(F32)).astype(BF16)
    ms = jnp.dot(hi, mean_w, preferred_element_type=F32) + jnp.dot(lo, mean_w, preferred_element_type=F32)
    yn = yh * lax.rsqrt(ms + RMS_EPS) * gain
    half = ROPE_AXIS_DIM // 2
    lane = lax.broadcasted_iota(jnp.int32, yn.shape, 1)
    partner = jnp.where(lane % ROPE_AXIS_DIM < half, pltpu.roll(yn, HEAD_DIM - half, 1), pltpu.roll(yn, half, 1))
    return yn * cos + partner * sin


def _proj_q_kernel(x_ref, w_ref, gain_ref, cos_ref, sin_ref, o_ref):
    y = jnp.dot(x_ref[...], w_ref[...], preferred_element_type=F32)
    cos = cos_ref[...]
    sin = sin_ref[...]
    for h in range(y.shape[1] // HEAD_DIM):
        sl = slice(h * HEAD_DIM, (h + 1) * HEAD_DIM)
        o_ref[:, sl] = _norm_rope(y[:, sl], gain_ref[:, sl], cos, sin).astype(o_ref.dtype)


def _proj_kv_kernel(x_ref, w_ref, gain_ref, cos_ref, sin_ref, o_ref):
    y = jnp.dot(x_ref[...], w_ref[...], preferred_element_type=F32)
    cos = cos_ref[...]
    sin = sin_ref[...]
    for h in range(A_KV_HEADS):
        sl = slice(h * HEAD_DIM, (h + 1) * HEAD_DIM)
        o_ref[h] = _norm_rope(y[:, sl], gain_ref[:, sl], cos, sin).astype(o_ref.dtype)
    for h in range(A_KV_HEADS, 2 * A_KV_HEADS):
        o_ref[h] = y[:, h * HEAD_DIM:(h + 1) * HEAD_DIM].astype(o_ref.dtype)


def _proj_gate_kernel(x_ref, w_ref, b_ref, o_ref, xb_ref):
    @pl.when(pl.program_id(1) == 0)
    def _():
        xb_ref[...] = x_ref[...].astype(xb_ref.dtype)

    y = jnp.dot(xb_ref[...], w_ref[...], preferred_element_type=F32)
    o_ref[...] = (0.5 * jnp.tanh(0.5 * (y + b_ref[...])) + 0.5).astype(o_ref.dtype)


def _proj_gate(x2d, w, bias, casts, *, tm, tn):
    t, d = x2d.shape
    n_cols = w.shape[1]
    grid = (t // tm, n_cols // tn)
    cast_specs, cast_shapes = _cast_specs(casts, grid)
    return pl.pallas_call(
        _with_casts(_proj_gate_kernel, 3, 2, len(casts)),
        grid=grid,
        in_specs=[pl.BlockSpec((tm, d), lambda i, j: (i, 0)),
                  pl.BlockSpec((d, tn), lambda i, j: (0, j)),
                  pl.BlockSpec((1, tn), lambda i, j: (0, j))] + cast_specs,
        out_specs=[pl.BlockSpec((tm, tn), lambda i, j: (i, j)), pl.BlockSpec((tm, d), lambda i, j: (i, 0))] + cast_specs,
        out_shape=[jax.ShapeDtypeStruct((t, n_cols), BF16), jax.ShapeDtypeStruct((t, d), BF16)] + cast_shapes,
        compiler_params=_params("parallel", "arbitrary"),
    )(x2d, w, bias, *[w_f32 for w_f32, _ in casts])


def _in_proj(kern, xb, w, col_off, n_cols, row_vecs, tabs=(), *, tm, tn, seq_len, head_major=False):
    t, d = xb.shape
    off = col_off // tn
    pos_blocks = seq_len // tm
    in_specs = [pl.BlockSpec((tm, d), lambda i, j: (i, 0)),
                pl.BlockSpec((d, tn), lambda i, j: (0, off + j))]
    in_specs += [pl.BlockSpec((1, tn), lambda i, j: (0, j)) for _ in row_vecs]
    in_specs += [pl.BlockSpec((tm, HEAD_DIM), lambda i, j: (i % pos_blocks, 0)) for _ in tabs]
    if head_major:
        assert n_cols == tn
        out_spec = pl.BlockSpec((tn // HEAD_DIM, tm, HEAD_DIM), lambda i, j: (0, i, 0))
        out_shape = jax.ShapeDtypeStruct((tn // HEAD_DIM, t, HEAD_DIM), BF16)
    else:
        out_spec = pl.BlockSpec((tm, tn), lambda i, j: (i, j))
        out_shape = jax.ShapeDtypeStruct((t, n_cols), BF16)
    return pl.pallas_call(
        kern,
        grid=(t // tm, n_cols // tn),
        in_specs=in_specs,
        out_specs=out_spec,
        out_shape=out_shape,
        compiler_params=_params("parallel", "arbitrary"),
    )(xb, w, *row_vecs, *tabs)


def _gqa_kernel(q_ref, k_ref, v_ref, o_ref, m_sc, l_sc, acc_sc, s0_sc, s1_sc, *, tk):
    tq = q_ref.shape[1]
    n_chunks = k_ref.shape[1] // tk
    s_sc = (s0_sc, s1_sc)
    q = jnp.concatenate([q_ref[0, :, r * HEAD_DIM:(r + 1) * HEAD_DIM] for r in range(A_REP)], axis=0)
    m_sc[...] = jnp.full(m_sc.shape, -jnp.inf, F32)
    l_sc[...] = jnp.zeros(l_sc.shape, F32)
    acc_sc[...] = jnp.zeros(acc_sc.shape, F32)

    def scores(c, slot):
        s_sc[slot][...] = lax.dot_general(k_ref[0, c * tk:(c + 1) * tk, :], q, NT_DIMS, preferred_element_type=F32)

    def softmax_pv(c, slot):
        v = v_ref[0, c * tk:(c + 1) * tk, :]
        st = s_sc[slot][...]
        m_prev = m_sc[...]
        m_new = jnp.maximum(m_prev, jnp.max(st, axis=0, keepdims=True))
        alpha = jnp.exp2(m_prev - m_new)
        pt = jnp.exp2(st - m_new)
        l_sc[...] = alpha * l_sc[...] + jnp.sum(pt, axis=0, keepdims=True)
        pv = lax.dot_general(v, pt.astype(BF16), TN_DIMS, preferred_element_type=F32)
        acc_sc[...] = alpha * acc_sc[...] + pv
        m_sc[...] = m_new

    scores(0, 0)
    for c in range(n_chunks - 1):
        scores(c + 1, (c + 1) % 2)
        softmax_pv(c, c % 2)
    softmax_pv(n_chunks - 1, (n_chunks - 1) % 2)
    ot = acc_sc[...] / l_sc[...]
    for r in range(A_REP):
        o_ref[0, :, r * HEAD_DIM:(r + 1) * HEAD_DIM] = ot[:, r * tq:(r + 1) * tq].T.astype(o_ref.dtype)


def _gqa(qa, kv, casts, *, tq, tk):
    b, s, _ = qa.shape
    assert s % tk == 0 and s % tq == 0
    gw = A_REP * HEAD_DIM
    rows = A_REP * tq
    grid = (b, A_KV_HEADS, s // tq)
    cast_specs, cast_shapes = _cast_specs(casts, grid)
    return pl.pallas_call(
        _with_casts(functools.partial(_gqa_kernel, tk=tk), 3, 1, len(casts)),
        grid=grid,
        in_specs=[pl.BlockSpec((1, tq, gw), lambda bi, g, i: (bi, i, g)),
                  pl.BlockSpec((None, 1, s, HEAD_DIM), lambda bi, g, i: (g, bi, 0, 0)),
                  pl.BlockSpec((None, 1, s, HEAD_DIM), lambda bi, g, i: (A_KV_HEADS + g, bi, 0, 0))] + cast_specs,
        out_specs=[pl.BlockSpec((1, tq, gw), lambda bi, g, i: (bi, i, g))] + cast_specs,
        out_shape=[jax.ShapeDtypeStruct((b, s, A_Q_W), BF16)] + cast_shapes,
        scratch_shapes=[pltpu.VMEM((1, rows), F32), pltpu.VMEM((1, rows), F32), pltpu.VMEM((HEAD_DIM, rows), F32),
                        pltpu.VMEM((tk, rows), F32), pltpu.VMEM((tk, rows), F32)],
        compiler_params=_params("parallel", "parallel", "arbitrary"),
    )(qa, kv, kv, *[w_f32 for w_f32, _ in casts])


def _proj_deint_kernel(x_ref, wq_ref, wk_ref, wv_ref, o_ref, *stage, dil):
    x = x_ref[...]
    tm = x.shape[0]
    for which, w_ref in enumerate((wq_ref, wk_ref, wv_ref)):
        y = jnp.dot(x, w_ref[...], preferred_element_type=F32)
        if which == 0:
            y = y * Q_SCALE
        tn = y.shape[1]
        if dil == 1:
            o_ref[0, 0, :, which * tn:(which + 1) * tn] = y.astype(o_ref.dtype)
            continue
        y_sc = stage[which]
        n_slabs = tn // LANES
        for s in range(n_slabs):
            y_sc[s] = y[:, s * LANES:(s + 1) * LANES]
        if dil == FAST_STRIDE ** 2:
            z_sc = stage[3 + which]
            part = tm // FAST_STRIDE
            for lo in range(FAST_STRIDE):
                for s in range(n_slabs):
                    z_sc[s, lo * part:(lo + 1) * part, :] = y_sc[s, pl.ds(lo, part, stride=FAST_STRIDE), :]
            for lo in range(FAST_STRIDE):
                for hi in range(FAST_STRIDE):
                    for s in range(n_slabs):
                        c0 = which * tn + s * LANES
                        rows = z_sc[s, pl.ds(lo * part + hi, tm // dil, stride=FAST_STRIDE), :]
                        o_ref[0, lo + FAST_STRIDE * hi, :, c0:c0 + LANES] = rows.astype(o_ref.dtype)
            continue
        for r in range(dil):
            for s in range(n_slabs):
                c0 = which * tn + s * LANES
                o_ref[0, r, :, c0:c0 + LANES] = y_sc[s, pl.ds(r, tm // dil, stride=dil), :].astype(o_ref.dtype)


def _proj_deint(xb, w, g, casts=(), *, tm, bsz, seq_len):
    t, d = xb.shape
    dil = B_DILATIONS[g]
    tn = B_OUT_W
    tiles_per_seq = seq_len // tm
    col0 = COL_QB // tn
    w_spec = lambda which: pl.BlockSpec((d, tn), lambda i: (0, col0 + B_GROUPS * which + g))
    grid = (t // tm,)
    cast_specs, cast_shapes = _cast_specs(casts, grid)
    out_spec = pl.BlockSpec((1, dil, tm // dil, 3 * tn), lambda i: (i // tiles_per_seq, 0, i % tiles_per_seq, 0))
    qkv, *cast_out = pl.pallas_call(
        _with_casts(functools.partial(_proj_deint_kernel, dil=dil), 4, 1, len(casts)),
        grid=grid,
        in_specs=[pl.BlockSpec((tm, d), lambda i: (i, 0)), w_spec(0), w_spec(1), w_spec(2)] + cast_specs,
        out_specs=[out_spec] + cast_specs,
        out_shape=[jax.ShapeDtypeStruct((bsz, dil, seq_len // dil, 3 * tn), BF16)] + cast_shapes,
        scratch_shapes=[pltpu.VMEM((tn // LANES, tm, LANES), F32)] * {1: 0, FAST_STRIDE ** 2: 6}.get(dil, 3),
        compiler_params=_params("parallel"),
    )(xb, w, w, w, *[w_f32 for w_f32, _ in casts])
    return qkv, cast_out


def _dilated_kernel(q_ref, k_ref, v_ref, o_ref, lse_ref, *token_order_scratch, dil, slopes, n_blocks, n_res):
    qb = 2 * B_SIDE
    kw = 4 * B_SIDE
    n_rows = k_ref.shape[2]
    row = lax.broadcasted_iota(jnp.int32, (qb, kw), 0)
    col = lax.broadcasted_iota(jnp.int32, (qb, kw), 1)

    def attend(r, blk, h, ks, valid, token_dist):
        sl = slice(h * HEAD_DIM, (h + 1) * HEAD_DIM)
        q = q_ref[0, r, blk * qb:(blk + 1) * qb, sl]
        k = k_ref[0, r, pl.ds(ks, kw), sl]
        v = v_ref[0, r, pl.ds(ks, kw), sl]
        s = lax.dot_general(q, k, NT_DIMS, preferred_element_type=F32) - slopes[h] * token_dist
        s = jnp.where(valid, s, -jnp.inf)
        m = jnp.max(s, axis=-1, keepdims=True)
        p = jnp.exp(s - m)
        l = jnp.sum(p, axis=-1, keepdims=True)
        o = jnp.dot(p.astype(BF16), v, preferred_element_type=F32) / l
        if dil == 1:
            o_ref[0, blk * qb:(blk + 1) * qb, sl] = o.astype(o_ref.dtype)
        else:
            o_sc, _ = token_order_scratch
            o_sc[h, pl.ds(blk * qb * dil + r, qb, stride=dil), :] = o
        return m + jnp.log(l)

    lane = lax.broadcasted_iota(jnp.int32, (qb, LANES), 1)

    def residues(it, carry):
        for blk in range(n_blocks):
            q0 = (pl.program_id(1) * n_blocks + blk) * qb
            ks = pl.multiple_of(jnp.clip(q0 - B_SIDE, 0, n_rows - kw), B_SIDE)
            dist = jnp.abs(col - row + (ks - q0))
            valid = dist <= B_SIDE
            token_dist = dist.astype(F32) * float(dil)
            for u in range(n_res):
                r = it * n_res + u
                lse = None
                for h in reversed(range(B_HEADS_PER_GROUP)):
                    col_h = jnp.broadcast_to(attend(r, blk, h, ks, valid, token_dist), (qb, LANES))
                    lse = col_h if lse is None else jnp.where(lane < (h + 1) * LSE_LANES, col_h, lse)
                if dil == 1:
                    lse_ref[0, blk * qb:(blk + 1) * qb, :] = lse
                else:
                    _, l_sc = token_order_scratch
                    l_sc[pl.ds(blk * qb * dil + r, qb, stride=dil), :] = lse
        return carry

    if dil == 1:
        residues(0, 0)
    else:
        o_sc, l_sc = token_order_scratch
        lax.fori_loop(0, dil // n_res, residues, 0)
        for h in range(B_HEADS_PER_GROUP):
            sl = slice(h * HEAD_DIM, (h + 1) * HEAD_DIM)
            o_ref[0, :, sl] = o_sc[h].astype(o_ref.dtype)
        lse_ref[0] = l_sc[...]


DILATED_BATCHING = ((8, 1), (4, 1), (1, 4))


def _dilated_group(qkv, g):
    b, dil, n_rows, _ = qkv.shape
    s = n_rows * dil
    n_blocks, n_res = DILATED_BATCHING[g]
    rows = n_blocks * 2 * B_SIDE
    assert n_rows >= 4 * B_SIDE and n_rows % rows == 0 and dil % n_res == 0
    slopes = tuple(2.0 ** (-8.0 * (g * B_HEADS_PER_GROUP + h + 1) / B_HEADS) for h in range(B_HEADS_PER_GROUP))
    kv_spec = lambda which: pl.BlockSpec((1, dil, n_rows, B_OUT_W), lambda bi, i: (bi, 0, 0, which))
    out_spec = lambda w: pl.BlockSpec((1, rows * dil, w), lambda bi, i: (bi, i, 0))
    staging = [pltpu.VMEM((B_HEADS_PER_GROUP, rows * dil, LANES), F32), pltpu.VMEM((rows * dil, LANES), F32)]
    o, lse = pl.pallas_call(
        functools.partial(_dilated_kernel, dil=dil, slopes=slopes, n_blocks=n_blocks, n_res=n_res),
        grid=(b, n_rows // rows),
        in_specs=[pl.BlockSpec((1, dil, rows, B_OUT_W), lambda bi, i: (bi, 0, i, 0)), kv_spec(1), kv_spec(2)],
        out_specs=[out_spec(B_OUT_W), out_spec(LANES)],
        out_shape=[jax.ShapeDtypeStruct((b, s, B_OUT_W), BF16), jax.ShapeDtypeStruct((b, s, LANES), F32)],
        scratch_shapes=staging if dil > 1 else [],
        compiler_params=_params("parallel", "arbitrary"),
    )(qkv, qkv, qkv)
    return o.reshape(b * s, B_OUT_W), lse.reshape(b * s, LANES)


def _mix_out_kernel(oa_ref, o0_ref, o1_ref, o2_ref, l0_ref, l1_ref, l2_ref, ga_ref, gb_ref, x_ref,
                    wpa_ref, wpb_ref, wo_ref, g_ref, b_ref, out_ref):
    l0, l1, l2 = l0_ref[...], l1_ref[...], l2_ref[...]
    mx = jnp.maximum(jnp.maximum(l0, l1), l2)
    e0, e1, e2 = jnp.exp(l0 - mx), jnp.exp(l1 - mx), jnp.exp(l2 - mx)
    inv = 1.0 / (e0 + e1 + e2)
    weights = (e0 * inv, e1 * inv, e2 * inv)
    rows = l0.shape[0]
    merged = []
    for h in range(B_HEADS_PER_GROUP):
        sl = slice(h * HEAD_DIM, (h + 1) * HEAD_DIM)
        acc = None
        for w, o_ref in zip(weights, (o0_ref, o1_ref, o2_ref)):
            term = jnp.broadcast_to(w[:, h * LSE_LANES:h * LSE_LANES + 1], (rows, HEAD_DIM)) * o_ref[:, sl]
            acc = term if acc is None else acc + term
        merged.append(acc.astype(BF16))
    ob = jnp.concatenate(merged, axis=1)
    pa = jnp.dot(oa_ref[...], wpa_ref[...], preferred_element_type=F32)
    pb = jnp.dot(ob, wpb_ref[...], preferred_element_type=F32)
    mixed = (ga_ref[...].astype(F32) * pa + gb_ref[...].astype(F32) * pb).astype(BF16)
    y = jnp.dot(mixed, wo_ref[...], preferred_element_type=F32)
    out_ref[...] = _layer_norm(ALPHA * x_ref[...] + y, g_ref[...], b_ref[...])


def _mix_out(out_a, o_groups, lse_groups, gates, x2d, wpa, wpb, wo, g, b, *, tm):
    t = out_a.shape[0]
    row = lambda w, cb=0: pl.BlockSpec((tm, w), lambda i: (i, cb))
    const = lambda a: pl.BlockSpec(a.shape, lambda i: (0, 0), pipeline_mode=pl.Buffered(1))
    return pl.pallas_call(
        _mix_out_kernel,
        grid=(t // tm,),
        in_specs=[row(A_Q_W)] + [row(B_OUT_W)] * 3 + [row(LANES)] * 3 + [row(D_MODEL, 0), row(D_MODEL, 1), row(D_MODEL),
                  const(wpa), const(wpb), const(wo), const(g), const(b)],
        out_specs=row(D_MODEL),
        out_shape=jax.ShapeDtypeStruct((t, D_MODEL), F32),
        compiler_params=_params("parallel"),
    )(out_a, *o_groups, *lse_groups, gates, gates, x2d, wpa, wpb, wo, g, b)


def _ffn_kernel(x_ref, wg_ref, wu_ref, wd_ref, g_ref, b_ref, o_ref, xb_sc, acc_sc):
    f = pl.program_id(1)

    @pl.when(f == 0)
    def _():
        xb_sc[...] = x_ref[...].astype(BF16)
        acc_sc[...] = jnp.zeros(acc_sc.shape, F32)

    xb = xb_sc[...]
    gate = jnp.dot(xb, wg_ref[...], preferred_element_type=F32)
    up = jnp.dot(xb, wu_ref[...], preferred_element_type=F32)
    h = gate * (0.5 * jnp.tanh(0.5 * gate) + 0.5) * up
    acc_sc[...] += jnp.dot(h.astype(BF16), wd_ref[...], preferred_element_type=F32)

    @pl.when(f == pl.num_programs(1) - 1)
    def _():
        o_ref[...] = _layer_norm(ALPHA * x_ref[...] + acc_sc[...], g_ref[...], b_ref[...])


def _ffn(x1, wg, wu, wd, g, b, *, tm, tf):
    t, d = x1.shape
    hidden = wg.shape[1]
    row = pl.BlockSpec((tm, d), lambda i, f: (i, 0))
    vec = pl.BlockSpec((1, d), lambda i, f: (0, 0))
    return pl.pallas_call(
        _ffn_kernel,
        grid=(t // tm, hidden // tf),
        in_specs=[row,
                  pl.BlockSpec((d, tf), lambda i, f: (0, f)),
                  pl.BlockSpec((d, tf), lambda i, f: (0, f)),
                  pl.BlockSpec((tf, d), lambda i, f: (f, 0)),
                  vec, vec],
        out_specs=row,
        out_shape=jax.ShapeDtypeStruct((t, d), F32),
        scratch_shapes=[pltpu.VMEM((tm, d), BF16), pltpu.VMEM((tm, d), F32)],
        compiler_params=_params("parallel", "arbitrary"),
    )(x1, wg, wu, wd, g, b)


def _rope_tables(seq_len):
    pos = np.arange(seq_len)
    n_freq = ROPE_AXIS_DIM // 2
    freqs = (np.float32(ROPE_THETA) ** (-np.arange(n_freq, dtype=np.float32) / np.float32(n_freq))).astype(np.float32)
    ang_r = (pos // GRID_W).astype(np.float32)[:, None] * freqs[None, :]
    ang_c = (pos % GRID_W).astype(np.float32)[:, None] * freqs[None, :]
    cos = np.concatenate([np.cos(ang_r)] * 2 + [np.cos(ang_c)] * 2, axis=-1).astype(np.float32)
    sin = np.concatenate([-np.sin(ang_r), np.sin(ang_r), -np.sin(ang_c), np.sin(ang_c)], axis=-1).astype(np.float32)
    return jnp.asarray(cos), jnp.asarray(sin)


def _layer(x, w_in, b_gate, q_norm_a, k_norm_a, w_proj_a, w_proj_b, w_out,
           ln1_g, ln1_b, w_ffn_gate, w_ffn_up, w_ffn_down, ln2_g, ln2_b, rope):
    bsz, seq_len, d = x.shape
    t = bsz * seq_len
    x2d = x.reshape(t, d)
    vec = lambda a: a.reshape(1, -1).astype(F32)

    gates, xb, w_in_b = _proj_gate(x2d, w_in[:, COL_GA:].astype(BF16), vec(b_gate), [(w_in, COL_GA)], tm=1024, tn=1024)

    q_gain = jnp.tile(q_norm_a * (Q_SCALE * LOG2_E), A_Q_HEADS)
    kv_gain = jnp.concatenate([jnp.tile(k_norm_a, A_KV_HEADS), jnp.ones((A_KV_W,), F32)])
    proj = functools.partial(_in_proj, tm=1024, seq_len=seq_len)
    qa = proj(_proj_q_kernel, xb, w_in_b, COL_QA, A_Q_W, [vec(q_gain)], rope, tn=512)
    kv = proj(_proj_kv_kernel, xb, w_in_b, COL_KA, 2 * A_KV_W, [vec(kv_gain)], rope, tn=2 * A_KV_W, head_major=True)

    out_a, w_gate_b, w_up_b = _gqa(qa.reshape(bsz, seq_len, A_Q_W), kv.reshape(2 * A_KV_HEADS, bsz, seq_len, HEAD_DIM),
                                   [(w_ffn_gate, FFN_HIDDEN), (w_ffn_up, FFN_HIDDEN)], tq=256, tk=1024)
    casts = [[(w_ffn_down, D_MODEL)], [(w_proj_a, D_MODEL), (w_proj_b, D_MODEL), (w_out, D_MODEL)], []]
    groups, cast_out = [], []
    for g in range(B_GROUPS):
        qkv, converted = _proj_deint(xb, w_in_b, g, casts[g], tm=1024, bsz=bsz, seq_len=seq_len)
        groups.append(_dilated_group(qkv, g))
        cast_out += converted
    w_down_b, w_proj_a_b, w_proj_b_b, w_out_b = cast_out
    x1 = _mix_out(out_a.reshape(t, A_Q_W), [o for o, _ in groups], [l for _, l in groups], gates, x2d,
                  w_proj_a_b, w_proj_b_b, w_out_b, vec(ln1_g), vec(ln1_b), tm=512)
    x2 = _ffn(x1, w_gate_b, w_up_b, w_down_b, vec(ln2_g), vec(ln2_b), tm=512, tf=512)
    return x2.reshape(bsz, seq_len, d)


def kernel(x, w_in, b_gate, q_norm_a, k_norm_a, w_proj_a, w_proj_b, w_out, ln1_g, ln1_b, w_ffn_gate, w_ffn_up, w_ffn_down, ln2_g, ln2_b):
    rope = _rope_tables(x.shape[1])
    for l in range(w_in.shape[0]):
        x = _layer(x, w_in[l], b_gate[l], q_norm_a[l], k_norm_a[l], w_proj_a[l], w_proj_b[l], w_out[l],
                   ln1_g[l], ln1_b[l], w_ffn_gate[l], w_ffn_up[l], w_ffn_down[l], ln2_g[l], ln2_b[l], rope)
    return x
```

```python
import functools

import jax
import jax.numpy as jnp
import numpy as np
from jax import lax
from jax.experimental import pallas as pl
from jax.experimental.pallas import tpu as pltpu

F32 = jnp.float32
BF16 = jnp.bfloat16

D_MODEL = 2048
HEAD_DIM = 128
A_Q_HEADS = 8
A_KV_HEADS = 2
A_REP = A_Q_HEADS // A_KV_HEADS
ROPE_THETA = 10000.0
ROPE_AXIS_DIM = HEAD_DIM // 2
GRID_W = 64
B_DILATIONS = (1, 4, 16)
B_GROUPS = 3
B_HEADS_PER_GROUP = 4
B_HEADS = B_GROUPS * B_HEADS_PER_GROUP
B_SIDE = 64
A_Q_W = A_Q_HEADS * HEAD_DIM
A_KV_W = A_KV_HEADS * HEAD_DIM
B_W = B_HEADS * HEAD_DIM
B_OUT_W = B_HEADS_PER_GROUP * HEAD_DIM
LSE_LANES = HEAD_DIM // B_HEADS_PER_GROUP
FFN_HIDDEN = 5632
ALPHA = 2.0 ** 0.25
RMS_EPS = 1e-6
LN_EPS = 1e-5
Q_SCALE = HEAD_DIM ** -0.5

COL_QA = 0
COL_KA = A_Q_W
COL_VA = A_Q_W + A_KV_W
COL_QB = COL_VA + A_KV_W
COL_GA = COL_QB + 3 * B_W

VMEM_LIMIT = 60 * 1024 * 1024

NT_DIMS = (((1,), (1,)), ((), ()))
TN_DIMS = (((0,), (0,)), ((), ()))
LANES = 128
BF16_ROWS = 16
FAST_STRIDE = 4
LOG2_E = 1.4426950408889634


def _params(*sem):
    return pltpu.CompilerParams(dimension_semantics=sem, vmem_limit_bytes=VMEM_LIMIT)


def _with_casts(kernel, n_in, n_out, n_casts):
    def wrapped(*refs):
        ins, rest = refs[:n_in], refs[n_in:]
        cast_in, rest = rest[:n_casts], rest[n_casts:]
        outs, rest = rest[:n_out], rest[n_out:]
        cast_out, scratch = rest[:n_casts], rest[n_casts:]
        kernel(*ins, *outs, *scratch)
        for src, dst in zip(cast_in, cast_out):
            dst[...] = src[...].astype(dst.dtype)
    return wrapped


def _cast_specs(weights, grid):
    n_steps = 1
    for n in grid:
        n_steps *= n

    def step(*idx):
        s = 0
        for i, n in zip(idx, grid):
            s = s * n + i
        return s

    specs, shapes = [], []
    for w, n_cols in weights:
        rows = w.shape[0] // n_steps
        assert rows * n_steps == w.shape[0] and rows % BF16_ROWS == 0 and n_cols % LANES == 0
        specs.append(pl.BlockSpec((rows, n_cols), lambda *idx: (step(*idx), 0)))
        shapes.append(jax.ShapeDtypeStruct((w.shape[0], n_cols), BF16))
    return specs, shapes


def _layer_norm(z, g, b):
    mu = jnp.mean(z, axis=-1, keepdims=True)
    zc = z - mu
    var = jnp.mean(zc * zc, axis=-1, keepdims=True)
    return zc * lax.rsqrt(var + LN_EPS) * g + b


def _norm_rope(yh, gain, cos, sin):
    mean_w = jnp.full((HEAD_DIM, HEAD_DIM), 1.0 / HEAD_DIM, BF16)
    sq = yh * yh
    hi = sq.astype(BF16)
    lo = (sq - hi.astype(F32)).astype(BF16)
    ms = jnp.dot(hi, mean_w, preferred_element_type=F32) + jnp.dot(lo, mean_w, preferred_element_type=F32)
    yn = yh * lax.rsqrt(ms + RMS_EPS) * gain
    half = ROPE_AXIS_DIM // 2
    lane = lax.broadcasted_iota(jnp.int32, yn.shape, 1)
    partner = jnp.where(lane % ROPE_AXIS_DIM < half, pltpu.roll(yn, HEAD_DIM - half, 1), pltpu.roll(yn, half, 1))
    return yn * cos + partner * sin


def _proj_a_kernel(x_ref, w_ref, gain_ref, cos_ref, sin_ref, q_ref, kv_ref):
    y = jnp.dot(x_ref[...], w_ref[...], preferred_element_type=F32)
    cos = cos_ref[...]
    sin = sin_ref[...]
    head = lambda c: slice(c * HEAD_DIM, (c + 1) * HEAD_DIM)
    for h in range(A_Q_HEADS):
        q_ref[:, head(h)] = _norm_rope(y[:, head(h)], gain_ref[:, head(h)], cos, sin).astype(q_ref.dtype)
    for h in range(A_KV_HEADS):
        c = A_Q_HEADS + h
        kv_ref[h] = _norm_rope(y[:, head(c)], gain_ref[:, head(c)], cos, sin).astype(kv_ref.dtype)
    for h in range(A_KV_HEADS):
        kv_ref[A_KV_HEADS + h] = y[:, head(A_Q_HEADS + A_KV_HEADS + h)].astype(kv_ref.dtype)


def _proj_a(xb, w, gain, rope, *, tm, seq_len):
    t, d = xb.shape
    n_cols = A_Q_W + 2 * A_KV_W
    pos_blocks = seq_len // tm
    tab = pl.BlockSpec((tm, HEAD_DIM), lambda i: (i % pos_blocks, 0))
    return pl.pallas_call(
        _proj_a_kernel,
        grid=(t // tm,),
        in_specs=[pl.BlockSpec((tm, d), lambda i: (i, 0)),
                  pl.BlockSpec((d, n_cols), lambda i: (0, COL_QA // n_cols)),
                  pl.BlockSpec((1, A_Q_W + A_KV_W), lambda i: (0, 0)), tab, tab],
        out_specs=[pl.BlockSpec((tm, A_Q_W), lambda i: (i, 0)),
                   pl.BlockSpec((2 * A_KV_HEADS, tm, HEAD_DIM), lambda i: (0, i, 0))],
        out_shape=[jax.ShapeDtypeStruct((t, A_Q_W), BF16), jax.ShapeDtypeStruct((2 * A_KV_HEADS, t, HEAD_DIM), BF16)],
        compiler_params=_params("parallel"),
    )(xb, w, gain, *rope)


def _proj_gate_kernel(x_ref, w_ref, b_ref, o_ref, xb_ref):
    @pl.when(pl.program_id(1) == 0)
    def _():
        xb_ref[...] = x_ref[...].astype(xb_ref.dtype)

    y = jnp.dot(xb_ref[...], w_ref[...], preferred_element_type=F32)
    o_ref[...] = (0.5 * jnp.tanh(0.5 * (y + b_ref[...])) + 0.5).astype(o_ref.dtype)


def _proj_gate(x2d, w, bias, casts, *, tm, tn):
    t, d = x2d.shape
    n_cols = w.shape[1]
    grid = (t // tm, n_cols // tn)
    cast_specs, cast_shapes = _cast_specs(casts, grid)
    return pl.pallas_call(
        _with_casts(_proj_gate_kernel, 3, 2, len(casts)),
        grid=grid,
        in_specs=[pl.BlockSpec((tm, d), lambda i, j: (i, 0)),
                  pl.BlockSpec((d, tn), lambda i, j: (0, j)),
                  pl.BlockSpec((1, tn), lambda i, j: (0, j))] + cast_specs,
        out_specs=[pl.BlockSpec((tm, tn), lambda i, j: (i, j)), pl.BlockSpec((tm, d), lambda i, j: (i, 0))] + cast_specs,
        out_shape=[jax.ShapeDtypeStruct((t, n_cols), BF16), jax.ShapeDtypeStruct((t, d), BF16)] + cast_shapes,
        compiler_params=_params("parallel", "arbitrary"),
    )(x2d, w, bias, *[w_f32 for w_f32, _ in casts])


def _gqa_kernel(q_ref, k_ref, v_ref, o_ref, m_sc, l_sc, acc_sc, s0_sc, s1_sc, *, tk):
    tq = q_ref.shape[1]
    n_chunks = k_ref.shape[1] // tk
    s_sc = (s0_sc, s1_sc)
    q = jnp.concatenate([q_ref[0, :, r * HEAD_DIM:(r + 1) * HEAD_DIM] for r in range(A_REP)], axis=0)
    m_sc[...] = jnp.full(m_sc.shape, -jnp.inf, F32)
    l_sc[...] = jnp.zeros(l_sc.shape, F32)
    acc_sc[...] = jnp.zeros(acc_sc.shape, F32)

    def scores(c, slot):
        s_sc[slot][...] = lax.dot_general(k_ref[0, c * tk:(c + 1) * tk, :], q, NT_DIMS, preferred_element_type=F32)

    def softmax_pv(c, slot):
        v = v_ref[0, c * tk:(c + 1) * tk, :]
        st = s_sc[slot][...]
        m_prev = m_sc[...]
        m_new = jnp.maximum(m_prev, jnp.max(st, axis=0, keepdims=True))
        alpha = jnp.exp2(m_prev - m_new)
        pt = jnp.exp2(st - m_new)
        l_sc[...] = alpha * l_sc[...] + jnp.sum(pt, axis=0, keepdims=True)
        pv = lax.dot_general(v, pt.astype(BF16), TN_DIMS, preferred_element_type=F32)
        acc_sc[...] = alpha * acc_sc[...] + pv
        m_sc[...] = m_new

    scores(0, 0)
    for c in range(n_chunks - 1):
        scores(c + 1, (c + 1) % 2)
        softmax_pv(c, c % 2)
    softmax_pv(n_chunks - 1, (n_chunks - 1) % 2)
    ot = acc_sc[...] / l_sc[...]
    for r in range(A_REP):
        o_ref[0, :, r * HEAD_DIM:(r + 1) * HEAD_DIM] = ot[:, r * tq:(r + 1) * tq].T.astype(o_ref.dtype)


def _gqa(qa, kv, casts, *, tq, tk):
    b, s, _ = qa.shape
    assert s % tk == 0 and s % tq == 0
    gw = A_REP * HEAD_DIM
    rows = A_REP * tq
    grid = (b, A_KV_HEADS, s // tq)
    cast_specs, cast_shapes = _cast_specs(casts, grid)
    return pl.pallas_call(
        _with_casts(functools.partial(_gqa_kernel, tk=tk), 3, 1, len(casts)),
        grid=grid,
        in_specs=[pl.BlockSpec((1, tq, gw), lambda bi, g, i: (bi, i, g)),
                  pl.BlockSpec((None, 1, s, HEAD_DIM), lambda bi, g, i: (g, bi, 0, 0)),
                  pl.BlockSpec((None, 1, s, HEAD_DIM), lambda bi, g, i: (A_KV_HEADS + g, bi, 0, 0))] + cast_specs,
        out_specs=[pl.BlockSpec((1, tq, gw), lambda bi, g, i: (bi, i, g))] + cast_specs,
        out_shape=[jax.ShapeDtypeStruct((b, s, A_Q_W), BF16)] + cast_shapes,
        scratch_shapes=[pltpu.VMEM((1, rows), F32), pltpu.VMEM((1, rows), F32), pltpu.VMEM((HEAD_DIM, rows), F32),
                        pltpu.VMEM((tk, rows), F32), pltpu.VMEM((tk, rows), F32)],
        compiler_params=_params("parallel", "parallel", "arbitrary"),
    )(qa, kv, kv, *[w_f32 for w_f32, _ in casts])


def _proj_deint_kernel(x_ref, wq_ref, wk_ref, wv_ref, o_ref, *stage, dil):
    x = x_ref[...]
    tm = x.shape[0]
    for which, w_ref in enumerate((wq_ref, wk_ref, wv_ref)):
        y = jnp.dot(x, w_ref[...], preferred_element_type=F32)
        if which == 0:
            y = y * Q_SCALE
        tn = y.shape[1]
        if dil == 1:
            o_ref[0, 0, :, which * tn:(which + 1) * tn] = y.astype(o_ref.dtype)
            continue
        y_sc = stage[which]
        n_slabs = tn // LANES
        for s in range(n_slabs):
            y_sc[s] = y[:, s * LANES:(s + 1) * LANES]
        if dil == FAST_STRIDE ** 2:
            z_sc = stage[3 + which]
            part = tm // FAST_STRIDE
            for lo in range(FAST_STRIDE):
                for s in range(n_slabs):
                    z_sc[s, lo * part:(lo + 1) * part, :] = y_sc[s, pl.ds(lo, part, stride=FAST_STRIDE), :]
            for lo in range(FAST_STRIDE):
                for hi in range(FAST_STRIDE):
                    for s in range(n_slabs):
                        c0 = which * tn + s * LANES
                        rows = z_sc[s, pl.ds(lo * part + hi, tm // dil, stride=FAST_STRIDE), :]
                        o_ref[0, lo + FAST_STRIDE * hi, :, c0:c0 + LANES] = rows.astype(o_ref.dtype)
            continue
        for r in range(dil):
            for s in range(n_slabs):
                c0 = which * tn + s * LANES
                o_ref[0, r, :, c0:c0 + LANES] = y_sc[s, pl.ds(r, tm // dil, stride=dil), :].astype(o_ref.dtype)


def _proj_deint(xb, w, g, casts=(), *, tm, bsz, seq_len):
    t, d = xb.shape
    dil = B_DILATIONS[g]
    tn = B_OUT_W
    tiles_per_seq = seq_len // tm
    col0 = COL_QB // tn
    w_spec = lambda which: pl.BlockSpec((d, tn), lambda i: (0, col0 + B_GROUPS * which + g))
    grid = (t // tm,)
    cast_specs, cast_shapes = _cast_specs(casts, grid)
    out_spec = pl.BlockSpec((1, dil, tm // dil, 3 * tn), lambda i: (i // tiles_per_seq, 0, i % tiles_per_seq, 0))
    qkv, *cast_out = pl.pallas_call(
        _with_casts(functools.partial(_proj_deint_kernel, dil=dil), 4, 1, len(casts)),
        grid=grid,
        in_specs=[pl.BlockSpec((tm, d), lambda i: (i, 0)), w_spec(0), w_spec(1), w_spec(2)] + cast_specs,
        out_specs=[out_spec] + cast_specs,
        out_shape=[jax.ShapeDtypeStruct((bsz, dil, seq_len // dil, 3 * tn), BF16)] + cast_shapes,
        scratch_shapes=[pltpu.VMEM((tn // LANES, tm, LANES), F32)] * {1: 0, FAST_STRIDE ** 2: 6}.get(dil, 3),
        compiler_params=_params("parallel"),
    )(xb, w, w, w, *[w_f32 for w_f32, _ in casts])
    return qkv, cast_out


def _dilated_kernel(q_ref, k_ref, v_ref, o_ref, lse_ref, *token_order_scratch, dil, slopes, n_blocks, n_res):
    qb = 2 * B_SIDE
    kw = 4 * B_SIDE
    n_rows = k_ref.shape[2]
    row = lax.broadcasted_iota(jnp.int32, (qb, kw), 0)
    col = lax.broadcasted_iota(jnp.int32, (qb, kw), 1)

    def attend(r, blk, h, ks, valid, token_dist):
        sl = slice(h * HEAD_DIM, (h + 1) * HEAD_DIM)
        q = q_ref[0, r, blk * qb:(blk + 1) * qb, sl]
        k = k_ref[0, r, pl.ds(ks, kw), sl]
        v = v_ref[0, r, pl.ds(ks, kw), sl]
        s = lax.dot_general(q, k, NT_DIMS, preferred_element_type=F32) - slopes[h] * token_dist
        s = jnp.where(valid, s, -jnp.inf)
        m = jnp.max(s, axis=-1, keepdims=True)
        p = jnp.exp(s - m)
        l = jnp.sum(p, axis=-1, keepdims=True)
        o = jnp.dot(p.astype(BF16), v, preferred_element_type=F32) / l
        if dil == 1:
            o_ref[0, blk * qb:(blk + 1) * qb, sl] = o.astype(o_ref.dtype)
        else:
            o_sc, _ = token_order_scratch
            o_sc[h, pl.ds(blk * qb * dil + r, qb, stride=dil), :] = o
        return m + jnp.log(l)

    lane = lax.broadcasted_iota(jnp.int32, (qb, LANES), 1)

    def residues(it, carry):
        for blk in range(n_blocks):
            q0 = (pl.program_id(1) * n_blocks + blk) * qb
            ks = pl.multiple_of(jnp.clip(q0 - B_SIDE, 0, n_rows - kw), B_SIDE)
            dist = jnp.abs(col - row + (ks - q0))
            valid = dist <= B_SIDE
            token_dist = dist.astype(F32) * float(dil)
            for u in range(n_res):
                r = it * n_res + u
                lse = None
                for h in reversed(range(B_HEADS_PER_GROUP)):
                    col_h = jnp.broadcast_to(attend(r, blk, h, ks, valid, token_dist), (qb, LANES))
                    lse = col_h if lse is None else jnp.where(lane < (h + 1) * LSE_LANES, col_h, lse)
                if dil == 1:
                    lse_ref[0, blk * qb:(blk + 1) * qb, :] = lse
                else:
                    _, l_sc = token_order_scratch
                    l_sc[pl.ds(blk * qb * dil + r, qb, stride=dil), :] = lse
        return carry

    if dil == 1:
        residues(0, 0)
    else:
        o_sc, l_sc = token_order_scratch
        lax.fori_loop(0, dil // n_res, residues, 0)
        for h in range(B_HEADS_PER_GROUP):
            sl = slice(h * HEAD_DIM, (h + 1) * HEAD_DIM)
            o_ref[0, :, sl] = o_sc[h].astype(o_ref.dtype)
        lse_ref[0] = l_sc[...]


DILATED_BATCHING = ((8, 1), (4, 1), (1, 4))


def _dilated_group(qkv, g):
    b, dil, n_rows, _ = qkv.shape
    s = n_rows * dil
    n_blocks, n_res = DILATED_BATCHING[g]
    rows = n_blocks * 2 * B_SIDE
    assert n_rows >= 4 * B_SIDE and n_rows % rows == 0 and dil % n_res == 0
    slopes = tuple(2.0 ** (-8.0 * (g * B_HEADS_PER_GROUP + h + 1) / B_HEADS) for h in range(B_HEADS_PER_GROUP))
    kv_spec = lambda which: pl.BlockSpec((1, dil, n_rows, B_OUT_W), lambda bi, i: (bi, 0, 0, which))
    out_spec = lambda w: pl.BlockSpec((1, rows * dil, w), lambda bi, i: (bi, i, 0))
    staging = [pltpu.VMEM((B_HEADS_PER_GROUP, rows * dil, LANES), F32), pltpu.VMEM((rows * dil, LANES), F32)]
    o, lse = pl.pallas_call(
        functools.partial(_dilated_kernel, dil=dil, slopes=slopes, n_blocks=n_blocks, n_res=n_res),
        grid=(b, n_rows // rows),
        in_specs=[pl.BlockSpec((1, dil, rows, B_OUT_W), lambda bi, i: (bi, 0, i, 0)), kv_spec(1), kv_spec(2)],
        out_specs=[out_spec(B_OUT_W), out_spec(LANES)],
        out_shape=[jax.ShapeDtypeStruct((b, s, B_OUT_W), BF16), jax.ShapeDtypeStruct((b, s, LANES), F32)],
        scratch_shapes=staging if dil > 1 else [],
        compiler_params=_params("parallel", "arbitrary"),
    )(qkv, qkv, qkv)
    return o.reshape(b * s, B_OUT_W), lse.reshape(b * s, LANES)


def _mix_out_kernel(oa_ref, o0_ref, o1_ref, o2_ref, l0_ref, l1_ref, l2_ref, ga_ref, gb_ref, x_ref,
                    wpa_ref, wpb_ref, wo_ref, g_ref, b_ref, out_ref):
    l0, l1, l2 = l0_ref[...], l1_ref[...], l2_ref[...]
    mx = jnp.maximum(jnp.maximum(l0, l1), l2)
    e0, e1, e2 = jnp.exp(l0 - mx), jnp.exp(l1 - mx), jnp.exp(l2 - mx)
    inv = 1.0 / (e0 + e1 + e2)
    weights = (e0 * inv, e1 * inv, e2 * inv)
    rows = l0.shape[0]
    merged = []
    for h in range(B_HEADS_PER_GROUP):
        sl = slice(h * HEAD_DIM, (h + 1) * HEAD_DIM)
        acc = None
        for w, o_ref in zip(weights, (o0_ref, o1_ref, o2_ref)):
            term = jnp.broadcast_to(w[:, h * LSE_LANES:h * LSE_LANES + 1], (rows, HEAD_DIM)) * o_ref[:, sl]
            acc = term if acc is None else acc + term
        merged.append(acc.astype(BF16))
    ob = jnp.concatenate(merged, axis=1)
    pa = jnp.dot(oa_ref[...], wpa_ref[...], preferred_element_type=F32)
    pb = jnp.dot(ob, wpb_ref[...], preferred_element_type=F32)
    mixed = (ga_ref[...].astype(F32) * pa + gb_ref[...].astype(F32) * pb).astype(BF16)
    y = jnp.dot(mixed, wo_ref[...], preferred_element_type=F32)
    out_ref[...] = _layer_norm(ALPHA * x_ref[...] + y, g_ref[...], b_ref[...])


def _mix_out(out_a, o_groups, lse_groups, gates, x2d, wpa, wpb, wo, g, b, *, tm):
    t = out_a.shape[0]
    row = lambda w, cb=0: pl.BlockSpec((tm, w), lambda i: (i, cb))
    const = lambda a: pl.BlockSpec(a.shape, lambda i: (0, 0), pipeline_mode=pl.Buffered(1))
    return pl.pallas_call(
        _mix_out_kernel,
        grid=(t // tm,),
        in_specs=[row(A_Q_W)] + [row(B_OUT_W)] * 3 + [row(LANES)] * 3 + [row(D_MODEL, 0), row(D_MODEL, 1), row(D_MODEL),
                  const(wpa), const(wpb), const(wo), const(g), const(b)],
        out_specs=row(D_MODEL),
        out_shape=jax.ShapeDtypeStruct((t, D_MODEL), F32),
        compiler_params=_params("parallel"),
    )(out_a, *o_groups, *lse_groups, gates, gates, x2d, wpa, wpb, wo, g, b)


def _ffn_kernel(x_ref, wg_ref, wu_ref, wd_ref, g_ref, b_ref, o_ref, xb_sc, acc_sc):
    f = pl.program_id(1)

    @pl.when(f == 0)
    def _():
        xb_sc[...] = x_ref[...].astype(BF16)
        acc_sc[...] = jnp.zeros(acc_sc.shape, F32)

    xb = xb_sc[...]
    gate = jnp.dot(xb, wg_ref[...], preferred_element_type=F32)
    up = jnp.dot(xb, wu_ref[...], preferred_element_type=F32)
    h = gate * (0.5 * jnp.tanh(0.5 * gate) + 0.5) * up
    acc_sc[...] += jnp.dot(h.astype(BF16), wd_ref[...], preferred_element_type=F32)

    @pl.when(f == pl.num_programs(1) - 1)
    def _():
        o_ref[...] = _layer_norm(ALPHA * x_ref[...] + acc_sc[...], g_ref[...], b_ref[...])


def _ffn(x1, wg, wu, wd, g, b, *, tm, tf):
    t, d = x1.shape
    hidden = wg.shape[1]
    row = pl.BlockSpec((tm, d), lambda i, f: (i, 0))
    vec = pl.BlockSpec((1, d), lambda i, f: (0, 0))
    return pl.pallas_call(
        _ffn_kernel,
        grid=(t // tm, hidden // tf),
        in_specs=[row,
                  pl.BlockSpec((d, tf), lambda i, f: (0, f)),
                  pl.BlockSpec((d, tf), lambda i, f: (0, f)),
                  pl.BlockSpec((tf, d), lambda i, f: (f, 0)),
                  vec, vec],
        out_specs=row,
        out_shape=jax.ShapeDtypeStruct((t, d), F32),
        scratch_shapes=[pltpu.VMEM((tm, d), BF16), pltpu.VMEM((tm, d), F32)],
        compiler_params=_params("parallel", "arbitrary"),
    )(x1, wg, wu, wd, g, b)


def _rope_tables(seq_len):
    pos = np.arange(seq_len)
    n_freq = ROPE_AXIS_DIM // 2
    freqs = (np.float32(ROPE_THETA) ** (-np.arange(n_freq, dtype=np.float32) / np.float32(n_freq))).astype(np.float32)
    ang_r = (pos // GRID_W).astype(np.float32)[:, None] * freqs[None, :]
    ang_c = (pos % GRID_W).astype(np.float32)[:, None] * freqs[None, :]
    cos = np.concatenate([np.cos(ang_r)] * 2 + [np.cos(ang_c)] * 2, axis=-1).astype(np.float32)
    sin = np.concatenate([-np.sin(ang_r), np.sin(ang_r), -np.sin(ang_c), np.sin(ang_c)], axis=-1).astype(np.float32)
    return jnp.asarray(cos), jnp.asarray(sin)


def _layer(x, w_in, b_gate, q_norm_a, k_norm_a, w_proj_a, w_proj_b, w_out,
           ln1_g, ln1_b, w_ffn_gate, w_ffn_up, w_ffn_down, ln2_g, ln2_b, rope):
    bsz, seq_len, d = x.shape
    t = bsz * seq_len
    x2d = x.reshape(t, d)
    vec = lambda a: a.reshape(1, -1).astype(F32)

    gates, xb, w_in_b = _proj_gate(x2d, w_in[:, COL_GA:].astype(BF16), vec(b_gate), [(w_in, COL_GA)], tm=1024, tn=1024)

    qk_gain = jnp.concatenate([jnp.tile(q_norm_a * (Q_SCALE * LOG2_E), A_Q_HEADS), jnp.tile(k_norm_a, A_KV_HEADS)])
    qa, kv = _proj_a(xb, w_in_b, vec(qk_gain), rope, tm=1024, seq_len=seq_len)

    out_a, w_gate_b, w_up_b = _gqa(qa.reshape(bsz, seq_len, A_Q_W), kv.reshape(2 * A_KV_HEADS, bsz, seq_len, HEAD_DIM),
                                   [(w_ffn_gate, FFN_HIDDEN), (w_ffn_up, FFN_HIDDEN)], tq=256, tk=1024)
    casts = [[(w_ffn_down, D_MODEL)], [(w_proj_a, D_MODEL), (w_proj_b, D_MODEL), (w_out, D_MODEL)], []]
    groups, cast_out = [], []
    for g in range(B_GROUPS):
        qkv, converted = _proj_deint(xb, w_in_b, g, casts[g], tm=1024, bsz=bsz, seq_len=seq_len)
        groups.append(_dilated_group(qkv, g))
        cast_out += converted
    w_down_b, w_proj_a_b, w_proj_b_b, w_out_b = cast_out
    x1 = _mix_out(out_a.reshape(t, A_Q_W), [o for o, _ in groups], [l for _, l in groups], gates, x2d,
                  w_proj_a_b, w_proj_b_b, w_out_b, vec(ln1_g), vec(ln1_b), tm=512)
    x2 = _ffn(x1, w_gate_b, w_up_b, w_down_b, vec(ln2_g), vec(ln2_b), tm=512, tf=512)
    return x2.reshape(bsz, seq_len, d)


def kernel(x, w_in, b_gate, q_norm_a, k_norm_a, w_proj_a, w_proj_b, w_out, ln1_g, ln1_b, w_ffn_gate, w_ffn_up, w_ffn_down, ln2_g, ln2_b):
    rope = _rope_tables(x.shape[1])
    for l in range(w_in.shape[0]):
        x = _layer(x, w_in[l], b_gate[l], q_norm_a[l], k_norm_a[l], w_proj_a[l], w_proj_b[l], w_out[l],
                   ln1_g[l], ln1_b[l], w_ffn_gate[l], w_ffn_up[l], w_ffn_down[l], ln2_g[l], ln2_b[l], rope)
    return x
```

```python
import functools

import jax
import jax.numpy as jnp
import numpy as np
from jax import lax
from jax.experimental import pallas as pl
from jax.experimental.pallas import tpu as pltpu

F32 = jnp.float32
BF16 = jnp.bfloat16

D_MODEL = 2048
HEAD_DIM = 128
A_Q_HEADS = 8
A_KV_HEADS = 2
A_REP = A_Q_HEADS // A_KV_HEADS
ROPE_THETA = 10000.0
ROPE_AXIS_DIM = HEAD_DIM // 2
GRID_W = 64
B_DILATIONS = (1, 4, 16)
B_GROUPS = 3
B_HEADS_PER_GROUP = 4
B_HEADS = B_GROUPS * B_HEADS_PER_GROUP
B_SIDE = 64
A_Q_W = A_Q_HEADS * HEAD_DIM
A_KV_W = A_KV_HEADS * HEAD_DIM
B_W = B_HEADS * HEAD_DIM
B_OUT_W = B_HEADS_PER_GROUP * HEAD_DIM
LSE_LANES = HEAD_DIM // B_HEADS_PER_GROUP
FFN_HIDDEN = 5632
ALPHA = 2.0 ** 0.25
RMS_EPS = 1e-6
LN_EPS = 1e-5
Q_SCALE = HEAD_DIM ** -0.5

COL_QA = 0
COL_KA = A_Q_W
COL_VA = A_Q_W + A_KV_W
COL_QB = COL_VA + A_KV_W
COL_GA = COL_QB + 3 * B_W

VMEM_LIMIT = 60 * 1024 * 1024

NT_DIMS = (((1,), (1,)), ((), ()))
TN_DIMS = (((0,), (0,)), ((), ()))
LANES = 128
BF16_ROWS = 16
FAST_STRIDE = 4
LOG2_E = 1.4426950408889634


def _params(*sem):
    return pltpu.CompilerParams(dimension_semantics=sem, vmem_limit_bytes=VMEM_LIMIT)


def _with_casts(kernel, n_in, n_out, n_casts):
    def wrapped(*refs):
        ins, rest = refs[:n_in], refs[n_in:]
        cast_in, rest = rest[:n_casts], rest[n_casts:]
        outs, rest = rest[:n_out], rest[n_out:]
        cast_out, scratch = rest[:n_casts], rest[n_casts:]
        kernel(*ins, *outs, *scratch)
        for src, dst in zip(cast_in, cast_out):
            dst[...] = src[...].astype(dst.dtype)
    return wrapped


def _cast_specs(weights, grid):
    n_steps = 1
    for n in grid:
        n_steps *= n

    def step(*idx):
        s = 0
        for i, n in zip(idx, grid):
            s = s * n + i
        return s

    specs, shapes = [], []
    for w, n_cols in weights:
        rows = w.shape[0] // n_steps
        assert rows * n_steps == w.shape[0] and rows % BF16_ROWS == 0 and n_cols % LANES == 0
        specs.append(pl.BlockSpec((rows, n_cols), lambda *idx: (step(*idx), 0)))
        shapes.append(jax.ShapeDtypeStruct((w.shape[0], n_cols), BF16))
    return specs, shapes


def _layer_norm(z, g, b):
    mu = jnp.mean(z, axis=-1, keepdims=True)
    zc = z - mu
    var = jnp.mean(zc * zc, axis=-1, keepdims=True)
    return zc * lax.rsqrt(var + LN_EPS) * g + b


def _norm_rope(yh, gain, cos, sin):
    mean_w = jnp.full((2 * HEAD_DIM, HEAD_DIM), 1.0 / HEAD_DIM, BF16)
    sq = yh * yh
    hi = sq.astype(BF16)
    lo = (sq - hi.astype(F32)).astype(BF16)
    ms = jnp.dot(jnp.concatenate([hi, lo], axis=1), mean_w, preferred_element_type=F32)
    yn = yh * lax.rsqrt(ms + RMS_EPS) * gain
    half = ROPE_AXIS_DIM // 2
    lane = lax.broadcasted_iota(jnp.int32, yn.shape, 1)
    partner = jnp.where(lane % ROPE_AXIS_DIM < half, pltpu.roll(yn, HEAD_DIM - half, 1), pltpu.roll(yn, half, 1))
    return yn * cos + partner * sin


def _proj_a_kernel(x_ref, w_ref, gain_ref, cos_ref, sin_ref, q_ref, kv_ref):
    y = jnp.dot(x_ref[...], w_ref[...], preferred_element_type=F32)
    cos = cos_ref[...]
    sin = sin_ref[...]
    head = lambda c: slice(c * HEAD_DIM, (c + 1) * HEAD_DIM)
    for h in range(A_Q_HEADS):
        q_ref[:, head(h)] = _norm_rope(y[:, head(h)], gain_ref[:, head(h)], cos, sin).astype(q_ref.dtype)
    for h in range(A_KV_HEADS):
        c = A_Q_HEADS + h
        kv_ref[h] = _norm_rope(y[:, head(c)], gain_ref[:, head(c)], cos, sin).astype(kv_ref.dtype)
    for h in range(A_KV_HEADS):
        kv_ref[A_KV_HEADS + h] = y[:, head(A_Q_HEADS + A_KV_HEADS + h)].astype(kv_ref.dtype)


def _proj_a(xb, w, gain, rope, *, tm, seq_len):
    t, d = xb.shape
    n_cols = A_Q_W + 2 * A_KV_W
    pos_blocks = seq_len // tm
    tab = pl.BlockSpec((tm, HEAD_DIM), lambda i: (i % pos_blocks, 0))
    return pl.pallas_call(
        _proj_a_kernel,
        grid=(t // tm,),
        in_specs=[pl.BlockSpec((tm, d), lambda i: (i, 0)),
                  pl.BlockSpec((d, n_cols), lambda i: (0, COL_QA // n_cols)),
                  pl.BlockSpec((1, A_Q_W + A_KV_W), lambda i: (0, 0)), tab, tab],
        out_specs=[pl.BlockSpec((tm, A_Q_W), lambda i: (i, 0)),
                   pl.BlockSpec((2 * A_KV_HEADS, tm, HEAD_DIM), lambda i: (0, i, 0))],
        out_shape=[jax.ShapeDtypeStruct((t, A_Q_W), BF16), jax.ShapeDtypeStruct((2 * A_KV_HEADS, t, HEAD_DIM), BF16)],
        compiler_params=_params("parallel"),
    )(xb, w, gain, *rope)


def _proj_gate_kernel(x_ref, w_ref, b_ref, o_ref, xb_ref):
    @pl.when(pl.program_id(1) == 0)
    def _():
        xb_ref[...] = x_ref[...].astype(xb_ref.dtype)

    y = jnp.dot(xb_ref[...], w_ref[...], preferred_element_type=F32)
    o_ref[...] = (0.5 * jnp.tanh(0.5 * (y + b_ref[...])) + 0.5).astype(o_ref.dtype)


def _proj_gate(x2d, w, bias, casts, *, tm, tn):
    t, d = x2d.shape
    n_cols = w.shape[1]
    grid = (t // tm, n_cols // tn)
    cast_specs, cast_shapes = _cast_specs(casts, grid)
    return pl.pallas_call(
        _with_casts(_proj_gate_kernel, 3, 2, len(casts)),
        grid=grid,
        in_specs=[pl.BlockSpec((tm, d), lambda i, j: (i, 0)),
                  pl.BlockSpec((d, tn), lambda i, j: (0, j)),
                  pl.BlockSpec((1, tn), lambda i, j: (0, j))] + cast_specs,
        out_specs=[pl.BlockSpec((tm, tn), lambda i, j: (i, j)), pl.BlockSpec((tm, d), lambda i, j: (i, 0))] + cast_specs,
        out_shape=[jax.ShapeDtypeStruct((t, n_cols), BF16), jax.ShapeDtypeStruct((t, d), BF16)] + cast_shapes,
        compiler_params=_params("parallel", "arbitrary"),
    )(x2d, w, bias, *[w_f32 for w_f32, _ in casts])


def _gqa_kernel(q_ref, k_ref, v_ref, o_ref, m_sc, l_sc, acc_sc, s0_sc, s1_sc, *, tk):
    tq = q_ref.shape[1]
    n_chunks = k_ref.shape[1] // tk
    s_sc = (s0_sc, s1_sc)
    q = jnp.concatenate([q_ref[0, :, r * HEAD_DIM:(r + 1) * HEAD_DIM] for r in range(A_REP)], axis=0)
    m_sc[...] = jnp.full(m_sc.shape, -jnp.inf, F32)
    l_sc[...] = jnp.zeros(l_sc.shape, F32)
    acc_sc[...] = jnp.zeros(acc_sc.shape, F32)

    def scores(c, slot):
        s_sc[slot][...] = lax.dot_general(k_ref[0, c * tk:(c + 1) * tk, :], q, NT_DIMS, preferred_element_type=F32)

    def softmax_pv(c, slot):
        v = v_ref[0, c * tk:(c + 1) * tk, :]
        st = s_sc[slot][...]
        m_prev = m_sc[...]
        m_new = jnp.maximum(m_prev, jnp.max(st, axis=0, keepdims=True))
        alpha = jnp.exp2(m_prev - m_new)
        pt = jnp.exp2(st - m_new)
        l_sc[...] = alpha * l_sc[...] + jnp.sum(pt, axis=0, keepdims=True)
        pv = lax.dot_general(v, pt.astype(BF16), TN_DIMS, preferred_element_type=F32)
        acc_sc[...] = alpha * acc_sc[...] + pv
        m_sc[...] = m_new

    scores(0, 0)
    for c in range(n_chunks - 1):
        scores(c + 1, (c + 1) % 2)
        softmax_pv(c, c % 2)
    softmax_pv(n_chunks - 1, (n_chunks - 1) % 2)
    ot = acc_sc[...] / l_sc[...]
    for r in range(A_REP):
        o_ref[0, :, r * HEAD_DIM:(r + 1) * HEAD_DIM] = ot[:, r * tq:(r + 1) * tq].T.astype(o_ref.dtype)


def _gqa(qa, kv, casts, *, tq, tk):
    b, s, _ = qa.shape
    assert s % tk == 0 and s % tq == 0
    gw = A_REP * HEAD_DIM
    rows = A_REP * tq
    grid = (b, A_KV_HEADS, s // tq)
    cast_specs, cast_shapes = _cast_specs(casts, grid)
    return pl.pallas_call(
        _with_casts(functools.partial(_gqa_kernel, tk=tk), 3, 1, len(casts)),
        grid=grid,
        in_specs=[pl.BlockSpec((1, tq, gw), lambda bi, g, i: (bi, i, g)),
                  pl.BlockSpec((None, 1, s, HEAD_DIM), lambda bi, g, i: (g, bi, 0, 0)),
                  pl.BlockSpec((None, 1, s, HEAD_DIM), lambda bi, g, i: (A_KV_HEADS + g, bi, 0, 0))] + cast_specs,
        out_specs=[pl.BlockSpec((1, tq, gw), lambda bi, g, i: (bi, i, g))] + cast_specs,
        out_shape=[jax.ShapeDtypeStruct((b, s, A_Q_W), BF16)] + cast_shapes,
        scratch_shapes=[pltpu.VMEM((1, rows), F32), pltpu.VMEM((1, rows), F32), pltpu.VMEM((HEAD_DIM, rows), F32),
                        pltpu.VMEM((tk, rows), F32), pltpu.VMEM((tk, rows), F32)],
        compiler_params=_params("parallel", "parallel", "arbitrary"),
    )(qa, kv, kv, *[w_f32 for w_f32, _ in casts])


def _proj_deint_kernel(x_ref, wq_ref, wk_ref, wv_ref, o_ref, *stage, dil):
    x = x_ref[...]
    tm = x.shape[0]
    for which, w_ref in enumerate((wq_ref, wk_ref, wv_ref)):
        y = jnp.dot(x, w_ref[...], preferred_element_type=F32)
        if which == 0:
            y = y * Q_SCALE
        tn = y.shape[1]
        if dil == 1:
            o_ref[0, 0, :, which * tn:(which + 1) * tn] = y.astype(o_ref.dtype)
            continue
        y_sc = stage[which]
        n_slabs = tn // LANES
        for s in range(n_slabs):
            y_sc[s] = y[:, s * LANES:(s + 1) * LANES]
        if dil == FAST_STRIDE ** 2:
            z_sc = stage[3 + which]
            part = tm // FAST_STRIDE
            for lo in range(FAST_STRIDE):
                for s in range(n_slabs):
                    z_sc[s, lo * part:(lo + 1) * part, :] = y_sc[s, pl.ds(lo, part, stride=FAST_STRIDE), :]
            for lo in range(FAST_STRIDE):
                for hi in range(FAST_STRIDE):
                    for s in range(n_slabs):
                        c0 = which * tn + s * LANES
                        rows = z_sc[s, pl.ds(lo * part + hi, tm // dil, stride=FAST_STRIDE), :]
                        o_ref[0, lo + FAST_STRIDE * hi, :, c0:c0 + LANES] = rows.astype(o_ref.dtype)
            continue
        for r in range(dil):
            for s in range(n_slabs):
                c0 = which * tn + s * LANES
                o_ref[0, r, :, c0:c0 + LANES] = y_sc[s, pl.ds(r, tm // dil, stride=dil), :].astype(o_ref.dtype)


def _proj_deint(xb, w, g, casts=(), *, tm, bsz, seq_len):
    t, d = xb.shape
    dil = B_DILATIONS[g]
    tn = B_OUT_W
    tiles_per_seq = seq_len // tm
    col0 = COL_QB // tn
    w_spec = lambda which: pl.BlockSpec((d, tn), lambda i: (0, col0 + B_GROUPS * which + g))
    grid = (t // tm,)
    cast_specs, cast_shapes = _cast_specs(casts, grid)
    out_spec = pl.BlockSpec((1, dil, tm // dil, 3 * tn), lambda i: (i // tiles_per_seq, 0, i % tiles_per_seq, 0))
    qkv, *cast_out = pl.pallas_call(
        _with_casts(functools.partial(_proj_deint_kernel, dil=dil), 4, 1, len(casts)),
        grid=grid,
        in_specs=[pl.BlockSpec((tm, d), lambda i: (i, 0)), w_spec(0), w_spec(1), w_spec(2)] + cast_specs,
        out_specs=[out_spec] + cast_specs,
        out_shape=[jax.ShapeDtypeStruct((bsz, dil, seq_len // dil, 3 * tn), BF16)] + cast_shapes,
        scratch_shapes=[pltpu.VMEM((tn // LANES, tm, LANES), F32)] * {1: 0, FAST_STRIDE ** 2: 6}.get(dil, 3),
        compiler_params=_params("parallel"),
    )(xb, w, w, w, *[w_f32 for w_f32, _ in casts])
    return qkv, cast_out


def _dilated_kernel(q_ref, k_ref, v_ref, o_ref, lse_ref, *token_order_scratch, dil, slopes, n_blocks, n_res):
    qb = 2 * B_SIDE
    kw = 4 * B_SIDE
    n_rows = k_ref.shape[2]
    row = lax.broadcasted_iota(jnp.int32, (qb, kw), 0)
    col = lax.broadcasted_iota(jnp.int32, (qb, kw), 1)

    def attend(r, blk, h, ks, valid, token_dist):
        sl = slice(h * HEAD_DIM, (h + 1) * HEAD_DIM)
        q = q_ref[0, r, blk * qb:(blk + 1) * qb, sl]
        k = k_ref[0, r, pl.ds(ks, kw), sl]
        v = v_ref[0, r, pl.ds(ks, kw), sl]
        s = lax.dot_general(q, k, NT_DIMS, preferred_element_type=F32) - slopes[h] * token_dist
        s = jnp.where(valid, s, -jnp.inf)
        m = jnp.max(s, axis=-1, keepdims=True)
        p = jnp.exp(s - m)
        l = jnp.sum(p, axis=-1, keepdims=True)
        o = jnp.dot(p.astype(BF16), v, preferred_element_type=F32) / l
        if dil == 1:
            o_ref[0, blk * qb:(blk + 1) * qb, sl] = o.astype(o_ref.dtype)
        else:
            o_sc, _ = token_order_scratch
            o_sc[h, pl.ds(blk * qb * dil + r, qb, stride=dil), :] = o
        return m + jnp.log(l)

    lane = lax.broadcasted_iota(jnp.int32, (qb, LANES), 1)

    def residues(it, carry):
        for blk in range(n_blocks):
            q0 = (pl.program_id(1) * n_blocks + blk) * qb
            ks = pl.multiple_of(jnp.clip(q0 - B_SIDE, 0, n_rows - kw), B_SIDE)
            dist = jnp.abs(col - row + (ks - q0))
            valid = dist <= B_SIDE
            token_dist = dist.astype(F32) * float(dil)
            for u in range(n_res):
                r = it * n_res + u
                lse = None
                for h in reversed(range(B_HEADS_PER_GROUP)):
                    col_h = jnp.broadcast_to(attend(r, blk, h, ks, valid, token_dist), (qb, LANES))
                    lse = col_h if lse is None else jnp.where(lane < (h + 1) * LSE_LANES, col_h, lse)
                if dil == 1:
                    lse_ref[0, blk * qb:(blk + 1) * qb, :] = lse
                else:
                    _, l_sc = token_order_scratch
                    l_sc[pl.ds(blk * qb * dil + r, qb, stride=dil), :] = lse
        return carry

    if dil == 1:
        residues(0, 0)
    else:
        o_sc, l_sc = token_order_scratch
        lax.fori_loop(0, dil // n_res, residues, 0)
        for h in range(B_HEADS_PER_GROUP):
            sl = slice(h * HEAD_DIM, (h + 1) * HEAD_DIM)
            o_ref[0, :, sl] = o_sc[h].astype(o_ref.dtype)
        lse_ref[0] = l_sc[...]


DILATED_BATCHING = ((8, 1), (8, 1), (1, 8))


def _dilated_group(qkv, g):
    b, dil, n_rows, _ = qkv.shape
    s = n_rows * dil
    n_blocks, n_res = DILATED_BATCHING[g]
    rows = n_blocks * 2 * B_SIDE
    assert n_rows >= 4 * B_SIDE and n_rows % rows == 0 and dil % n_res == 0
    slopes = tuple(2.0 ** (-8.0 * (g * B_HEADS_PER_GROUP + h + 1) / B_HEADS) for h in range(B_HEADS_PER_GROUP))
    kv_spec = lambda which: pl.BlockSpec((1, dil, n_rows, B_OUT_W), lambda bi, i: (bi, 0, 0, which))
    out_spec = lambda w: pl.BlockSpec((1, rows * dil, w), lambda bi, i: (bi, i, 0))
    staging = [pltpu.VMEM((B_HEADS_PER_GROUP, rows * dil, LANES), F32), pltpu.VMEM((rows * dil, LANES), F32)]
    o, lse = pl.pallas_call(
        functools.partial(_dilated_kernel, dil=dil, slopes=slopes, n_blocks=n_blocks, n_res=n_res),
        grid=(b, n_rows // rows),
        in_specs=[pl.BlockSpec((1, dil, rows, B_OUT_W), lambda bi, i: (bi, 0, i, 0)), kv_spec(1), kv_spec(2)],
        out_specs=[out_spec(B_OUT_W), out_spec(LANES)],
        out_shape=[jax.ShapeDtypeStruct((b, s, B_OUT_W), BF16), jax.ShapeDtypeStruct((b, s, LANES), F32)],
        scratch_shapes=staging if dil > 1 else [],
        compiler_params=_params("parallel", "arbitrary"),
    )(qkv, qkv, qkv)
    return o.reshape(b * s, B_OUT_W), lse.reshape(b * s, LANES)


def _mix_out_kernel(oa_ref, o0_ref, o1_ref, o2_ref, l0_ref, l1_ref, l2_ref, ga_ref, gb_ref, x_ref,
                    wpa_ref, wpb_ref, wo_ref, g_ref, b_ref, out_ref):
    l0, l1, l2 = l0_ref[...], l1_ref[...], l2_ref[...]
    mx = jnp.maximum(jnp.maximum(l0, l1), l2)
    e0, e1, e2 = jnp.exp(l0 - mx), jnp.exp(l1 - mx), jnp.exp(l2 - mx)
    inv = 1.0 / (e0 + e1 + e2)
    weights = (e0 * inv, e1 * inv, e2 * inv)
    rows = l0.shape[0]
    merged = []
    for h in range(B_HEADS_PER_GROUP):
        sl = slice(h * HEAD_DIM, (h + 1) * HEAD_DIM)
        acc = None
        for w, o_ref in zip(weights, (o0_ref, o1_ref, o2_ref)):
            term = jnp.broadcast_to(w[:, h * LSE_LANES:h * LSE_LANES + 1], (rows, HEAD_DIM)) * o_ref[:, sl]
            acc = term if acc is None else acc + term
        merged.append(acc.astype(BF16))
    ob = jnp.concatenate(merged, axis=1)
    pa = jnp.dot(oa_ref[...], wpa_ref[...], preferred_element_type=F32)
    pb = jnp.dot(ob, wpb_ref[...], preferred_element_type=F32)
    mixed = (ga_ref[...].astype(F32) * pa + gb_ref[...].astype(F32) * pb).astype(BF16)
    y = jnp.dot(mixed, wo_ref[...], preferred_element_type=F32)
    out_ref[...] = _layer_norm(ALPHA * x_ref[...] + y, g_ref[...], b_ref[...])


def _mix_out(out_a, o_groups, lse_groups, gates, x2d, wpa, wpb, wo, g, b, *, tm):
    t = out_a.shape[0]
    row = lambda w, cb=0: pl.BlockSpec((tm, w), lambda i: (i, cb))
    const = lambda a: pl.BlockSpec(a.shape, lambda i: (0, 0), pipeline_mode=pl.Buffered(1))
    return pl.pallas_call(
        _mix_out_kernel,
        grid=(t // tm,),
        in_specs=[row(A_Q_W)] + [row(B_OUT_W)] * 3 + [row(LANES)] * 3 + [row(D_MODEL, 0), row(D_MODEL, 1), row(D_MODEL),
                  const(wpa), const(wpb), const(wo), const(g), const(b)],
        out_specs=row(D_MODEL),
        out_shape=jax.ShapeDtypeStruct((t, D_MODEL), F32),
        compiler_params=_params("parallel"),
    )(out_a, *o_groups, *lse_groups, gates, gates, x2d, wpa, wpb, wo, g, b)


def _ffn_kernel(x_ref, wg_ref, wu_ref, wd_ref, g_ref, b_ref, o_ref, xb_sc, acc_sc):
    f = pl.program_id(1)

    @pl.when(f == 0)
    def _():
        xb_sc[...] = x_ref[...].astype(BF16)
        acc_sc[...] = jnp.zeros(acc_sc.shape, F32)

    xb = xb_sc[...]
    gate = jnp.dot(xb, wg_ref[...], preferred_element_type=F32)
    up = jnp.dot(xb, wu_ref[...], preferred_element_type=F32)
    h = gate * (0.5 * jnp.tanh(0.5 * gate) + 0.5) * up
    acc_sc[...] += jnp.dot(h.astype(BF16), wd_ref[...], preferred_element_type=F32)

    @pl.when(f == pl.num_programs(1) - 1)
    def _():
        o_ref[...] = _layer_norm(ALPHA * x_ref[...] + acc_sc[...], g_ref[...], b_ref[...])


def _ffn(x1, wg, wu, wd, g, b, *, tm, tf):
    t, d = x1.shape
    hidden = wg.shape[1]
    row = pl.BlockSpec((tm, d), lambda i, f: (i, 0))
    vec = pl.BlockSpec((1, d), lambda i, f: (0, 0))
    return pl.pallas_call(
        _ffn_kernel,
        grid=(t // tm, hidden // tf),
        in_specs=[row,
                  pl.BlockSpec((d, tf), lambda i, f: (0, f)),
                  pl.BlockSpec((d, tf), lambda i, f: (0, f)),
                  pl.BlockSpec((tf, d), lambda i, f: (f, 0)),
                  vec, vec],
        out_specs=row,
        out_shape=jax.ShapeDtypeStruct((t, d), F32),
        scratch_shapes=[pltpu.VMEM((tm, d), BF16), pltpu.VMEM((tm, d), F32)],
        compiler_params=_params("parallel", "arbitrary"),
    )(x1, wg, wu, wd, g, b)


def _rope_tables(seq_len):
    pos = np.arange(seq_len)
    n_freq = ROPE_AXIS_DIM // 2
    freqs = (np.float32(ROPE_THETA) ** (-np.arange(n_freq, dtype=np.float32) / np.float32(n_freq))).astype(np.float32)
    ang_r = (pos // GRID_W).astype(np.float32)[:, None] * freqs[None, :]
    ang_c = (pos % GRID_W).astype(np.float32)[:, None] * freqs[None, :]
    cos = np.concatenate([np.cos(ang_r)] * 2 + [np.cos(ang_c)] * 2, axis=-1).astype(np.float32)
    sin = np.concatenate([-np.sin(ang_r), np.sin(ang_r), -np.sin(ang_c), np.sin(ang_c)], axis=-1).astype(np.float32)
    return jnp.asarray(cos), jnp.asarray(sin)


def _layer(x, w_in, b_gate, q_norm_a, k_norm_a, w_proj_a, w_proj_b, w_out,
           ln1_g, ln1_b, w_ffn_gate, w_ffn_up, w_ffn_down, ln2_g, ln2_b, rope):
    bsz, seq_len, d = x.shape
    t = bsz * seq_len
    x2d = x.reshape(t, d)
    vec = lambda a: a.reshape(1, -1).astype(F32)

    gates, xb, w_in_b = _proj_gate(x2d, w_in[:, COL_GA:].astype(BF16), vec(b_gate), [(w_in, COL_GA)], tm=1024, tn=1024)

    qk_gain = jnp.concatenate([jnp.tile(q_norm_a * (Q_SCALE * LOG2_E), A_Q_HEADS), jnp.tile(k_norm_a, A_KV_HEADS)])
    qa, kv = _proj_a(xb, w_in_b, vec(qk_gain), rope, tm=1024, seq_len=seq_len)

    out_a, w_gate_b, w_up_b = _gqa(qa.reshape(bsz, seq_len, A_Q_W), kv.reshape(2 * A_KV_HEADS, bsz, seq_len, HEAD_DIM),
                                   [(w_ffn_gate, FFN_HIDDEN), (w_ffn_up, FFN_HIDDEN)], tq=256, tk=1024)
    casts = [[(w_ffn_down, D_MODEL)], [(w_proj_a, D_MODEL), (w_proj_b, D_MODEL), (w_out, D_MODEL)], []]
    groups, cast_out = [], []
    for g in range(B_GROUPS):
        qkv, converted = _proj_deint(xb, w_in_b, g, casts[g], tm=1024, bsz=bsz, seq_len=seq_len)
        groups.append(_dilated_group(qkv, g))
        cast_out += converted
    w_down_b, w_proj_a_b, w_proj_b_b, w_out_b = cast_out
    x1 = _mix_out(out_a.reshape(t, A_Q_W), [o for o, _ in groups], [l for _, l in groups], gates, x2d,
                  w_proj_a_b, w_proj_b_b, w_out_b, vec(ln1_g), vec(ln1_b), tm=512)
    x2 = _ffn(x1, w_gate_b, w_up_b, w_down_b, vec(ln2_g), vec(ln2_b), tm=512, tf=512)
    return x2.reshape(bsz, seq_len, d)


def kernel(x, w_in, b_gate, q_norm_a, k_norm_a, w_proj_a, w_proj_b, w_out, ln1_g, ln1_b, w_ffn_gate, w_ffn_up, w_ffn_down, ln2_g, ln2_b):
    rope = _rope_tables(x.shape[1])
    for l in range(w_in.shape[0]):
        x = _layer(x, w_in[l], b_gate[l], q_norm_a[l], k_norm_a[l], w_proj_a[l], w_proj_b[l], w_out[l],
                   ln1_g[l], ln1_b[l], w_ffn_gate[l], w_ffn_up[l], w_ffn_down[l], ln2_g[l], ln2_b[l], rope)
    return x
```

```python
import functools

import jax
import jax.numpy as jnp
import numpy as np
from jax import lax
from jax.experimental import pallas as pl
from jax.experimental.pallas import tpu as pltpu

F32 = jnp.float32
BF16 = jnp.bfloat16

D_MODEL = 2048
HEAD_DIM = 128
A_Q_HEADS = 8
A_KV_HEADS = 2
A_REP = A_Q_HEADS // A_KV_HEADS
ROPE_THETA = 10000.0
ROPE_AXIS_DIM = HEAD_DIM // 2
GRID_W = 64
B_DILATIONS = (1, 4, 16)
B_GROUPS = 3
B_HEADS_PER_GROUP = 4
B_HEADS = B_GROUPS * B_HEADS_PER_GROUP
B_SIDE = 64
A_Q_W = A_Q_HEADS * HEAD_DIM
A_KV_W = A_KV_HEADS * HEAD_DIM
B_W = B_HEADS * HEAD_DIM
B_OUT_W = B_HEADS_PER_GROUP * HEAD_DIM
LSE_LANES = HEAD_DIM // B_HEADS_PER_GROUP
FFN_HIDDEN = 5632
ALPHA = 2.0 ** 0.25
RMS_EPS = 1e-6
LN_EPS = 1e-5
Q_SCALE = HEAD_DIM ** -0.5

COL_QA = 0
COL_KA = A_Q_W
COL_VA = A_Q_W + A_KV_W
COL_QB = COL_VA + A_KV_W
COL_GA = COL_QB + 3 * B_W

VMEM_LIMIT = 60 * 1024 * 1024

NT_DIMS = (((1,), (1,)), ((), ()))
TN_DIMS = (((0,), (0,)), ((), ()))
LANES = 128
BF16_ROWS = 16
FAST_STRIDE = 4
LOG2_E = 1.4426950408889634


def _params(*sem):
    return pltpu.CompilerParams(dimension_semantics=sem, vmem_limit_bytes=VMEM_LIMIT)


def _with_casts(kernel, n_in, n_out, n_casts):
    def wrapped(*refs):
        ins, rest = refs[:n_in], refs[n_in:]
        cast_in, rest = rest[:n_casts], rest[n_casts:]
        outs, rest = rest[:n_out], rest[n_out:]
        cast_out, scratch = rest[:n_casts], rest[n_casts:]
        kernel(*ins, *outs, *scratch)
        for src, dst in zip(cast_in, cast_out):
            dst[...] = src[...].astype(dst.dtype)
    return wrapped


def _cast_specs(weights, grid):
    n_steps = 1
    for n in grid:
        n_steps *= n

    def step(*idx):
        s = 0
        for i, n in zip(idx, grid):
            s = s * n + i
        return s

    specs, shapes = [], []
    for w, n_cols in weights:
        rows = w.shape[0] // n_steps
        assert rows * n_steps == w.shape[0] and rows % BF16_ROWS == 0 and n_cols % LANES == 0
        specs.append(pl.BlockSpec((rows, n_cols), lambda *idx: (step(*idx), 0)))
        shapes.append(jax.ShapeDtypeStruct((w.shape[0], n_cols), BF16))
    return specs, shapes


def _layer_norm(z, g, b):
    mu = jnp.mean(z, axis=-1, keepdims=True)
    zc = z - mu
    var = jnp.mean(zc * zc, axis=-1, keepdims=True)
    return zc * lax.rsqrt(var + LN_EPS) * g + b


def _norm_rope(yh, gain, cos, sin):
    mean_w = jnp.full((2 * HEAD_DIM, HEAD_DIM), 1.0 / HEAD_DIM, BF16)
    sq = yh * yh
    hi = sq.astype(BF16)
    lo = (sq - hi.astype(F32)).astype(BF16)
    ms = jnp.dot(jnp.concatenate([hi, lo], axis=1), mean_w, preferred_element_type=F32)
    yn = yh * lax.rsqrt(ms + RMS_EPS) * gain
    half = ROPE_AXIS_DIM // 2
    lane = lax.broadcasted_iota(jnp.int32, yn.shape, 1)
    partner = jnp.where(lane % ROPE_AXIS_DIM < half, pltpu.roll(yn, HEAD_DIM - half, 1), pltpu.roll(yn, half, 1))
    return yn * cos + partner * sin


def _proj_a_kernel(x_ref, w_ref, gain_ref, cos_ref, sin_ref, q_ref, kv_ref):
    y = jnp.dot(x_ref[...], w_ref[...], preferred_element_type=F32)
    cos = cos_ref[...]
    sin = sin_ref[...]
    head = lambda c: slice(c * HEAD_DIM, (c + 1) * HEAD_DIM)
    for h in range(A_Q_HEADS):
        q_ref[:, head(h)] = _norm_rope(y[:, head(h)], gain_ref[:, head(h)], cos, sin).astype(q_ref.dtype)
    for h in range(A_KV_HEADS):
        c = A_Q_HEADS + h
        kv_ref[h] = _norm_rope(y[:, head(c)], gain_ref[:, head(c)], cos, sin).astype(kv_ref.dtype)
    for h in range(A_KV_HEADS):
        kv_ref[A_KV_HEADS + h] = y[:, head(A_Q_HEADS + A_KV_HEADS + h)].astype(kv_ref.dtype)


def _proj_a(xb, w, gain, rope, *, tm, seq_len):
    t, d = xb.shape
    n_cols = A_Q_W + 2 * A_KV_W
    pos_blocks = seq_len // tm
    tab = pl.BlockSpec((tm, HEAD_DIM), lambda i: (i % pos_blocks, 0))
    return pl.pallas_call(
        _proj_a_kernel,
        grid=(t // tm,),
        in_specs=[pl.BlockSpec((tm, d), lambda i: (i, 0)),
                  pl.BlockSpec((d, n_cols), lambda i: (0, COL_QA // n_cols)),
                  pl.BlockSpec((1, A_Q_W + A_KV_W), lambda i: (0, 0)), tab, tab],
        out_specs=[pl.BlockSpec((tm, A_Q_W), lambda i: (i, 0)),
                   pl.BlockSpec((2 * A_KV_HEADS, tm, HEAD_DIM), lambda i: (0, i, 0))],
        out_shape=[jax.ShapeDtypeStruct((t, A_Q_W), BF16), jax.ShapeDtypeStruct((2 * A_KV_HEADS, t, HEAD_DIM), BF16)],
        compiler_params=_params("parallel"),
    )(xb, w, gain, *rope)


def _proj_gate_kernel(x_ref, w_ref, b_ref, o_ref, xb_ref):
    @pl.when(pl.program_id(1) == 0)
    def _():
        xb_ref[...] = x_ref[...].astype(xb_ref.dtype)

    y = jnp.dot(xb_ref[...], w_ref[...], preferred_element_type=F32)
    o_ref[...] = (0.5 * jnp.tanh(0.5 * (y + b_ref[...])) + 0.5).astype(o_ref.dtype)


def _proj_gate(x2d, w, bias, casts, *, tm, tn):
    t, d = x2d.shape
    n_cols = w.shape[1]
    grid = (t // tm, n_cols // tn)
    cast_specs, cast_shapes = _cast_specs(casts, grid)
    return pl.pallas_call(
        _with_casts(_proj_gate_kernel, 3, 2, len(casts)),
        grid=grid,
        in_specs=[pl.BlockSpec((tm, d), lambda i, j: (i, 0)),
                  pl.BlockSpec((d, tn), lambda i, j: (0, j)),
                  pl.BlockSpec((1, tn), lambda i, j: (0, j))] + cast_specs,
        out_specs=[pl.BlockSpec((tm, tn), lambda i, j: (i, j)), pl.BlockSpec((tm, d), lambda i, j: (i, 0))] + cast_specs,
        out_shape=[jax.ShapeDtypeStruct((t, n_cols), BF16), jax.ShapeDtypeStruct((t, d), BF16)] + cast_shapes,
        compiler_params=_params("parallel", "arbitrary"),
    )(x2d, w, bias, *[w_f32 for w_f32, _ in casts])


def _gqa_kernel(q_ref, k_ref, v_ref, o_ref, m_sc, l_sc, acc_sc, s0_sc, s1_sc, *, tk):
    tq = q_ref.shape[1]
    n_chunks = k_ref.shape[1] // tk
    s_sc = (s0_sc, s1_sc)
    q = jnp.concatenate([q_ref[0, :, r * HEAD_DIM:(r + 1) * HEAD_DIM] for r in range(A_REP)], axis=0)
    m_sc[...] = jnp.full(m_sc.shape, -jnp.inf, F32)
    l_sc[...] = jnp.zeros(l_sc.shape, F32)
    acc_sc[...] = jnp.zeros(acc_sc.shape, F32)

    def scores(c, slot):
        s_sc[slot][...] = lax.dot_general(k_ref[0, c * tk:(c + 1) * tk, :], q, NT_DIMS, preferred_element_type=F32)

    def softmax_pv(c, slot):
        v = v_ref[0, c * tk:(c + 1) * tk, :]
        st = s_sc[slot][...]
        m_prev = m_sc[...]
        m_new = jnp.maximum(m_prev, jnp.max(st, axis=0, keepdims=True))
        alpha = jnp.exp2(m_prev - m_new)
        pt = jnp.exp2(st - m_new)
        l_sc[...] = alpha * l_sc[...] + jnp.sum(pt, axis=0, keepdims=True)
        pv = lax.dot_general(v, pt.astype(BF16), TN_DIMS, preferred_element_type=F32)
        acc_sc[...] = alpha * acc_sc[...] + pv
        m_sc[...] = m_new

    scores(0, 0)
    for c in range(n_chunks - 1):
        scores(c + 1, (c + 1) % 2)
        softmax_pv(c, c % 2)
    softmax_pv(n_chunks - 1, (n_chunks - 1) % 2)
    ot = acc_sc[...] / l_sc[...]
    for r in range(A_REP):
        o_ref[0, :, r * HEAD_DIM:(r + 1) * HEAD_DIM] = ot[:, r * tq:(r + 1) * tq].T.astype(o_ref.dtype)


def _gqa(qa, kv, casts, *, tq, tk):
    b, s, _ = qa.shape
    assert s % tk == 0 and s % tq == 0
    gw = A_REP * HEAD_DIM
    rows = A_REP * tq
    grid = (b, A_KV_HEADS, s // tq)
    cast_specs, cast_shapes = _cast_specs(casts, grid)
    return pl.pallas_call(
        _with_casts(functools.partial(_gqa_kernel, tk=tk), 3, 1, len(casts)),
        grid=grid,
        in_specs=[pl.BlockSpec((1, tq, gw), lambda bi, g, i: (bi, i, g)),
                  pl.BlockSpec((None, 1, s, HEAD_DIM), lambda bi, g, i: (g, bi, 0, 0)),
                  pl.BlockSpec((None, 1, s, HEAD_DIM), lambda bi, g, i: (A_KV_HEADS + g, bi, 0, 0))] + cast_specs,
        out_specs=[pl.BlockSpec((1, tq, gw), lambda bi, g, i: (bi, i, g))] + cast_specs,
        out_shape=[jax.ShapeDtypeStruct((b, s, A_Q_W), BF16)] + cast_shapes,
        scratch_shapes=[pltpu.VMEM((1, rows), F32), pltpu.VMEM((1, rows), F32), pltpu.VMEM((HEAD_DIM, rows), F32),
                        pltpu.VMEM((tk, rows), F32), pltpu.VMEM((tk, rows), F32)],
        compiler_params=_params("parallel", "parallel", "arbitrary"),
    )(qa, kv, kv, *[w_f32 for w_f32, _ in casts])


def _proj_deint_kernel(x_ref, wq_ref, wk_ref, wv_ref, o_ref, *stage, dil):
    x = x_ref[...]
    tm = x.shape[0]
    for which, w_ref in enumerate((wq_ref, wk_ref, wv_ref)):
        y = jnp.dot(x, w_ref[...], preferred_element_type=F32)
        if which == 0:
            y = y * Q_SCALE
        tn = y.shape[1]
        if dil == 1:
            o_ref[0, 0, :, which * tn:(which + 1) * tn] = y.astype(o_ref.dtype)
            continue
        y_sc = stage[which]
        n_slabs = tn // LANES
        for s in range(n_slabs):
            y_sc[s] = y[:, s * LANES:(s + 1) * LANES]
        if dil == FAST_STRIDE ** 2:
            z_sc = stage[3 + which]
            part = tm // FAST_STRIDE
            for lo in range(FAST_STRIDE):
                for s in range(n_slabs):
                    z_sc[s, lo * part:(lo + 1) * part, :] = y_sc[s, pl.ds(lo, part, stride=FAST_STRIDE), :]
            for lo in range(FAST_STRIDE):
                for hi in range(FAST_STRIDE):
                    for s in range(n_slabs):
                        c0 = which * tn + s * LANES
                        rows = z_sc[s, pl.ds(lo * part + hi, tm // dil, stride=FAST_STRIDE), :]
                        o_ref[0, lo + FAST_STRIDE * hi, :, c0:c0 + LANES] = rows.astype(o_ref.dtype)
            continue
        for r in range(dil):
            for s in range(n_slabs):
                c0 = which * tn + s * LANES
                o_ref[0, r, :, c0:c0 + LANES] = y_sc[s, pl.ds(r, tm // dil, stride=dil), :].astype(o_ref.dtype)


def _proj_deint(xb, w, g, casts=(), *, tm, bsz, seq_len):
    t, d = xb.shape
    dil = B_DILATIONS[g]
    tn = B_OUT_W
    tiles_per_seq = seq_len // tm
    col0 = COL_QB // tn
    w_spec = lambda which: pl.BlockSpec((d, tn), lambda i: (0, col0 + B_GROUPS * which + g))
    grid = (t // tm,)
    cast_specs, cast_shapes = _cast_specs(casts, grid)
    out_spec = pl.BlockSpec((1, dil, tm // dil, 3 * tn), lambda i: (i // tiles_per_seq, 0, i % tiles_per_seq, 0))
    qkv, *cast_out = pl.pallas_call(
        _with_casts(functools.partial(_proj_deint_kernel, dil=dil), 4, 1, len(casts)),
        grid=grid,
        in_specs=[pl.BlockSpec((tm, d), lambda i: (i, 0)), w_spec(0), w_spec(1), w_spec(2)] + cast_specs,
        out_specs=[out_spec] + cast_specs,
        out_shape=[jax.ShapeDtypeStruct((bsz, dil, seq_len // dil, 3 * tn), BF16)] + cast_shapes,
        scratch_shapes=[pltpu.VMEM((tn // LANES, tm, LANES), F32)] * {1: 0, FAST_STRIDE ** 2: 6}.get(dil, 3),
        compiler_params=_params("parallel"),
    )(xb, w, w, w, *[w_f32 for w_f32, _ in casts])
    return qkv, cast_out


def _dilated_kernel(q_ref, k_ref, v_ref, o_ref, lse_ref, *token_order_scratch, dil, slopes, n_blocks, n_res):
    qb = 2 * B_SIDE
    kw = 4 * B_SIDE
    n_rows = k_ref.shape[2]
    row = lax.broadcasted_iota(jnp.int32, (qb, kw), 0)
    col = lax.broadcasted_iota(jnp.int32, (qb, kw), 1)

    def attend(r, blk, h, ks, valid, token_dist):
        sl = slice(h * HEAD_DIM, (h + 1) * HEAD_DIM)
        q = q_ref[0, r, blk * qb:(blk + 1) * qb, sl]
        k = k_ref[0, r, pl.ds(ks, kw), sl]
        v = v_ref[0, r, pl.ds(ks, kw), sl]
        s = lax.dot_general(q, k, NT_DIMS, preferred_element_type=F32) - slopes[h] * token_dist
        s = jnp.where(valid, s, -jnp.inf)
        m = jnp.max(s, axis=-1, keepdims=True)
        p = jnp.exp(s - m)
        l = jnp.sum(p, axis=-1, keepdims=True)
        o = jnp.dot(p.astype(BF16), v, preferred_element_type=F32) / l
        if dil == 1:
            o_ref[0, blk * qb:(blk + 1) * qb, sl] = o.astype(o_ref.dtype)
        else:
            o_sc, _ = token_order_scratch
            o_sc[h, pl.ds(blk * qb * dil + r, qb, stride=dil), :] = o
        return m + jnp.log(l)

    lane = lax.broadcasted_iota(jnp.int32, (qb, LANES), 1)

    def residues(it, carry):
        for blk in range(n_blocks):
            q0 = (pl.program_id(1) * n_blocks + blk) * qb
            ks = pl.multiple_of(jnp.clip(q0 - B_SIDE, 0, n_rows - kw), B_SIDE)
            dist = jnp.abs(col - row + (ks - q0))
            valid = dist <= B_SIDE
            token_dist = dist.astype(F32) * float(dil)
            for u in range(n_res):
                r = it * n_res + u
                lse = None
                for h in reversed(range(B_HEADS_PER_GROUP)):
                    col_h = jnp.broadcast_to(attend(r, blk, h, ks, valid, token_dist), (qb, LANES))
                    lse = col_h if lse is None else jnp.where(lane < (h + 1) * LSE_LANES, col_h, lse)
                if dil == 1:
                    lse_ref[0, blk * qb:(blk + 1) * qb, :] = lse
                else:
                    _, l_sc = token_order_scratch
                    l_sc[pl.ds(blk * qb * dil + r, qb, stride=dil), :] = lse
        return carry

    if dil == 1:
        residues(0, 0)
    else:
        o_sc, l_sc = token_order_scratch
        lax.fori_loop(0, dil // n_res, residues, 0)
        for h in range(B_HEADS_PER_GROUP):
            sl = slice(h * HEAD_DIM, (h + 1) * HEAD_DIM)
            o_ref[0, :, sl] = o_sc[h].astype(o_ref.dtype)
        lse_ref[0] = l_sc[...]


DILATED_BATCHING = ((8, 1), (8, 1), (1, 8))


def _dilated_group(qkv, g):
    b, dil, n_rows, _ = qkv.shape
    s = n_rows * dil
    n_blocks, n_res = DILATED_BATCHING[g]
    rows = n_blocks * 2 * B_SIDE
    assert n_rows >= 4 * B_SIDE and n_rows % rows == 0 and dil % n_res == 0
    slopes = tuple(2.0 ** (-8.0 * (g * B_HEADS_PER_GROUP + h + 1) / B_HEADS) for h in range(B_HEADS_PER_GROUP))
    kv_spec = lambda which: pl.BlockSpec((1, dil, n_rows, B_OUT_W), lambda bi, i: (bi, 0, 0, which))
    out_spec = lambda w: pl.BlockSpec((1, rows * dil, w), lambda bi, i: (bi, i, 0))
    staging = [pltpu.VMEM((B_HEADS_PER_GROUP, rows * dil, LANES), F32), pltpu.VMEM((rows * dil, LANES), F32)]
    o, lse = pl.pallas_call(
        functools.partial(_dilated_kernel, dil=dil, slopes=slopes, n_blocks=n_blocks, n_res=n_res),
        grid=(b, n_rows // rows),
        in_specs=[pl.BlockSpec((1, dil, rows, B_OUT_W), lambda bi, i: (bi, 0, i, 0)), kv_spec(1), kv_spec(2)],
        out_specs=[out_spec(B_OUT_W), out_spec(LANES)],
        out_shape=[jax.ShapeDtypeStruct((b, s, B_OUT_W), BF16), jax.ShapeDtypeStruct((b, s, LANES), F32)],
        scratch_shapes=staging if dil > 1 else [],
        compiler_params=_params("parallel", "arbitrary"),
    )(qkv, qkv, qkv)
    return o.reshape(b * s, B_OUT_W), lse.reshape(b * s, LANES)


def _mix_out_kernel(oa_ref, o0_ref, o1_ref, o2_ref, l0_ref, l1_ref, l2_ref, ga_ref, gb_ref, x_ref,
                    wpa_ref, wpb_ref, wo_ref, g_ref, b_ref, out_ref):
    l0, l1, l2 = l0_ref[...], l1_ref[...], l2_ref[...]
    mx = jnp.maximum(jnp.maximum(l0, l1), l2)
    e0, e1, e2 = jnp.exp(l0 - mx), jnp.exp(l1 - mx), jnp.exp(l2 - mx)
    inv = 1.0 / (e0 + e1 + e2)
    weights = (e0 * inv, e1 * inv, e2 * inv)
    rows = l0.shape[0]
    merged = []
    for h in range(B_HEADS_PER_GROUP):
        sl = slice(h * HEAD_DIM, (h + 1) * HEAD_DIM)
        acc = None
        for w, o_ref in zip(weights, (o0_ref, o1_ref, o2_ref)):
            term = jnp.broadcast_to(w[:, h * LSE_LANES:h * LSE_LANES + 1], (rows, HEAD_DIM)) * o_ref[:, sl]
            acc = term if acc is None else acc + term
        merged.append(acc.astype(BF16))
    ob = jnp.concatenate(merged, axis=1)
    pa = jnp.dot(oa_ref[...], wpa_ref[...], preferred_element_type=F32)
    pb = jnp.dot(ob, wpb_ref[...], preferred_element_type=F32)
    mixed = (ga_ref[...].astype(F32) * pa + gb_ref[...].astype(F32) * pb).astype(BF16)
    y = jnp.dot(mixed, wo_ref[...], preferred_element_type=F32)
    out_ref[...] = _layer_norm(ALPHA * x_ref[...] + y, g_ref[...], b_ref[...])


def _mix_out(out_a, o_groups, lse_groups, gates, x2d, wpa, wpb, wo, g, b, *, tm):
    t = out_a.shape[0]
    row = lambda w, cb=0: pl.BlockSpec((tm, w), lambda i: (i, cb))
    const = lambda a: pl.BlockSpec(a.shape, lambda i: (0, 0), pipeline_mode=pl.Buffered(1))
    return pl.pallas_call(
        _mix_out_kernel,
        grid=(t // tm,),
        in_specs=[row(A_Q_W)] + [row(B_OUT_W)] * 3 + [row(LANES)] * 3 + [row(D_MODEL, 0), row(D_MODEL, 1), row(D_MODEL),
                  const(wpa), const(wpb), const(wo), const(g), const(b)],
        out_specs=row(D_MODEL),
        out_shape=jax.ShapeDtypeStruct((t, D_MODEL), F32),
        compiler_params=_params("parallel"),
    )(out_a, *o_groups, *lse_groups, gates, gates, x2d, wpa, wpb, wo, g, b)


def _ffn_kernel(x_ref, xprev_ref, wg_ref, wu_ref, wd_ref, g_ref, b_ref, o_ref, xb_sc, acc_sc):
    i = pl.program_id(0)
    f = pl.program_id(1)
    n_tiles = pl.num_programs(0) - 1
    cur = i % 2
    prev = 1 - cur

    def products(xb):
        gate = jnp.dot(xb, wg_ref[...], preferred_element_type=F32)
        up = jnp.dot(xb, wu_ref[...], preferred_element_type=F32)
        h = gate * (0.5 * jnp.tanh(0.5 * gate) + 0.5) * up
        return jnp.dot(h.astype(BF16), wd_ref[...], preferred_element_type=F32)

    @pl.when((i == 0) & (f == 0))
    def _():
        acc_sc[1] = jnp.zeros(acc_sc.shape[1:], F32)

    @pl.when(f == 0)
    def _():
        xb = x_ref[...].astype(BF16)
        xb_sc[...] = xb
        o_ref[...] = _layer_norm(ALPHA * xprev_ref[...] + acc_sc[prev], g_ref[...], b_ref[...])
        acc_sc[cur] = products(xb)

    @pl.when((f > 0) & (i < n_tiles))
    def _():
        acc_sc[cur] += products(xb_sc[...])


def _ffn(x1, wg, wu, wd, g, b, *, tm, tf):
    t, d = x1.shape
    hidden = wg.shape[1]
    n_tiles = t // tm
    vec = pl.BlockSpec((1, d), lambda i, f: (0, 0))
    behind = pl.BlockSpec((tm, d), lambda i, f: (jnp.maximum(i - 1, 0), 0))
    return pl.pallas_call(
        _ffn_kernel,
        grid=(n_tiles + 1, hidden // tf),
        in_specs=[pl.BlockSpec((tm, d), lambda i, f: (jnp.minimum(i, n_tiles - 1), 0)),
                  behind,
                  pl.BlockSpec((d, tf), lambda i, f: (0, f)),
                  pl.BlockSpec((d, tf), lambda i, f: (0, f)),
                  pl.BlockSpec((tf, d), lambda i, f: (f, 0)),
                  vec, vec],
        out_specs=behind,
        out_shape=jax.ShapeDtypeStruct((t, d), F32),
        scratch_shapes=[pltpu.VMEM((tm, d), BF16), pltpu.VMEM((2, tm, d), F32)],
        compiler_params=_params("arbitrary", "arbitrary"),
    )(x1, x1, wg, wu, wd, g, b)


def _rope_tables(seq_len):
    pos = np.arange(seq_len)
    n_freq = ROPE_AXIS_DIM // 2
    freqs = (np.float32(ROPE_THETA) ** (-np.arange(n_freq, dtype=np.float32) / np.float32(n_freq))).astype(np.float32)
    ang_r = (pos // GRID_W).astype(np.float32)[:, None] * freqs[None, :]
    ang_c = (pos % GRID_W).astype(np.float32)[:, None] * freqs[None, :]
    cos = np.concatenate([np.cos(ang_r)] * 2 + [np.cos(ang_c)] * 2, axis=-1).astype(np.float32)
    sin = np.concatenate([-np.sin(ang_r), np.sin(ang_r), -np.sin(ang_c), np.sin(ang_c)], axis=-1).astype(np.float32)
    return jnp.asarray(cos), jnp.asarray(sin)


def _layer(x, w_in, b_gate, q_norm_a, k_norm_a, w_proj_a, w_proj_b, w_out,
           ln1_g, ln1_b, w_ffn_gate, w_ffn_up, w_ffn_down, ln2_g, ln2_b, rope):
    bsz, seq_len, d = x.shape
    t = bsz * seq_len
    x2d = x.reshape(t, d)
    vec = lambda a: a.reshape(1, -1).astype(F32)

    gates, xb, w_in_b = _proj_gate(x2d, w_in[:, COL_GA:].astype(BF16), vec(b_gate), [(w_in, COL_GA)], tm=1024, tn=1024)

    qk_gain = jnp.concatenate([jnp.tile(q_norm_a * (Q_SCALE * LOG2_E), A_Q_HEADS), jnp.tile(k_norm_a, A_KV_HEADS)])
    qa, kv = _proj_a(xb, w_in_b, vec(qk_gain), rope, tm=1024, seq_len=seq_len)

    out_a, w_gate_b, w_up_b = _gqa(qa.reshape(bsz, seq_len, A_Q_W), kv.reshape(2 * A_KV_HEADS, bsz, seq_len, HEAD_DIM),
                                   [(w_ffn_gate, FFN_HIDDEN), (w_ffn_up, FFN_HIDDEN)], tq=256, tk=1024)
    casts = [[(w_ffn_down, D_MODEL)], [(w_proj_a, D_MODEL), (w_proj_b, D_MODEL), (w_out, D_MODEL)], []]
    groups, cast_out = [], []
    for g in range(B_GROUPS):
        qkv, converted = _proj_deint(xb, w_in_b, g, casts[g], tm=1024, bsz=bsz, seq_len=seq_len)
        groups.append(_dilated_group(qkv, g))
        cast_out += converted
    w_down_b, w_proj_a_b, w_proj_b_b, w_out_b = cast_out
    x1 = _mix_out(out_a.reshape(t, A_Q_W), [o for o, _ in groups], [l for _, l in groups], gates, x2d,
                  w_proj_a_b, w_proj_b_b, w_out_b, vec(ln1_g), vec(ln1_b), tm=512)
    x2 = _ffn(x1, w_gate_b, w_up_b, w_down_b, vec(ln2_g), vec(ln2_b), tm=512, tf=512)
    return x2.reshape(bsz, seq_len, d)


def kernel(x, w_in, b_gate, q_norm_a, k_norm_a, w_proj_a, w_proj_b, w_out, ln1_g, ln1_b, w_ffn_gate, w_ffn_up, w_ffn_down, ln2_g, ln2_b):
    rope = _rope_tables(x.shape[1])
    for l in range(w_in.shape[0]):
        x = _layer(x, w_in[l], b_gate[l], q_norm_a[l], k_norm_a[l], w_proj_a[l], w_proj_b[l], w_out[l],
                   ln1_g[l], ln1_b[l], w_ffn_gate[l], w_ffn_up[l], w_ffn_down[l], ln2_g[l], ln2_b[l], rope)
    return x
```

```python
import functools

import jax
import jax.numpy as jnp
import numpy as np
from jax import lax
from jax.experimental import pallas as pl
from jax.experimental.pallas import tpu as pltpu

F32 = jnp.float32
BF16 = jnp.bfloat16

D_MODEL = 2048
HEAD_DIM = 128
A_Q_HEADS = 8
A_KV_HEADS = 2
A_REP = A_Q_HEADS // A_KV_HEADS
ROPE_THETA = 10000.0
ROPE_AXIS_DIM = HEAD_DIM // 2
GRID_W = 64
B_DILATIONS = (1, 4, 16)
B_GROUPS = 3
B_HEADS_PER_GROUP = 4
B_HEADS = B_GROUPS * B_HEADS_PER_GROUP
B_SIDE = 64
A_Q_W = A_Q_HEADS * HEAD_DIM
A_KV_W = A_KV_HEADS * HEAD_DIM
B_W = B_HEADS * HEAD_DIM
B_OUT_W = B_HEADS_PER_GROUP * HEAD_DIM
LSE_LANES = HEAD_DIM // B_HEADS_PER_GROUP
FFN_HIDDEN = 5632
ALPHA = 2.0 ** 0.25
RMS_EPS = 1e-6
LN_EPS = 1e-5
Q_SCALE = HEAD_DIM ** -0.5

COL_QA = 0
COL_KA = A_Q_W
COL_VA = A_Q_W + A_KV_W
COL_QB = COL_VA + A_KV_W
COL_GA = COL_QB + 3 * B_W

VMEM_LIMIT = 60 * 1024 * 1024

NT_DIMS = (((1,), (1,)), ((), ()))
TN_DIMS = (((0,), (0,)), ((), ()))
LANES = 128
BF16_ROWS = 16
FAST_STRIDE = 4
LOG2_E = 1.4426950408889634


def _params(*sem):
    return pltpu.CompilerParams(dimension_semantics=sem, vmem_limit_bytes=VMEM_LIMIT)


def _with_casts(kernel, n_in, n_out, n_casts):
    def wrapped(*refs):
        ins, rest = refs[:n_in], refs[n_in:]
        cast_in, rest = rest[:n_casts], rest[n_casts:]
        outs, rest = rest[:n_out], rest[n_out:]
        cast_out, scratch = rest[:n_casts], rest[n_casts:]
        kernel(*ins, *outs, *scratch)
        for src, dst in zip(cast_in, cast_out):
            dst[...] = src[...].astype(dst.dtype)
    return wrapped


def _cast_specs(weights, grid):
    n_steps = 1
    for n in grid:
        n_steps *= n

    def step(*idx):
        s = 0
        for i, n in zip(idx, grid):
            s = s * n + i
        return s

    specs, shapes = [], []
    for w, n_cols in weights:
        rows = w.shape[0] // n_steps
        assert rows * n_steps == w.shape[0] and rows % BF16_ROWS == 0 and n_cols % LANES == 0
        specs.append(pl.BlockSpec((rows, n_cols), lambda *idx: (step(*idx), 0)))
        shapes.append(jax.ShapeDtypeStruct((w.shape[0], n_cols), BF16))
    return specs, shapes


def _layer_norm(z, g, b):
    mu = jnp.mean(z, axis=-1, keepdims=True)
    zc = z - mu
    var = jnp.mean(zc * zc, axis=-1, keepdims=True)
    return zc * lax.rsqrt(var + LN_EPS) * g + b


def _norm_rope(yh, gain, cos, sin):
    mean_w = jnp.full((2 * HEAD_DIM, HEAD_DIM), 1.0 / HEAD_DIM, BF16)
    sq = yh * yh
    hi = sq.astype(BF16)
    lo = (sq - hi.astype(F32)).astype(BF16)
    ms = jnp.dot(jnp.concatenate([hi, lo], axis=1), mean_w, preferred_element_type=F32)
    yn = yh * lax.rsqrt(ms + RMS_EPS) * gain
    half = ROPE_AXIS_DIM // 2
    lane = lax.broadcasted_iota(jnp.int32, yn.shape, 1)
    partner = jnp.where(lane % ROPE_AXIS_DIM < half, pltpu.roll(yn, HEAD_DIM - half, 1), pltpu.roll(yn, half, 1))
    return yn * cos + partner * sin


def _proj_a_kernel(x_ref, w_ref, gain_ref, cos_ref, sin_ref, q_ref, kv_ref):
    y = jnp.dot(x_ref[...], w_ref[...], preferred_element_type=F32)
    cos = cos_ref[...]
    sin = sin_ref[...]
    head = lambda c: slice(c * HEAD_DIM, (c + 1) * HEAD_DIM)
    for h in range(A_Q_HEADS):
        q_ref[:, head(h)] = _norm_rope(y[:, head(h)], gain_ref[:, head(h)], cos, sin).astype(q_ref.dtype)
    for h in range(A_KV_HEADS):
        c = A_Q_HEADS + h
        kv_ref[h] = _norm_rope(y[:, head(c)], gain_ref[:, head(c)], cos, sin).astype(kv_ref.dtype)
    for h in range(A_KV_HEADS):
        kv_ref[A_KV_HEADS + h] = y[:, head(A_Q_HEADS + A_KV_HEADS + h)].astype(kv_ref.dtype)


def _proj_a(xb, w, gain, rope, *, tm, seq_len):
    t, d = xb.shape
    n_cols = A_Q_W + 2 * A_KV_W
    pos_blocks = seq_len // tm
    tab = pl.BlockSpec((tm, HEAD_DIM), lambda i: (i % pos_blocks, 0))
    return pl.pallas_call(
        _proj_a_kernel,
        grid=(t // tm,),
        in_specs=[pl.BlockSpec((tm, d), lambda i: (i, 0)),
                  pl.BlockSpec((d, n_cols), lambda i: (0, COL_QA // n_cols)),
                  pl.BlockSpec((1, A_Q_W + A_KV_W), lambda i: (0, 0)), tab, tab],
        out_specs=[pl.BlockSpec((tm, A_Q_W), lambda i: (i, 0)),
                   pl.BlockSpec((2 * A_KV_HEADS, tm, HEAD_DIM), lambda i: (0, i, 0))],
        out_shape=[jax.ShapeDtypeStruct((t, A_Q_W), BF16), jax.ShapeDtypeStruct((2 * A_KV_HEADS, t, HEAD_DIM), BF16)],
        compiler_params=_params("parallel"),
    )(xb, w, gain, *rope)


def _proj_gate_kernel(x_ref, w_ref, b_ref, o_ref, xb_ref):
    @pl.when(pl.program_id(1) == 0)
    def _():
        xb_ref[...] = x_ref[...].astype(xb_ref.dtype)

    y = jnp.dot(xb_ref[...], w_ref[...], preferred_element_type=F32)
    o_ref[...] = (0.5 * jnp.tanh(0.5 * (y + b_ref[...])) + 0.5).astype(o_ref.dtype)


def _proj_gate(x2d, w, bias, casts, *, tm, tn):
    t, d = x2d.shape
    n_cols = w.shape[1]
    grid = (t // tm, n_cols // tn)
    cast_specs, cast_shapes = _cast_specs(casts, grid)
    return pl.pallas_call(
        _with_casts(_proj_gate_kernel, 3, 2, len(casts)),
        grid=grid,
        in_specs=[pl.BlockSpec((tm, d), lambda i, j: (i, 0)),
                  pl.BlockSpec((d, tn), lambda i, j: (0, j)),
                  pl.BlockSpec((1, tn), lambda i, j: (0, j))] + cast_specs,
        out_specs=[pl.BlockSpec((tm, tn), lambda i, j: (i, j)), pl.BlockSpec((tm, d), lambda i, j: (i, 0))] + cast_specs,
        out_shape=[jax.ShapeDtypeStruct((t, n_cols), BF16), jax.ShapeDtypeStruct((t, d), BF16)] + cast_shapes,
        compiler_params=_params("parallel", "arbitrary"),
    )(x2d, w, bias, *[w_f32 for w_f32, _ in casts])


def _gqa_kernel(q_ref, k_ref, v_ref, o_ref, m_sc, l_sc, acc_sc, s0_sc, s1_sc, *, tk):
    tq = q_ref.shape[1]
    n_chunks = k_ref.shape[1] // tk
    s_sc = (s0_sc, s1_sc)
    q = jnp.concatenate([q_ref[0, :, r * HEAD_DIM:(r + 1) * HEAD_DIM] for r in range(A_REP)], axis=0)
    m_sc[...] = jnp.full(m_sc.shape, -jnp.inf, F32)
    l_sc[...] = jnp.zeros(l_sc.shape, F32)
    acc_sc[...] = jnp.zeros(acc_sc.shape, F32)

    def scores(c, slot):
        k0 = pl.multiple_of(c * tk, tk)
        s_sc[slot][...] = lax.dot_general(k_ref[0, pl.ds(k0, tk), :], q, NT_DIMS, preferred_element_type=F32)

    def softmax_pv(c, slot):
        k0 = pl.multiple_of(c * tk, tk)
        v = v_ref[0, pl.ds(k0, tk), :]
        st = s_sc[slot][...]
        m_prev = m_sc[...]
        m_new = jnp.maximum(m_prev, jnp.max(st, axis=0, keepdims=True))
        alpha = jnp.exp2(m_prev - m_new)
        pt = jnp.exp2(st - m_new)
        l_sc[...] = alpha * l_sc[...] + jnp.sum(pt, axis=0, keepdims=True)
        pv = lax.dot_general(v, pt.astype(BF16), TN_DIMS, preferred_element_type=F32)
        acc_sc[...] = alpha * acc_sc[...] + pv
        m_sc[...] = m_new

    scores(0, 0)

    def pair(c2, carry):
        c = 2 * c2
        scores(c + 1, 1)
        softmax_pv(c, 0)
        scores(c + 2, 0)
        softmax_pv(c + 1, 1)
        return carry

    lax.fori_loop(0, n_chunks // 2 - 1, pair, 0)
    scores(n_chunks - 1, 1)
    softmax_pv(n_chunks - 2, 0)
    softmax_pv(n_chunks - 1, 1)
    ot = acc_sc[...] / l_sc[...]
    for r in range(A_REP):
        o_ref[0, :, r * HEAD_DIM:(r + 1) * HEAD_DIM] = ot[:, r * tq:(r + 1) * tq].T.astype(o_ref.dtype)


def _gqa(qa, kv, casts, *, tq, tk):
    b, s, _ = qa.shape
    assert s % (2 * tk) == 0 and s % tq == 0
    gw = A_REP * HEAD_DIM
    rows = A_REP * tq
    grid = (b, A_KV_HEADS, s // tq)
    cast_specs, cast_shapes = _cast_specs(casts, grid)
    return pl.pallas_call(
        _with_casts(functools.partial(_gqa_kernel, tk=tk), 3, 1, len(casts)),
        grid=grid,
        in_specs=[pl.BlockSpec((1, tq, gw), lambda bi, g, i: (bi, i, g)),
                  pl.BlockSpec((None, 1, s, HEAD_DIM), lambda bi, g, i: (g, bi, 0, 0)),
                  pl.BlockSpec((None, 1, s, HEAD_DIM), lambda bi, g, i: (A_KV_HEADS + g, bi, 0, 0))] + cast_specs,
        out_specs=[pl.BlockSpec((1, tq, gw), lambda bi, g, i: (bi, i, g))] + cast_specs,
        out_shape=[jax.ShapeDtypeStruct((b, s, A_Q_W), BF16)] + cast_shapes,
        scratch_shapes=[pltpu.VMEM((1, rows), F32), pltpu.VMEM((1, rows), F32), pltpu.VMEM((HEAD_DIM, rows), F32),
                        pltpu.VMEM((tk, rows), F32), pltpu.VMEM((tk, rows), F32)],
        compiler_params=_params("parallel", "parallel", "arbitrary"),
    )(qa, kv, kv, *[w_f32 for w_f32, _ in casts])


def _proj_deint_kernel(x_ref, wq_ref, wk_ref, wv_ref, o_ref, *stage, dil):
    x = x_ref[...]
    tm = x.shape[0]
    for which, w_ref in enumerate((wq_ref, wk_ref, wv_ref)):
        y = jnp.dot(x, w_ref[...], preferred_element_type=F32)
        if which == 0:
            y = y * Q_SCALE
        tn = y.shape[1]
        if dil == 1:
            o_ref[0, 0, :, which * tn:(which + 1) * tn] = y.astype(o_ref.dtype)
            continue
        y_sc = stage[which]
        n_slabs = tn // LANES
        for s in range(n_slabs):
            y_sc[s] = y[:, s * LANES:(s + 1) * LANES]
        if dil == FAST_STRIDE ** 2:
            z_sc = stage[3 + which]
            part = tm // FAST_STRIDE
            for lo in range(FAST_STRIDE):
                for s in range(n_slabs):
                    z_sc[s, lo * part:(lo + 1) * part, :] = y_sc[s, pl.ds(lo, part, stride=FAST_STRIDE), :]
            for lo in range(FAST_STRIDE):
                for hi in range(FAST_STRIDE):
                    for s in range(n_slabs):
                        c0 = which * tn + s * LANES
                        rows = z_sc[s, pl.ds(lo * part + hi, tm // dil, stride=FAST_STRIDE), :]
                        o_ref[0, lo + FAST_STRIDE * hi, :, c0:c0 + LANES] = rows.astype(o_ref.dtype)
            continue
        for r in range(dil):
            for s in range(n_slabs):
                c0 = which * tn + s * LANES
                o_ref[0, r, :, c0:c0 + LANES] = y_sc[s, pl.ds(r, tm // dil, stride=dil), :].astype(o_ref.dtype)


def _proj_deint(xb, w, g, casts=(), *, tm, bsz, seq_len):
    t, d = xb.shape
    dil = B_DILATIONS[g]
    tn = B_OUT_W
    tiles_per_seq = seq_len // tm
    col0 = COL_QB // tn
    w_spec = lambda which: pl.BlockSpec((d, tn), lambda i: (0, col0 + B_GROUPS * which + g))
    grid = (t // tm,)
    cast_specs, cast_shapes = _cast_specs(casts, grid)
    out_spec = pl.BlockSpec((1, dil, tm // dil, 3 * tn), lambda i: (i // tiles_per_seq, 0, i % tiles_per_seq, 0))
    qkv, *cast_out = pl.pallas_call(
        _with_casts(functools.partial(_proj_deint_kernel, dil=dil), 4, 1, len(casts)),
        grid=grid,
        in_specs=[pl.BlockSpec((tm, d), lambda i: (i, 0)), w_spec(0), w_spec(1), w_spec(2)] + cast_specs,
        out_specs=[out_spec] + cast_specs,
        out_shape=[jax.ShapeDtypeStruct((bsz, dil, seq_len // dil, 3 * tn), BF16)] + cast_shapes,
        scratch_shapes=[pltpu.VMEM((tn // LANES, tm, LANES), F32)] * {1: 0, FAST_STRIDE ** 2: 6}.get(dil, 3),
        compiler_params=_params("parallel"),
    )(xb, w, w, w, *[w_f32 for w_f32, _ in casts])
    return qkv, cast_out


def _dilated_kernel(q_ref, k_ref, v_ref, o_ref, lse_ref, *token_order_scratch, dil, slopes, n_blocks, n_res):
    qb = 2 * B_SIDE
    kw = 4 * B_SIDE
    n_rows = k_ref.shape[2]
    row = lax.broadcasted_iota(jnp.int32, (qb, kw), 0)
    col = lax.broadcasted_iota(jnp.int32, (qb, kw), 1)

    def attend(r, blk, h, ks, valid, token_dist):
        sl = slice(h * HEAD_DIM, (h + 1) * HEAD_DIM)
        q = q_ref[0, r, blk * qb:(blk + 1) * qb, sl]
        k = k_ref[0, r, pl.ds(ks, kw), sl]
        v = v_ref[0, r, pl.ds(ks, kw), sl]
        s = lax.dot_general(q, k, NT_DIMS, preferred_element_type=F32) - slopes[h] * token_dist
        s = jnp.where(valid, s, -jnp.inf)
        m = jnp.max(s, axis=-1, keepdims=True)
        p = jnp.exp(s - m)
        l = jnp.sum(p, axis=-1, keepdims=True)
        o = jnp.dot(p.astype(BF16), v, preferred_element_type=F32) / l
        if dil == 1:
            o_ref[0, blk * qb:(blk + 1) * qb, sl] = o.astype(o_ref.dtype)
        else:
            o_sc, _ = token_order_scratch
            o_sc[h, pl.ds(blk * qb * dil + r, qb, stride=dil), :] = o
        return m + jnp.log(l)

    lane = lax.broadcasted_iota(jnp.int32, (qb, LANES), 1)

    def residues(it, carry):
        for blk in range(n_blocks):
            q0 = (pl.program_id(1) * n_blocks + blk) * qb
            ks = pl.multiple_of(jnp.clip(q0 - B_SIDE, 0, n_rows - kw), B_SIDE)
            dist = jnp.abs(col - row + (ks - q0))
            valid = dist <= B_SIDE
            token_dist = dist.astype(F32) * float(dil)
            for u in range(n_res):
                r = it * n_res + u
                lse = None
                for h in reversed(range(B_HEADS_PER_GROUP)):
                    col_h = jnp.broadcast_to(attend(r, blk, h, ks, valid, token_dist), (qb, LANES))
                    lse = col_h if lse is None else jnp.where(lane < (h + 1) * LSE_LANES, col_h, lse)
                if dil == 1:
                    lse_ref[0, blk * qb:(blk + 1) * qb, :] = lse
                else:
                    _, l_sc = token_order_scratch
                    l_sc[pl.ds(blk * qb * dil + r, qb, stride=dil), :] = lse
        return carry

    if dil == 1:
        residues(0, 0)
    else:
        o_sc, l_sc = token_order_scratch
        lax.fori_loop(0, dil // n_res, residues, 0)
        for h in range(B_HEADS_PER_GROUP):
            sl = slice(h * HEAD_DIM, (h + 1) * HEAD_DIM)
            o_ref[0, :, sl] = o_sc[h].astype(o_ref.dtype)
        lse_ref[0] = l_sc[...]


DILATED_BATCHING = ((8, 1), (8, 1), (1, 8))


def _dilated_group(qkv, g):
    b, dil, n_rows, _ = qkv.shape
    s = n_rows * dil
    n_blocks, n_res = DILATED_BATCHING[g]
    rows = n_blocks * 2 * B_SIDE
    assert n_rows >= 4 * B_SIDE and n_rows % rows == 0 and dil % n_res == 0
    slopes = tuple(2.0 ** (-8.0 * (g * B_HEADS_PER_GROUP + h + 1) / B_HEADS) for h in range(B_HEADS_PER_GROUP))
    kv_spec = lambda which: pl.BlockSpec((1, dil, n_rows, B_OUT_W), lambda bi, i: (bi, 0, 0, which))
    out_spec = lambda w: pl.BlockSpec((1, rows * dil, w), lambda bi, i: (bi, i, 0))
    staging = [pltpu.VMEM((B_HEADS_PER_GROUP, rows * dil, LANES), F32), pltpu.VMEM((rows * dil, LANES), F32)]
    o, lse = pl.pallas_call(
        functools.partial(_dilated_kernel, dil=dil, slopes=slopes, n_blocks=n_blocks, n_res=n_res),
        grid=(b, n_rows // rows),
        in_specs=[pl.BlockSpec((1, dil, rows, B_OUT_W), lambda bi, i: (bi, 0, i, 0)), kv_spec(1), kv_spec(2)],
        out_specs=[out_spec(B_OUT_W), out_spec(LANES)],
        out_shape=[jax.ShapeDtypeStruct((b, s, B_OUT_W), BF16), jax.ShapeDtypeStruct((b, s, LANES), F32)],
        scratch_shapes=staging if dil > 1 else [],
        compiler_params=_params("parallel", "arbitrary"),
    )(qkv, qkv, qkv)
    return o.reshape(b * s, B_OUT_W), lse.reshape(b * s, LANES)


def _mix_out_kernel(oa_ref, o0_ref, o1_ref, o2_ref, l0_ref, l1_ref, l2_ref, ga_ref, gb_ref, x_ref,
                    wpa_ref, wpb_ref, wo_ref, g_ref, b_ref, out_ref):
    l0, l1, l2 = l0_ref[...], l1_ref[...], l2_ref[...]
    mx = jnp.maximum(jnp.maximum(l0, l1), l2)
    e0, e1, e2 = jnp.exp(l0 - mx), jnp.exp(l1 - mx), jnp.exp(l2 - mx)
    inv = 1.0 / (e0 + e1 + e2)
    weights = (e0 * inv, e1 * inv, e2 * inv)
    rows = l0.shape[0]
    merged = []
    for h in range(B_HEADS_PER_GROUP):
        sl = slice(h * HEAD_DIM, (h + 1) * HEAD_DIM)
        acc = None
        for w, o_ref in zip(weights, (o0_ref, o1_ref, o2_ref)):
            term = jnp.broadcast_to(w[:, h * LSE_LANES:h * LSE_LANES + 1], (rows, HEAD_DIM)) * o_ref[:, sl]
            acc = term if acc is None else acc + term
        merged.append(acc.astype(BF16))
    ob = jnp.concatenate(merged, axis=1)
    pa = jnp.dot(oa_ref[...], wpa_ref[...], preferred_element_type=F32)
    pb = jnp.dot(ob, wpb_ref[...], preferred_element_type=F32)
    mixed = (ga_ref[...].astype(F32) * pa + gb_ref[...].astype(F32) * pb).astype(BF16)
    y = jnp.dot(mixed, wo_ref[...], preferred_element_type=F32)
    out_ref[...] = _layer_norm(ALPHA * x_ref[...] + y, g_ref[...], b_ref[...])


def _mix_out(out_a, o_groups, lse_groups, gates, x2d, wpa, wpb, wo, g, b, *, tm):
    t = out_a.shape[0]
    row = lambda w, cb=0: pl.BlockSpec((tm, w), lambda i: (i, cb))
    const = lambda a: pl.BlockSpec(a.shape, lambda i: (0, 0), pipeline_mode=pl.Buffered(1))
    return pl.pallas_call(
        _mix_out_kernel,
        grid=(t // tm,),
        in_specs=[row(A_Q_W)] + [row(B_OUT_W)] * 3 + [row(LANES)] * 3 + [row(D_MODEL, 0), row(D_MODEL, 1), row(D_MODEL),
                  const(wpa), const(wpb), const(wo), const(g), const(b)],
        out_specs=row(D_MODEL),
        out_shape=jax.ShapeDtypeStruct((t, D_MODEL), F32),
        compiler_params=_params("parallel"),
    )(out_a, *o_groups, *lse_groups, gates, gates, x2d, wpa, wpb, wo, g, b)


def _ffn_kernel(x_ref, wg_ref, wu_ref, wd_ref, g_ref, b_ref, o_ref, xb_sc, acc_sc):
    f = pl.program_id(1)

    @pl.when(f == 0)
    def _():
        xb_sc[...] = x_ref[...].astype(BF16)
        acc_sc[...] = jnp.zeros(acc_sc.shape, F32)

    xb = xb_sc[...]
    gate = jnp.dot(xb, wg_ref[...], preferred_element_type=F32)
    up = jnp.dot(xb, wu_ref[...], preferred_element_type=F32)
    h = gate * (0.5 * jnp.tanh(0.5 * gate) + 0.5) * up
    acc_sc[...] += jnp.dot(h.astype(BF16), wd_ref[...], preferred_element_type=F32)

    @pl.when(f == pl.num_programs(1) - 1)
    def _():
        o_ref[...] = _layer_norm(ALPHA * x_ref[...] + acc_sc[...], g_ref[...], b_ref[...])


def _ffn(x1, wg, wu, wd, g, b, *, tm, tf):
    t, d = x1.shape
    hidden = wg.shape[1]
    row = pl.BlockSpec((tm, d), lambda i, f: (i, 0))
    vec = pl.BlockSpec((1, d), lambda i, f: (0, 0))
    return pl.pallas_call(
        _ffn_kernel,
        grid=(t // tm, hidden // tf),
        in_specs=[row,
                  pl.BlockSpec((d, tf), lambda i, f: (0, f)),
                  pl.BlockSpec((d, tf), lambda i, f: (0, f)),
                  pl.BlockSpec((tf, d), lambda i, f: (f, 0)),
                  vec, vec],
        out_specs=row,
        out_shape=jax.ShapeDtypeStruct((t, d), F32),
        scratch_shapes=[pltpu.VMEM((tm, d), BF16), pltpu.VMEM((tm, d), F32)],
        compiler_params=_params("parallel", "arbitrary"),
    )(x1, wg, wu, wd, g, b)


def _rope_tables(seq_len):
    pos = np.arange(seq_len)
    n_freq = ROPE_AXIS_DIM // 2
    freqs = (np.float32(ROPE_THETA) ** (-np.arange(n_freq, dtype=np.float32) / np.float32(n_freq))).astype(np.float32)
    ang_r = (pos // GRID_W).astype(np.float32)[:, None] * freqs[None, :]
    ang_c = (pos % GRID_W).astype(np.float32)[:, None] * freqs[None, :]
    cos = np.concatenate([np.cos(ang_r)] * 2 + [np.cos(ang_c)] * 2, axis=-1).astype(np.float32)
    sin = np.concatenate([-np.sin(ang_r), np.sin(ang_r), -np.sin(ang_c), np.sin(ang_c)], axis=-1).astype(np.float32)
    return jnp.asarray(cos), jnp.asarray(sin)


def _layer(x, w_in, b_gate, q_norm_a, k_norm_a, w_proj_a, w_proj_b, w_out,
           ln1_g, ln1_b, w_ffn_gate, w_ffn_up, w_ffn_down, ln2_g, ln2_b, rope):
    bsz, seq_len, d = x.shape
    t = bsz * seq_len
    x2d = x.reshape(t, d)
    vec = lambda a: a.reshape(1, -1).astype(F32)

    gates, xb, w_in_b = _proj_gate(x2d, w_in[:, COL_GA:].astype(BF16), vec(b_gate), [(w_in, COL_GA)], tm=1024, tn=1024)

    qk_gain = jnp.concatenate([jnp.tile(q_norm_a * (Q_SCALE * LOG2_E), A_Q_HEADS), jnp.tile(k_norm_a, A_KV_HEADS)])
    qa, kv = _proj_a(xb, w_in_b, vec(qk_gain), rope, tm=1024, seq_len=seq_len)

    out_a, w_gate_b, w_up_b = _gqa(qa.reshape(bsz, seq_len, A_Q_W), kv.reshape(2 * A_KV_HEADS, bsz, seq_len, HEAD_DIM),
                                   [(w_ffn_gate, FFN_HIDDEN), (w_ffn_up, FFN_HIDDEN)], tq=256, tk=512)
    casts = [[(w_ffn_down, D_MODEL)], [(w_proj_a, D_MODEL), (w_proj_b, D_MODEL), (w_out, D_MODEL)], []]
    groups, cast_out = [], []
    for g in range(B_GROUPS):
        qkv, converted = _proj_deint(xb, w_in_b, g, casts[g], tm=1024, bsz=bsz, seq_len=seq_len)
        groups.append(_dilated_group(qkv, g))
        cast_out += converted
    w_down_b, w_proj_a_b, w_proj_b_b, w_out_b = cast_out
    x1 = _mix_out(out_a.reshape(t, A_Q_W), [o for o, _ in groups], [l for _, l in groups], gates, x2d,
                  w_proj_a_b, w_proj_b_b, w_out_b, vec(ln1_g), vec(ln1_b), tm=512)
    x2 = _ffn(x1, w_gate_b, w_up_b, w_down_b, vec(ln2_g), vec(ln2_b), tm=512, tf=512)
    return x2.reshape(bsz, seq_len, d)


def kernel(x, w_in, b_gate, q_norm_a, k_norm_a, w_proj_a, w_proj_b, w_out, ln1_g, ln1_b, w_ffn_gate, w_ffn_up, w_ffn_down, ln2_g, ln2_b):
    rope = _rope_tables(x.shape[1])
    for l in range(w_in.shape[0]):
        x = _layer(x, w_in[l], b_gate[l], q_norm_a[l], k_norm_a[l], w_proj_a[l], w_proj_b[l], w_out[l],
                   ln1_g[l], ln1_b[l], w_ffn_gate[l], w_ffn_up[l], w_ffn_down[l], ln2_g[l], ln2_b[l], rope)
    return x
```

```python
import functools

import jax
import jax.numpy as jnp
import numpy as np
from jax import lax
from jax.experimental import pallas as pl
from jax.experimental.pallas import tpu as pltpu

F32 = jnp.float32
BF16 = jnp.bfloat16

D_MODEL = 2048
HEAD_DIM = 128
A_Q_HEADS = 8
A_KV_HEADS = 2
A_REP = A_Q_HEADS // A_KV_HEADS
ROPE_THETA = 10000.0
ROPE_AXIS_DIM = HEAD_DIM // 2
GRID_W = 64
B_DILATIONS = (1, 4, 16)
B_GROUPS = 3
B_HEADS_PER_GROUP = 4
B_HEADS = B_GROUPS * B_HEADS_PER_GROUP
B_SIDE = 64
A_Q_W = A_Q_HEADS * HEAD_DIM
A_KV_W = A_KV_HEADS * HEAD_DIM
B_W = B_HEADS * HEAD_DIM
B_OUT_W = B_HEADS_PER_GROUP * HEAD_DIM
LSE_LANES = HEAD_DIM // B_HEADS_PER_GROUP
FFN_HIDDEN = 5632
ALPHA = 2.0 ** 0.25
RMS_EPS = 1e-6
LN_EPS = 1e-5
Q_SCALE = HEAD_DIM ** -0.5

COL_QA = 0
COL_KA = A_Q_W
COL_VA = A_Q_W + A_KV_W
COL_QB = COL_VA + A_KV_W
COL_GA = COL_QB + 3 * B_W

VMEM_LIMIT = 60 * 1024 * 1024

NT_DIMS = (((1,), (1,)), ((), ()))
TN_DIMS = (((0,), (0,)), ((), ()))
LANES = 128
BF16_ROWS = 16
MIX_ROW_SPLITS = 2
FAST_STRIDE = 4
LOG2_E = 1.4426950408889634


def _params(*sem):
    return pltpu.CompilerParams(dimension_semantics=sem, vmem_limit_bytes=VMEM_LIMIT)


def _with_casts(kernel, n_in, n_out, n_casts):
    def wrapped(*refs):
        ins, rest = refs[:n_in], refs[n_in:]
        cast_in, rest = rest[:n_casts], rest[n_casts:]
        outs, rest = rest[:n_out], rest[n_out:]
        cast_out, scratch = rest[:n_casts], rest[n_casts:]
        kernel(*ins, *outs, *scratch)
        for src, dst in zip(cast_in, cast_out):
            dst[...] = src[...].astype(dst.dtype)
    return wrapped


def _cast_specs(weights, grid):
    n_steps = 1
    for n in grid:
        n_steps *= n

    def step(*idx):
        s = 0
        for i, n in zip(idx, grid):
            s = s * n + i
        return s

    specs, shapes = [], []
    for w, n_cols in weights:
        rows = w.shape[0] // n_steps
        assert rows * n_steps == w.shape[0] and rows % BF16_ROWS == 0 and n_cols % LANES == 0
        specs.append(pl.BlockSpec((rows, n_cols), lambda *idx: (step(*idx), 0)))
        shapes.append(jax.ShapeDtypeStruct((w.shape[0], n_cols), BF16))
    return specs, shapes


def _layer_norm(z, g, b):
    mu = jnp.mean(z, axis=-1, keepdims=True)
    zc = z - mu
    var = jnp.mean(zc * zc, axis=-1, keepdims=True)
    return zc * lax.rsqrt(var + LN_EPS) * g + b


def _norm_rope(yh, gain, cos, sin):
    mean_w = jnp.full((2 * HEAD_DIM, HEAD_DIM), 1.0 / HEAD_DIM, BF16)
    sq = yh * yh
    hi = sq.astype(BF16)
    lo = (sq - hi.astype(F32)).astype(BF16)
    ms = jnp.dot(jnp.concatenate([hi, lo], axis=1), mean_w, preferred_element_type=F32)
    yn = yh * lax.rsqrt(ms + RMS_EPS) * gain
    half = ROPE_AXIS_DIM // 2
    lane = lax.broadcasted_iota(jnp.int32, yn.shape, 1)
    partner = jnp.where(lane % ROPE_AXIS_DIM < half, pltpu.roll(yn, HEAD_DIM - half, 1), pltpu.roll(yn, half, 1))
    return yn * cos + partner * sin


def _proj_a_kernel(x_ref, w_ref, gain_ref, cos_ref, sin_ref, q_ref, kv_ref):
    y = jnp.dot(x_ref[...], w_ref[...], preferred_element_type=F32)
    cos = cos_ref[...]
    sin = sin_ref[...]
    head = lambda c: slice(c * HEAD_DIM, (c + 1) * HEAD_DIM)
    for h in range(A_Q_HEADS):
        q_ref[:, head(h)] = _norm_rope(y[:, head(h)], gain_ref[:, head(h)], cos, sin).astype(q_ref.dtype)
    for h in range(A_KV_HEADS):
        c = A_Q_HEADS + h
        kv_ref[h] = _norm_rope(y[:, head(c)], gain_ref[:, head(c)], cos, sin).astype(kv_ref.dtype)
    for h in range(A_KV_HEADS):
        kv_ref[A_KV_HEADS + h] = y[:, head(A_Q_HEADS + A_KV_HEADS + h)].astype(kv_ref.dtype)


def _proj_a(xb, w, gain, rope, *, tm, seq_len):
    t, d = xb.shape
    n_cols = A_Q_W + 2 * A_KV_W
    pos_blocks = seq_len // tm
    tab = pl.BlockSpec((tm, HEAD_DIM), lambda i: (i % pos_blocks, 0))
    return pl.pallas_call(
        _proj_a_kernel,
        grid=(t // tm,),
        in_specs=[pl.BlockSpec((tm, d), lambda i: (i, 0)),
                  pl.BlockSpec((d, n_cols), lambda i: (0, COL_QA // n_cols)),
                  pl.BlockSpec((1, A_Q_W + A_KV_W), lambda i: (0, 0)), tab, tab],
        out_specs=[pl.BlockSpec((tm, A_Q_W), lambda i: (i, 0)),
                   pl.BlockSpec((2 * A_KV_HEADS, tm, HEAD_DIM), lambda i: (0, i, 0))],
        out_shape=[jax.ShapeDtypeStruct((t, A_Q_W), BF16), jax.ShapeDtypeStruct((2 * A_KV_HEADS, t, HEAD_DIM), BF16)],
        compiler_params=_params("parallel"),
    )(xb, w, gain, *rope)


def _proj_gate_kernel(x_ref, w_ref, b_ref, o_ref, xb_ref):
    @pl.when(pl.program_id(1) == 0)
    def _():
        xb_ref[...] = x_ref[...].astype(xb_ref.dtype)

    y = jnp.dot(xb_ref[...], w_ref[...], preferred_element_type=F32)
    o_ref[...] = (0.5 * jnp.tanh(0.5 * (y + b_ref[...])) + 0.5).astype(o_ref.dtype)


def _proj_gate(x2d, w, bias, casts, *, tm, tn):
    t, d = x2d.shape
    n_cols = w.shape[1]
    grid = (t // tm, n_cols // tn)
    cast_specs, cast_shapes = _cast_specs(casts, grid)
    return pl.pallas_call(
        _with_casts(_proj_gate_kernel, 3, 2, len(casts)),
        grid=grid,
        in_specs=[pl.BlockSpec((tm, d), lambda i, j: (i, 0)),
                  pl.BlockSpec((d, tn), lambda i, j: (0, j)),
                  pl.BlockSpec((1, tn), lambda i, j: (0, j))] + cast_specs,
        out_specs=[pl.BlockSpec((tm, tn), lambda i, j: (i, j)), pl.BlockSpec((tm, d), lambda i, j: (i, 0))] + cast_specs,
        out_shape=[jax.ShapeDtypeStruct((t, n_cols), BF16), jax.ShapeDtypeStruct((t, d), BF16)] + cast_shapes,
        compiler_params=_params("parallel", "arbitrary"),
    )(x2d, w, bias, *[w_f32 for w_f32, _ in casts])


def _gqa_kernel(q_ref, k_ref, v_ref, o_ref, m_sc, l_sc, acc_sc, s0_sc, s1_sc, *, tk):
    tq = q_ref.shape[1]
    n_chunks = k_ref.shape[1] // tk
    s_sc = (s0_sc, s1_sc)
    q = jnp.concatenate([q_ref[0, :, r * HEAD_DIM:(r + 1) * HEAD_DIM] for r in range(A_REP)], axis=0)
    m_sc[...] = jnp.full(m_sc.shape, -jnp.inf, F32)
    l_sc[...] = jnp.zeros(l_sc.shape, F32)
    acc_sc[...] = jnp.zeros(acc_sc.shape, F32)

    def scores(c, slot):
        s_sc[slot][...] = lax.dot_general(k_ref[0, c * tk:(c + 1) * tk, :], q, NT_DIMS, preferred_element_type=F32)

    def softmax_pv(c, slot):
        v = v_ref[0, c * tk:(c + 1) * tk, :]
        st = s_sc[slot][...]
        m_prev = m_sc[...]
        m_new = jnp.maximum(m_prev, jnp.max(st, axis=0, keepdims=True))
        alpha = jnp.exp2(m_prev - m_new)
        pt = jnp.exp2(st - m_new)
        l_sc[...] = alpha * l_sc[...] + jnp.sum(pt, axis=0, keepdims=True)
        pv = lax.dot_general(v, pt.astype(BF16), TN_DIMS, preferred_element_type=F32)
        acc_sc[...] = alpha * acc_sc[...] + pv
        m_sc[...] = m_new

    scores(0, 0)
    for c in range(n_chunks - 1):
        scores(c + 1, (c + 1) % 2)
        softmax_pv(c, c % 2)
    softmax_pv(n_chunks - 1, (n_chunks - 1) % 2)
    ot = acc_sc[...] / l_sc[...]
    for r in range(A_REP):
        o_ref[0, :, r * HEAD_DIM:(r + 1) * HEAD_DIM] = ot[:, r * tq:(r + 1) * tq].T.astype(o_ref.dtype)


def _gqa(qa, kv, casts, *, tq, tk):
    b, s, _ = qa.shape
    assert s % tk == 0 and s % tq == 0
    gw = A_REP * HEAD_DIM
    rows = A_REP * tq
    grid = (b, A_KV_HEADS, s // tq)
    cast_specs, cast_shapes = _cast_specs(casts, grid)
    return pl.pallas_call(
        _with_casts(functools.partial(_gqa_kernel, tk=tk), 3, 1, len(casts)),
        grid=grid,
        in_specs=[pl.BlockSpec((1, tq, gw), lambda bi, g, i: (bi, i, g)),
                  pl.BlockSpec((None, 1, s, HEAD_DIM), lambda bi, g, i: (g, bi, 0, 0)),
                  pl.BlockSpec((None, 1, s, HEAD_DIM), lambda bi, g, i: (A_KV_HEADS + g, bi, 0, 0))] + cast_specs,
        out_specs=[pl.BlockSpec((1, tq, gw), lambda bi, g, i: (bi, i, g))] + cast_specs,
        out_shape=[jax.ShapeDtypeStruct((b, s, A_Q_W), BF16)] + cast_shapes,
        scratch_shapes=[pltpu.VMEM((1, rows), F32), pltpu.VMEM((1, rows), F32), pltpu.VMEM((HEAD_DIM, rows), F32),
                        pltpu.VMEM((tk, rows), F32), pltpu.VMEM((tk, rows), F32)],
        compiler_params=_params("parallel", "parallel", "arbitrary"),
    )(qa, kv, kv, *[w_f32 for w_f32, _ in casts])


def _proj_deint_kernel(x_ref, wq_ref, wk_ref, wv_ref, o_ref, *stage, dil):
    x = x_ref[...]
    tm = x.shape[0]
    for which, w_ref in enumerate((wq_ref, wk_ref, wv_ref)):
        y = jnp.dot(x, w_ref[...], preferred_element_type=F32)
        if which == 0:
            y = y * Q_SCALE
        tn = y.shape[1]
        if dil == 1:
            o_ref[0, 0, :, which * tn:(which + 1) * tn] = y.astype(o_ref.dtype)
            continue
        y_sc = stage[which]
        n_slabs = tn // LANES
        for s in range(n_slabs):
            y_sc[s] = y[:, s * LANES:(s + 1) * LANES]
        if dil == FAST_STRIDE ** 2:
            z_sc = stage[3 + which]
            part = tm // FAST_STRIDE
            for lo in range(FAST_STRIDE):
                for s in range(n_slabs):
                    z_sc[s, lo * part:(lo + 1) * part, :] = y_sc[s, pl.ds(lo, part, stride=FAST_STRIDE), :]
            for lo in range(FAST_STRIDE):
                for hi in range(FAST_STRIDE):
                    for s in range(n_slabs):
                        c0 = which * tn + s * LANES
                        rows = z_sc[s, pl.ds(lo * part + hi, tm // dil, stride=FAST_STRIDE), :]
                        o_ref[0, lo + FAST_STRIDE * hi, :, c0:c0 + LANES] = rows.astype(o_ref.dtype)
            continue
        for r in range(dil):
            for s in range(n_slabs):
                c0 = which * tn + s * LANES
                o_ref[0, r, :, c0:c0 + LANES] = y_sc[s, pl.ds(r, tm // dil, stride=dil), :].astype(o_ref.dtype)


def _proj_deint(xb, w, g, casts=(), *, tm, bsz, seq_len):
    t, d = xb.shape
    dil = B_DILATIONS[g]
    tn = B_OUT_W
    tiles_per_seq = seq_len // tm
    col0 = COL_QB // tn
    w_spec = lambda which: pl.BlockSpec((d, tn), lambda i: (0, col0 + B_GROUPS * which + g))
    grid = (t // tm,)
    cast_specs, cast_shapes = _cast_specs(casts, grid)
    out_spec = pl.BlockSpec((1, dil, tm // dil, 3 * tn), lambda i: (i // tiles_per_seq, 0, i % tiles_per_seq, 0))
    qkv, *cast_out = pl.pallas_call(
        _with_casts(functools.partial(_proj_deint_kernel, dil=dil), 4, 1, len(casts)),
        grid=grid,
        in_specs=[pl.BlockSpec((tm, d), lambda i: (i, 0)), w_spec(0), w_spec(1), w_spec(2)] + cast_specs,
        out_specs=[out_spec] + cast_specs,
        out_shape=[jax.ShapeDtypeStruct((bsz, dil, seq_len // dil, 3 * tn), BF16)] + cast_shapes,
        scratch_shapes=[pltpu.VMEM((tn // LANES, tm, LANES), F32)] * {1: 0, FAST_STRIDE ** 2: 6}.get(dil, 3),
        compiler_params=_params("parallel"),
    )(xb, w, w, w, *[w_f32 for w_f32, _ in casts])
    return qkv, cast_out


def _dilated_kernel(q_ref, k_ref, v_ref, o_ref, lse_ref, *token_order_scratch, dil, slopes, n_blocks, n_res):
    qb = 2 * B_SIDE
    kw = 4 * B_SIDE
    n_rows = k_ref.shape[2]
    row = lax.broadcasted_iota(jnp.int32, (qb, kw), 0)
    col = lax.broadcasted_iota(jnp.int32, (qb, kw), 1)

    def attend(r, blk, h, ks, valid, token_dist):
        sl = slice(h * HEAD_DIM, (h + 1) * HEAD_DIM)
        q = q_ref[0, r, blk * qb:(blk + 1) * qb, sl]
        k = k_ref[0, r, pl.ds(ks, kw), sl]
        v = v_ref[0, r, pl.ds(ks, kw), sl]
        s = lax.dot_general(q, k, NT_DIMS, preferred_element_type=F32) - slopes[h] * token_dist
        s = jnp.where(valid, s, -jnp.inf)
        m = jnp.max(s, axis=-1, keepdims=True)
        p = jnp.exp(s - m)
        l = jnp.sum(p, axis=-1, keepdims=True)
        o = jnp.dot(p.astype(BF16), v, preferred_element_type=F32) / l
        if dil == 1:
            o_ref[0, blk * qb:(blk + 1) * qb, sl] = o.astype(o_ref.dtype)
        else:
            o_sc, _ = token_order_scratch
            o_sc[h, pl.ds(blk * qb * dil + r, qb, stride=dil), :] = o
        return m + jnp.log(l)

    lane = lax.broadcasted_iota(jnp.int32, (qb, LANES), 1)

    def residues(it, carry):
        for blk in range(n_blocks):
            q0 = (pl.program_id(1) * n_blocks + blk) * qb
            ks = pl.multiple_of(jnp.clip(q0 - B_SIDE, 0, n_rows - kw), B_SIDE)
            dist = jnp.abs(col - row + (ks - q0))
            valid = dist <= B_SIDE
            token_dist = dist.astype(F32) * float(dil)
            for u in range(n_res):
                r = it * n_res + u
                lse = None
                for h in reversed(range(B_HEADS_PER_GROUP)):
                    col_h = jnp.broadcast_to(attend(r, blk, h, ks, valid, token_dist), (qb, LANES))
                    lse = col_h if lse is None else jnp.where(lane < (h + 1) * LSE_LANES, col_h, lse)
                if dil == 1:
                    lse_ref[0, blk * qb:(blk + 1) * qb, :] = lse
                else:
                    _, l_sc = token_order_scratch
                    l_sc[pl.ds(blk * qb * dil + r, qb, stride=dil), :] = lse
        return carry

    if dil == 1:
        residues(0, 0)
    else:
        o_sc, l_sc = token_order_scratch
        lax.fori_loop(0, dil // n_res, residues, 0)
        for h in range(B_HEADS_PER_GROUP):
            sl = slice(h * HEAD_DIM, (h + 1) * HEAD_DIM)
            o_ref[0, :, sl] = o_sc[h].astype(o_ref.dtype)
        lse_ref[0] = l_sc[...]


DILATED_BATCHING = ((8, 1), (8, 1), (1, 8))


def _dilated_group(qkv, g):
    b, dil, n_rows, _ = qkv.shape
    s = n_rows * dil
    n_blocks, n_res = DILATED_BATCHING[g]
    rows = n_blocks * 2 * B_SIDE
    assert n_rows >= 4 * B_SIDE and n_rows % rows == 0 and dil % n_res == 0
    slopes = tuple(2.0 ** (-8.0 * (g * B_HEADS_PER_GROUP + h + 1) / B_HEADS) for h in range(B_HEADS_PER_GROUP))
    kv_spec = lambda which: pl.BlockSpec((1, dil, n_rows, B_OUT_W), lambda bi, i: (bi, 0, 0, which))
    out_spec = lambda w: pl.BlockSpec((1, rows * dil, w), lambda bi, i: (bi, i, 0))
    staging = [pltpu.VMEM((B_HEADS_PER_GROUP, rows * dil, LANES), F32), pltpu.VMEM((rows * dil, LANES), F32)]
    o, lse = pl.pallas_call(
        functools.partial(_dilated_kernel, dil=dil, slopes=slopes, n_blocks=n_blocks, n_res=n_res),
        grid=(b, n_rows // rows),
        in_specs=[pl.BlockSpec((1, dil, rows, B_OUT_W), lambda bi, i: (bi, 0, i, 0)), kv_spec(1), kv_spec(2)],
        out_specs=[out_spec(B_OUT_W), out_spec(LANES)],
        out_shape=[jax.ShapeDtypeStruct((b, s, B_OUT_W), BF16), jax.ShapeDtypeStruct((b, s, LANES), F32)],
        scratch_shapes=staging if dil > 1 else [],
        compiler_params=_params("parallel", "arbitrary"),
    )(qkv, qkv, qkv)
    return o.reshape(b * s, B_OUT_W), lse.reshape(b * s, LANES)


def _mix_out_kernel(oa_ref, o0_ref, o1_ref, o2_ref, l0_ref, l1_ref, l2_ref, ga_ref, gb_ref, x_ref,
                    wpa_ref, wpb_ref, wo_ref, g_ref, b_ref, out_ref):
    rows = out_ref.shape[0] // MIX_ROW_SPLITS
    for part in range(MIX_ROW_SPLITS):
        rs = slice(part * rows, (part + 1) * rows)
        l0, l1, l2 = l0_ref[rs, :], l1_ref[rs, :], l2_ref[rs, :]
        mx = jnp.maximum(jnp.maximum(l0, l1), l2)
        e0, e1, e2 = jnp.exp(l0 - mx), jnp.exp(l1 - mx), jnp.exp(l2 - mx)
        inv = 1.0 / (e0 + e1 + e2)
        weights = (e0 * inv, e1 * inv, e2 * inv)
        merged = []
        for h in range(B_HEADS_PER_GROUP):
            sl = slice(h * HEAD_DIM, (h + 1) * HEAD_DIM)
            acc = None
            for w, o_ref in zip(weights, (o0_ref, o1_ref, o2_ref)):
                term = jnp.broadcast_to(w[:, h * LSE_LANES:h * LSE_LANES + 1], (rows, HEAD_DIM)) * o_ref[rs, sl]
                acc = term if acc is None else acc + term
            merged.append(acc.astype(BF16))
        ob = jnp.concatenate(merged, axis=1)
        pa = jnp.dot(oa_ref[rs, :], wpa_ref[...], preferred_element_type=F32)
        pb = jnp.dot(ob, wpb_ref[...], preferred_element_type=F32)
        mixed = (ga_ref[rs, :].astype(F32) * pa + gb_ref[rs, :].astype(F32) * pb).astype(BF16)
        y = jnp.dot(mixed, wo_ref[...], preferred_element_type=F32)
        out_ref[rs, :] = _layer_norm(ALPHA * x_ref[rs, :] + y, g_ref[...], b_ref[...])


def _mix_out(out_a, o_groups, lse_groups, gates, x2d, wpa, wpb, wo, g, b, *, tm):
    t = out_a.shape[0]
    row = lambda w, cb=0: pl.BlockSpec((tm, w), lambda i: (i, cb))
    const = lambda a: pl.BlockSpec(a.shape, lambda i: (0, 0), pipeline_mode=pl.Buffered(1))
    return pl.pallas_call(
        _mix_out_kernel,
        grid=(t // tm,),
        in_specs=[row(A_Q_W)] + [row(B_OUT_W)] * 3 + [row(LANES)] * 3 + [row(D_MODEL, 0), row(D_MODEL, 1), row(D_MODEL),
                  const(wpa), const(wpb), const(wo), const(g), const(b)],
        out_specs=row(D_MODEL),
        out_shape=jax.ShapeDtypeStruct((t, D_MODEL), F32),
        compiler_params=_params("parallel"),
    )(out_a, *o_groups, *lse_groups, gates, gates, x2d, wpa, wpb, wo, g, b)


def _ffn_kernel(x_ref, wg_ref, wu_ref, wd_ref, g_ref, b_ref, o_ref, xb_sc, acc_sc):
    f = pl.program_id(1)

    @pl.when(f == 0)
    def _():
        xb_sc[...] = x_ref[...].astype(BF16)
        acc_sc[...] = jnp.zeros(acc_sc.shape, F32)

    xb = xb_sc[...]
    gate = jnp.dot(xb, wg_ref[...], preferred_element_type=F32)
    up = jnp.dot(xb, wu_ref[...], preferred_element_type=F32)
    h = gate * (0.5 * jnp.tanh(0.5 * gate) + 0.5) * up
    acc_sc[...] += jnp.dot(h.astype(BF16), wd_ref[...], preferred_element_type=F32)

    @pl.when(f == pl.num_programs(1) - 1)
    def _():
        o_ref[...] = _layer_norm(ALPHA * x_ref[...] + acc_sc[...], g_ref[...], b_ref[...])


def _ffn(x1, wg, wu, wd, g, b, *, tm, tf):
    t, d = x1.shape
    hidden = wg.shape[1]
    row = pl.BlockSpec((tm, d), lambda i, f: (i, 0))
    vec = pl.BlockSpec((1, d), lambda i, f: (0, 0))
    return pl.pallas_call(
        _ffn_kernel,
        grid=(t // tm, hidden // tf),
        in_specs=[row,
                  pl.BlockSpec((d, tf), lambda i, f: (0, f)),
                  pl.BlockSpec((d, tf), lambda i, f: (0, f)),
                  pl.BlockSpec((tf, d), lambda i, f: (f, 0)),
                  vec, vec],
        out_specs=row,
        out_shape=jax.ShapeDtypeStruct((t, d), F32),
        scratch_shapes=[pltpu.VMEM((tm, d), BF16), pltpu.VMEM((tm, d), F32)],
        compiler_params=_params("parallel", "arbitrary"),
    )(x1, wg, wu, wd, g, b)


def _rope_tables(seq_len):
    pos = np.arange(seq_len)
    n_freq = ROPE_AXIS_DIM // 2
    freqs = (np.float32(ROPE_THETA) ** (-np.arange(n_freq, dtype=np.float32) / np.float32(n_freq))).astype(np.float32)
    ang_r = (pos // GRID_W).astype(np.float32)[:, None] * freqs[None, :]
    ang_c = (pos % GRID_W).astype(np.float32)[:, None] * freqs[None, :]
    cos = np.concatenate([np.cos(ang_r)] * 2 + [np.cos(ang_c)] * 2, axis=-1).astype(np.float32)
    sin = np.concatenate([-np.sin(ang_r), np.sin(ang_r), -np.sin(ang_c), np.sin(ang_c)], axis=-1).astype(np.float32)
    return jnp.asarray(cos), jnp.asarray(sin)


def _layer(x, w_in, b_gate, q_norm_a, k_norm_a, w_proj_a, w_proj_b, w_out,
           ln1_g, ln1_b, w_ffn_gate, w_ffn_up, w_ffn_down, ln2_g, ln2_b, rope):
    bsz, seq_len, d = x.shape
    t = bsz * seq_len
    x2d = x.reshape(t, d)
    vec = lambda a: a.reshape(1, -1).astype(F32)

    gates, xb, w_in_b = _proj_gate(x2d, w_in[:, COL_GA:].astype(BF16), vec(b_gate), [(w_in, COL_GA)], tm=1024, tn=1024)

    qk_gain = jnp.concatenate([jnp.tile(q_norm_a * (Q_SCALE * LOG2_E), A_Q_HEADS), jnp.tile(k_norm_a, A_KV_HEADS)])
    qa, kv = _proj_a(xb, w_in_b, vec(qk_gain), rope, tm=1024, seq_len=seq_len)

    out_a, w_gate_b, w_up_b = _gqa(qa.reshape(bsz, seq_len, A_Q_W), kv.reshape(2 * A_KV_HEADS, bsz, seq_len, HEAD_DIM),
                                   [(w_ffn_gate, FFN_HIDDEN), (w_ffn_up, FFN_HIDDEN)], tq=256, tk=1024)
    casts = [[(w_ffn_down, D_MODEL)], [(w_proj_a, D_MODEL), (w_proj_b, D_MODEL), (w_out, D_MODEL)], []]
    groups, cast_out = [], []
    for g in range(B_GROUPS):
        qkv, converted = _proj_deint(xb, w_in_b, g, casts[g], tm=1024, bsz=bsz, seq_len=seq_len)
        groups.append(_dilated_group(qkv, g))
        cast_out += converted
    w_down_b, w_proj_a_b, w_proj_b_b, w_out_b = cast_out
    x1 = _mix_out(out_a.reshape(t, A_Q_W), [o for o, _ in groups], [l for _, l in groups], gates, x2d,
                  w_proj_a_b, w_proj_b_b, w_out_b, vec(ln1_g), vec(ln1_b), tm=512)
    x2 = _ffn(x1, w_gate_b, w_up_b, w_down_b, vec(ln2_g), vec(ln2_b), tm=512, tf=512)
    return x2.reshape(bsz, seq_len, d)


def kernel(x, w_in, b_gate, q_norm_a, k_norm_a, w_proj_a, w_proj_b, w_out, ln1_g, ln1_b, w_ffn_gate, w_ffn_up, w_ffn_down, ln2_g, ln2_b):
    rope = _rope_tables(x.shape[1])
    for l in range(w_in.shape[0]):
        x = _layer(x, w_in[l], b_gate[l], q_norm_a[l], k_norm_a[l], w_proj_a[l], w_proj_b[l], w_out[l],
                   ln1_g[l], ln1_b[l], w_ffn_gate[l], w_ffn_up[l], w_ffn_down[l], ln2_g[l], ln2_b[l], rope)
    return x
```

```python
import functools

import jax
import jax.numpy as jnp
import numpy as np
from jax import lax
from jax.experimental import pallas as pl
from jax.experimental.pallas import tpu as pltpu

F32 = jnp.float32
BF16 = jnp.bfloat16

D_MODEL = 2048
HEAD_DIM = 128
A_Q_HEADS = 8
A_KV_HEADS = 2
A_REP = A_Q_HEADS // A_KV_HEADS
ROPE_THETA = 10000.0
ROPE_AXIS_DIM = HEAD_DIM // 2
GRID_W = 64
B_DILATIONS = (1, 4, 16)
B_GROUPS = 3
B_HEADS_PER_GROUP = 4
B_HEADS = B_GROUPS * B_HEADS_PER_GROUP
B_SIDE = 64
A_Q_W = A_Q_HEADS * HEAD_DIM
A_KV_W = A_KV_HEADS * HEAD_DIM
B_W = B_HEADS * HEAD_DIM
B_OUT_W = B_HEADS_PER_GROUP * HEAD_DIM
LSE_LANES = HEAD_DIM // B_HEADS_PER_GROUP
FFN_HIDDEN = 5632
ALPHA = 2.0 ** 0.25
RMS_EPS = 1e-6
LN_EPS = 1e-5
Q_SCALE = HEAD_DIM ** -0.5

COL_QA = 0
COL_KA = A_Q_W
COL_VA = A_Q_W + A_KV_W
COL_QB = COL_VA + A_KV_W
COL_GA = COL_QB + 3 * B_W

VMEM_LIMIT = 60 * 1024 * 1024

NT_DIMS = (((1,), (1,)), ((), ()))
TN_DIMS = (((0,), (0,)), ((), ()))
LANES = 128
BF16_ROWS = 16
FAST_STRIDE = 4
LOG2_E = 1.4426950408889634


def _params(*sem):
    return pltpu.CompilerParams(dimension_semantics=sem, vmem_limit_bytes=VMEM_LIMIT)


def _with_casts(kernel, n_in, n_out, n_casts):
    def wrapped(*refs):
        ins, rest = refs[:n_in], refs[n_in:]
        cast_in, rest = rest[:n_casts], rest[n_casts:]
        outs, rest = rest[:n_out], rest[n_out:]
        cast_out, scratch = rest[:n_casts], rest[n_casts:]
        kernel(*ins, *outs, *scratch)
        for src, dst in zip(cast_in, cast_out):
            dst[...] = src[...].astype(dst.dtype)
    return wrapped


def _cast_specs(weights, grid):
    n_steps = 1
    for n in grid:
        n_steps *= n

    def step(*idx):
        s = 0
        for i, n in zip(idx, grid):
            s = s * n + i
        return s

    specs, shapes = [], []
    for w, n_cols in weights:
        rows = w.shape[0] // n_steps
        assert rows * n_steps == w.shape[0] and rows % BF16_ROWS == 0 and n_cols % LANES == 0
        specs.append(pl.BlockSpec((rows, n_cols), lambda *idx: (step(*idx), 0)))
        shapes.append(jax.ShapeDtypeStruct((w.shape[0], n_cols), BF16))
    return specs, shapes


def _layer_norm(z, g, b):
    mu = jnp.mean(z, axis=-1, keepdims=True)
    zc = z - mu
    var = jnp.mean(zc * zc, axis=-1, keepdims=True)
    return zc * lax.rsqrt(var + LN_EPS) * g + b


def _norm_rope(yh, gain, cos, sin):
    mean_w = jnp.full((2 * HEAD_DIM, HEAD_DIM), 1.0 / HEAD_DIM, BF16)
    sq = yh * yh
    hi = sq.astype(BF16)
    lo = (sq - hi.astype(F32)).astype(BF16)
    ms = jnp.dot(jnp.concatenate([hi, lo], axis=1), mean_w, preferred_element_type=F32)
    yn = yh * lax.rsqrt(ms + RMS_EPS) * gain
    half = ROPE_AXIS_DIM // 2
    lane = lax.broadcasted_iota(jnp.int32, yn.shape, 1)
    partner = jnp.where(lane % ROPE_AXIS_DIM < half, pltpu.roll(yn, HEAD_DIM - half, 1), pltpu.roll(yn, half, 1))
    return yn * cos + partner * sin


def _proj_a_kernel(x_ref, w_ref, gain_ref, cos_ref, sin_ref, q_ref, kv_ref):
    y = jnp.dot(x_ref[...], w_ref[...], preferred_element_type=F32)
    cos = cos_ref[...]
    sin = sin_ref[...]
    head = lambda c: slice(c * HEAD_DIM, (c + 1) * HEAD_DIM)
    for h in range(A_Q_HEADS):
        q_ref[:, head(h)] = _norm_rope(y[:, head(h)], gain_ref[:, head(h)], cos, sin).astype(q_ref.dtype)
    for h in range(A_KV_HEADS):
        c = A_Q_HEADS + h
        kv_ref[h] = _norm_rope(y[:, head(c)], gain_ref[:, head(c)], cos, sin).astype(kv_ref.dtype)
    for h in range(A_KV_HEADS):
        kv_ref[A_KV_HEADS + h] = y[:, head(A_Q_HEADS + A_KV_HEADS + h)].astype(kv_ref.dtype)


def _proj_a(xb, w, gain, rope, *, tm, seq_len):
    t, d = xb.shape
    n_cols = A_Q_W + 2 * A_KV_W
    pos_blocks = seq_len // tm
    tab = pl.BlockSpec((tm, HEAD_DIM), lambda i: (i % pos_blocks, 0))
    return pl.pallas_call(
        _proj_a_kernel,
        grid=(t // tm,),
        in_specs=[pl.BlockSpec((tm, d), lambda i: (i, 0)),
                  pl.BlockSpec((d, n_cols), lambda i: (0, COL_QA // n_cols)),
                  pl.BlockSpec((1, A_Q_W + A_KV_W), lambda i: (0, 0)), tab, tab],
        out_specs=[pl.BlockSpec((tm, A_Q_W), lambda i: (i, 0)),
                   pl.BlockSpec((2 * A_KV_HEADS, tm, HEAD_DIM), lambda i: (0, i, 0))],
        out_shape=[jax.ShapeDtypeStruct((t, A_Q_W), BF16), jax.ShapeDtypeStruct((2 * A_KV_HEADS, t, HEAD_DIM), BF16)],
        compiler_params=_params("parallel"),
    )(xb, w, gain, *rope)


def _proj_gate_kernel(x_ref, w_ref, b_ref, o_ref, xb_ref):
    @pl.when(pl.program_id(1) == 0)
    def _():
        xb_ref[...] = x_ref[...].astype(xb_ref.dtype)

    y = jnp.dot(xb_ref[...], w_ref[...], preferred_element_type=F32)
    o_ref[...] = (0.5 * jnp.tanh(0.5 * (y + b_ref[...])) + 0.5).astype(o_ref.dtype)


def _proj_gate(x2d, w, bias, casts, *, tm, tn):
    t, d = x2d.shape
    n_cols = w.shape[1]
    grid = (t // tm, n_cols // tn)
    cast_specs, cast_shapes = _cast_specs(casts, grid)
    return pl.pallas_call(
        _with_casts(_proj_gate_kernel, 3, 2, len(casts)),
        grid=grid,
        in_specs=[pl.BlockSpec((tm, d), lambda i, j: (i, 0)),
                  pl.BlockSpec((d, tn), lambda i, j: (0, j)),
                  pl.BlockSpec((1, tn), lambda i, j: (0, j))] + cast_specs,
        out_specs=[pl.BlockSpec((tm, tn), lambda i, j: (i, j)), pl.BlockSpec((tm, d), lambda i, j: (i, 0))] + cast_specs,
        out_shape=[jax.ShapeDtypeStruct((t, n_cols), BF16), jax.ShapeDtypeStruct((t, d), BF16)] + cast_shapes,
        compiler_params=_params("parallel", "arbitrary"),
    )(x2d, w, bias, *[w_f32 for w_f32, _ in casts])


def _gqa_kernel(q_ref, k_ref, v_ref, o_ref, m_sc, l_sc, acc_sc, s0_sc, s1_sc, *, tk):
    tq = q_ref.shape[1]
    n_chunks = k_ref.shape[1] // tk
    s_sc = (s0_sc, s1_sc)
    q = jnp.concatenate([q_ref[0, :, r * HEAD_DIM:(r + 1) * HEAD_DIM] for r in range(A_REP)], axis=0)
    m_sc[...] = jnp.full(m_sc.shape, -jnp.inf, F32)
    l_sc[...] = jnp.zeros(l_sc.shape, F32)
    acc_sc[...] = jnp.zeros(acc_sc.shape, F32)

    def scores(c, slot):
        s_sc[slot][...] = lax.dot_general(k_ref[0, c * tk:(c + 1) * tk, :], q, NT_DIMS, preferred_element_type=F32)

    def softmax_pv(c, slot):
        v = v_ref[0, c * tk:(c + 1) * tk, :]
        st = s_sc[slot][...]
        m_prev = m_sc[...]
        m_new = jnp.maximum(m_prev, jnp.max(st, axis=0, keepdims=True))
        alpha = jnp.exp2(m_prev - m_new)
        pt = jnp.exp2(st - m_new)
        l_sc[...] = alpha * l_sc[...] + jnp.sum(pt, axis=0, keepdims=True)
        pv = lax.dot_general(v, pt.astype(BF16), TN_DIMS, preferred_element_type=F32)
        acc_sc[...] = alpha * acc_sc[...] + pv
        m_sc[...] = m_new

    scores(0, 0)
    for c in range(n_chunks - 1):
        scores(c + 1, (c + 1) % 2)
        softmax_pv(c, c % 2)
    softmax_pv(n_chunks - 1, (n_chunks - 1) % 2)
    ot = acc_sc[...] / l_sc[...]
    for r in range(A_REP):
        o_ref[0, :, r * HEAD_DIM:(r + 1) * HEAD_DIM] = ot[:, r * tq:(r + 1) * tq].T.astype(o_ref.dtype)


def _gqa(qa, kv, casts, *, tq, tk):
    b, s, _ = qa.shape
    assert s % tk == 0 and s % tq == 0
    gw = A_REP * HEAD_DIM
    rows = A_REP * tq
    grid = (b, A_KV_HEADS, s // tq)
    cast_specs, cast_shapes = _cast_specs(casts, grid)
    return pl.pallas_call(
        _with_casts(functools.partial(_gqa_kernel, tk=tk), 3, 1, len(casts)),
        grid=grid,
        in_specs=[pl.BlockSpec((1, tq, gw), lambda bi, g, i: (bi, i, g)),
                  pl.BlockSpec((None, 1, s, HEAD_DIM), lambda bi, g, i: (g, bi, 0, 0)),
                  pl.BlockSpec((None, 1, s, HEAD_DIM), lambda bi, g, i: (A_KV_HEADS + g, bi, 0, 0))] + cast_specs,
        out_specs=[pl.BlockSpec((1, tq, gw), lambda bi, g, i: (bi, i, g))] + cast_specs,
        out_shape=[jax.ShapeDtypeStruct((b, s, A_Q_W), BF16)] + cast_shapes,
        scratch_shapes=[pltpu.VMEM((1, rows), F32), pltpu.VMEM((1, rows), F32), pltpu.VMEM((HEAD_DIM, rows), F32),
                        pltpu.VMEM((tk, rows), F32), pltpu.VMEM((tk, rows), F32)],
        compiler_params=_params("parallel", "parallel", "arbitrary"),
    )(qa, kv, kv, *[w_f32 for w_f32, _ in casts])


def _proj_deint_kernel(x_ref, wq_ref, wk_ref, wv_ref, o_ref, *stage, dil):
    x = x_ref[...]
    tm = x.shape[0]
    for which, w_ref in enumerate((wq_ref, wk_ref, wv_ref)):
        y = jnp.dot(x, w_ref[...], preferred_element_type=F32)
        if which == 0:
            y = y * Q_SCALE
        tn = y.shape[1]
        if dil == 1:
            o_ref[0, 0, :, which * tn:(which + 1) * tn] = y.astype(o_ref.dtype)
            continue
        y_sc = stage[which]
        n_slabs = tn // LANES
        for s in range(n_slabs):
            y_sc[s] = y[:, s * LANES:(s + 1) * LANES]
        if dil == FAST_STRIDE ** 2:
            z_sc = stage[3 + which]
            part = tm // FAST_STRIDE
            for lo in range(FAST_STRIDE):
                for s in range(n_slabs):
                    z_sc[s, lo * part:(lo + 1) * part, :] = y_sc[s, pl.ds(lo, part, stride=FAST_STRIDE), :]
            for lo in range(FAST_STRIDE):
                for hi in range(FAST_STRIDE):
                    for s in range(n_slabs):
                        c0 = which * tn + s * LANES
                        rows = z_sc[s, pl.ds(lo * part + hi, tm // dil, stride=FAST_STRIDE), :]
                        o_ref[0, lo + FAST_STRIDE * hi, :, c0:c0 + LANES] = rows.astype(o_ref.dtype)
            continue
        for r in range(dil):
            for s in range(n_slabs):
                c0 = which * tn + s * LANES
                o_ref[0, r, :, c0:c0 + LANES] = y_sc[s, pl.ds(r, tm // dil, stride=dil), :].astype(o_ref.dtype)


def _proj_deint(xb, w, g, casts=(), *, tm, bsz, seq_len):
    t, d = xb.shape
    dil = B_DILATIONS[g]
    tn = B_OUT_W
    tiles_per_seq = seq_len // tm
    col0 = COL_QB // tn
    w_spec = lambda which: pl.BlockSpec((d, tn), lambda i: (0, col0 + B_GROUPS * which + g))
    grid = (t // tm,)
    cast_specs, cast_shapes = _cast_specs(casts, grid)
    out_spec = pl.BlockSpec((1, dil, tm // dil, 3 * tn), lambda i: (i // tiles_per_seq, 0, i % tiles_per_seq, 0))
    qkv, *cast_out = pl.pallas_call(
        _with_casts(functools.partial(_proj_deint_kernel, dil=dil), 4, 1, len(casts)),
        grid=grid,
        in_specs=[pl.BlockSpec((tm, d), lambda i: (i, 0)), w_spec(0), w_spec(1), w_spec(2)] + cast_specs,
        out_specs=[out_spec] + cast_specs,
        out_shape=[jax.ShapeDtypeStruct((bsz, dil, seq_len // dil, 3 * tn), BF16)] + cast_shapes,
        scratch_shapes=[pltpu.VMEM((tn // LANES, tm, LANES), F32)] * {1: 0, FAST_STRIDE ** 2: 6}.get(dil, 3),
        compiler_params=_params("parallel"),
    )(xb, w, w, w, *[w_f32 for w_f32, _ in casts])
    return qkv, cast_out


def _dilated_kernel(q_ref, k_ref, v_ref, o_ref, lse_ref, *token_order_scratch, dil, slopes, n_blocks, n_res):
    qb = 2 * B_SIDE
    kw = 4 * B_SIDE
    n_rows = k_ref.shape[2]
    row = lax.broadcasted_iota(jnp.int32, (qb, kw), 0)
    col = lax.broadcasted_iota(jnp.int32, (qb, kw), 1)

    def attend(r, blk, h, ks, valid, token_dist):
        sl = slice(h * HEAD_DIM, (h + 1) * HEAD_DIM)
        q = q_ref[0, r, blk * qb:(blk + 1) * qb, sl]
        k = k_ref[0, r, pl.ds(ks, kw), sl]
        v = v_ref[0, r, pl.ds(ks, kw), sl]
        s = lax.dot_general(q, k, NT_DIMS, preferred_element_type=F32) - slopes[h] * token_dist
        s = jnp.where(valid, s, -jnp.inf)
        m = jnp.max(s, axis=-1, keepdims=True)
        p = jnp.exp(s - m)
        l = jnp.sum(p, axis=-1, keepdims=True)
        o = jnp.dot(p.astype(BF16), v, preferred_element_type=F32) / l
        if dil == 1:
            o_ref[0, blk * qb:(blk + 1) * qb, sl] = o.astype(o_ref.dtype)
        else:
            o_sc, _ = token_order_scratch
            o_sc[h, pl.ds(blk * qb * dil + r, qb, stride=dil), :] = o
        return m + jnp.log(l)

    lane = lax.broadcasted_iota(jnp.int32, (qb, LANES), 1)

    def residues(it, carry):
        for blk in range(n_blocks):
            q0 = (pl.program_id(1) * n_blocks + blk) * qb
            ks = pl.multiple_of(jnp.clip(q0 - B_SIDE, 0, n_rows - kw), B_SIDE)
            dist = jnp.abs(col - row + (ks - q0))
            valid = dist <= B_SIDE
            token_dist = dist.astype(F32) * float(dil)
            for u in range(n_res):
                r = it * n_res + u
                lse = None
                for h in reversed(range(B_HEADS_PER_GROUP)):
                    col_h = jnp.broadcast_to(attend(r, blk, h, ks, valid, token_dist), (qb, LANES))
                    lse = col_h if lse is None else jnp.where(lane < (h + 1) * LSE_LANES, col_h, lse)
                if dil == 1:
                    lse_ref[0, blk * qb:(blk + 1) * qb, :] = lse
                else:
                    _, l_sc = token_order_scratch
                    l_sc[pl.ds(blk * qb * dil + r, qb, stride=dil), :] = lse
        return carry

    if dil == 1:
        residues(0, 0)
    else:
        o_sc, l_sc = token_order_scratch
        lax.fori_loop(0, dil // n_res, residues, 0)
        for h in range(B_HEADS_PER_GROUP):
            sl = slice(h * HEAD_DIM, (h + 1) * HEAD_DIM)
            o_ref[0, :, sl] = o_sc[h].astype(o_ref.dtype)
        lse_ref[0] = l_sc[...]


DILATED_BATCHING = ((8, 1), (8, 1), (1, 8))


def _dilated_group(qkv, g):
    b, dil, n_rows, _ = qkv.shape
    s = n_rows * dil
    n_blocks, n_res = DILATED_BATCHING[g]
    rows = n_blocks * 2 * B_SIDE
    assert n_rows >= 4 * B_SIDE and n_rows % rows == 0 and dil % n_res == 0
    slopes = tuple(2.0 ** (-8.0 * (g * B_HEADS_PER_GROUP + h + 1) / B_HEADS) for h in range(B_HEADS_PER_GROUP))
    kv_spec = lambda which: pl.BlockSpec((1, dil, n_rows, B_OUT_W), lambda bi, i: (bi, 0, 0, which))
    out_spec = lambda w: pl.BlockSpec((1, rows * dil, w), lambda bi, i: (bi, i, 0))
    staging = [pltpu.VMEM((B_HEADS_PER_GROUP, rows * dil, LANES), F32), pltpu.VMEM((rows * dil, LANES), F32)]
    o, lse = pl.pallas_call(
        functools.partial(_dilated_kernel, dil=dil, slopes=slopes, n_blocks=n_blocks, n_res=n_res),
        grid=(b, n_rows // rows),
        in_specs=[pl.BlockSpec((1, dil, rows, B_OUT_W), lambda bi, i: (bi, 0, i, 0)), kv_spec(1), kv_spec(2)],
        out_specs=[out_spec(B_OUT_W), out_spec(LANES)],
        out_shape=[jax.ShapeDtypeStruct((b, s, B_OUT_W), BF16), jax.ShapeDtypeStruct((b, s, LANES), F32)],
        scratch_shapes=staging if dil > 1 else [],
        compiler_params=_params("parallel", "arbitrary"),
    )(qkv, qkv, qkv)
    return o.reshape(b * s, B_OUT_W), lse.reshape(b * s, LANES)


def _mix_out_kernel(oa_ref, o0_ref, o1_ref, o2_ref, l0_ref, l1_ref, l2_ref, ga_ref, gb_ref, x_ref,
                    wpa_ref, wpb_ref, wo_ref, g_ref, b_ref, out_ref):
    l0, l1, l2 = l0_ref[...], l1_ref[...], l2_ref[...]
    mx = jnp.maximum(jnp.maximum(l0, l1), l2)
    e0, e1, e2 = jnp.exp(l0 - mx), jnp.exp(l1 - mx), jnp.exp(l2 - mx)
    inv = 1.0 / (e0 + e1 + e2)
    weights = (e0 * inv, e1 * inv, e2 * inv)
    rows = l0.shape[0]
    merged = []
    for h in range(B_HEADS_PER_GROUP):
        sl = slice(h * HEAD_DIM, (h + 1) * HEAD_DIM)
        acc = None
        for w, o_ref in zip(weights, (o0_ref, o1_ref, o2_ref)):
            term = jnp.broadcast_to(w[:, h * LSE_LANES:h * LSE_LANES + 1], (rows, HEAD_DIM)) * o_ref[:, sl]
            acc = term if acc is None else acc + term
        merged.append(acc.astype(BF16))
    ob = jnp.concatenate(merged, axis=1)
    pa = jnp.dot(oa_ref[...], wpa_ref[...], preferred_element_type=F32)
    pb = jnp.dot(ob, wpb_ref[...], preferred_element_type=F32)
    mixed = (ga_ref[...].astype(F32) * pa + gb_ref[...].astype(F32) * pb).astype(BF16)
    y = jnp.dot(mixed, wo_ref[...], preferred_element_type=F32)
    out_ref[...] = _layer_norm(ALPHA * x_ref[...] + y, g_ref[...], b_ref[...])


def _mix_out(out_a, o_groups, lse_groups, gates, x2d, wpa, wpb, wo, g, b, *, tm):
    t = out_a.shape[0]
    row = lambda w, cb=0: pl.BlockSpec((tm, w), lambda i: (i, cb))
    const = lambda a: pl.BlockSpec(a.shape, lambda i: (0, 0), pipeline_mode=pl.Buffered(1))
    return pl.pallas_call(
        _mix_out_kernel,
        grid=(t // tm,),
        in_specs=[row(A_Q_W)] + [row(B_OUT_W)] * 3 + [row(LANES)] * 3 + [row(D_MODEL, 0), row(D_MODEL, 1), row(D_MODEL),
                  const(wpa), const(wpb), const(wo), const(g), const(b)],
        out_specs=row(D_MODEL),
        out_shape=jax.ShapeDtypeStruct((t, D_MODEL), F32),
        compiler_params=_params("parallel"),
    )(out_a, *o_groups, *lse_groups, gates, gates, x2d, wpa, wpb, wo, g, b)


def _ffn_up_kernel(x_ref, wg_ref, wu_ref, h_ref, xb_sc):
    @pl.when(pl.program_id(1) == 0)
    def _():
        xb_sc[...] = x_ref[...].astype(BF16)

    xb = xb_sc[...]
    gate = jnp.dot(xb, wg_ref[...], preferred_element_type=F32)
    up = jnp.dot(xb, wu_ref[...], preferred_element_type=F32)
    h_ref[...] = (gate * (0.5 * jnp.tanh(0.5 * gate) + 0.5) * up).astype(h_ref.dtype)


def _ffn_down_kernel(h_ref, x_ref, wd_ref, g_ref, b_ref, o_ref):
    y = jnp.dot(h_ref[...], wd_ref[...], preferred_element_type=F32)
    o_ref[...] = _layer_norm(ALPHA * x_ref[...] + y, g_ref[...], b_ref[...])


def _ffn(x1, wg, wu, wd, g, b, *, tm_up, tf, tm_down):
    t, d = x1.shape
    hidden = wg.shape[1]
    h = pl.pallas_call(
        _ffn_up_kernel,
        grid=(t // tm_up, hidden // tf),
        in_specs=[pl.BlockSpec((tm_up, d), lambda i, f: (i, 0)),
                  pl.BlockSpec((d, tf), lambda i, f: (0, f)),
                  pl.BlockSpec((d, tf), lambda i, f: (0, f))],
        out_specs=pl.BlockSpec((tm_up, tf), lambda i, f: (i, f)),
        out_shape=jax.ShapeDtypeStruct((t, hidden), BF16),
        scratch_shapes=[pltpu.VMEM((tm_up, d), BF16)],
        compiler_params=_params("parallel", "arbitrary"),
    )(x1, wg, wu)
    const = lambda a: pl.BlockSpec(a.shape, lambda i: (0, 0), pipeline_mode=pl.Buffered(1))
    return pl.pallas_call(
        _ffn_down_kernel,
        grid=(t // tm_down,),
        in_specs=[pl.BlockSpec((tm_down, hidden), lambda i: (i, 0)),
                  pl.BlockSpec((tm_down, d), lambda i: (i, 0)),
                  const(wd), const(g), const(b)],
        out_specs=pl.BlockSpec((tm_down, d), lambda i: (i, 0)),
        out_shape=jax.ShapeDtypeStruct((t, d), F32),
        compiler_params=_params("parallel"),
    )(h, x1, wd, g, b)


def _rope_tables(seq_len):
    pos = np.arange(seq_len)
    n_freq = ROPE_AXIS_DIM // 2
    freqs = (np.float32(ROPE_THETA) ** (-np.arange(n_freq, dtype=np.float32) / np.float32(n_freq))).astype(np.float32)
    ang_r = (pos // GRID_W).astype(np.float32)[:, None] * freqs[None, :]
    ang_c = (pos % GRID_W).astype(np.float32)[:, None] * freqs[None, :]
    cos = np.concatenate([np.cos(ang_r)] * 2 + [np.cos(ang_c)] * 2, axis=-1).astype(np.float32)
    sin = np.concatenate([-np.sin(ang_r), np.sin(ang_r), -np.sin(ang_c), np.sin(ang_c)], axis=-1).astype(np.float32)
    return jnp.asarray(cos), jnp.asarray(sin)


def _layer(x, w_in, b_gate, q_norm_a, k_norm_a, w_proj_a, w_proj_b, w_out,
           ln1_g, ln1_b, w_ffn_gate, w_ffn_up, w_ffn_down, ln2_g, ln2_b, rope):
    bsz, seq_len, d = x.shape
    t = bsz * seq_len
    x2d = x.reshape(t, d)
    vec = lambda a: a.reshape(1, -1).astype(F32)

    gates, xb, w_in_b = _proj_gate(x2d, w_in[:, COL_GA:].astype(BF16), vec(b_gate), [(w_in, COL_GA)], tm=1024, tn=1024)

    qk_gain = jnp.concatenate([jnp.tile(q_norm_a * (Q_SCALE * LOG2_E), A_Q_HEADS), jnp.tile(k_norm_a, A_KV_HEADS)])
    qa, kv = _proj_a(xb, w_in_b, vec(qk_gain), rope, tm=1024, seq_len=seq_len)

    out_a, w_gate_b, w_up_b = _gqa(qa.reshape(bsz, seq_len, A_Q_W), kv.reshape(2 * A_KV_HEADS, bsz, seq_len, HEAD_DIM),
                                   [(w_ffn_gate, FFN_HIDDEN), (w_ffn_up, FFN_HIDDEN)], tq=256, tk=1024)
    casts = [[(w_ffn_down, D_MODEL)], [(w_proj_a, D_MODEL), (w_proj_b, D_MODEL), (w_out, D_MODEL)], []]
    groups, cast_out = [], []
    for g in range(B_GROUPS):
        qkv, converted = _proj_deint(xb, w_in_b, g, casts[g], tm=1024, bsz=bsz, seq_len=seq_len)
        groups.append(_dilated_group(qkv, g))
        cast_out += converted
    w_down_b, w_proj_a_b, w_proj_b_b, w_out_b = cast_out
    x1 = _mix_out(out_a.reshape(t, A_Q_W), [o for o, _ in groups], [l for _, l in groups], gates, x2d,
                  w_proj_a_b, w_proj_b_b, w_out_b, vec(ln1_g), vec(ln1_b), tm=512)
    x2 = _ffn(x1, w_gate_b, w_up_b, w_down_b, vec(ln2_g), vec(ln2_b), tm_up=1024, tf=512, tm_down=512)
    return x2.reshape(bsz, seq_len, d)


def kernel(x, w_in, b_gate, q_norm_a, k_norm_a, w_proj_a, w_proj_b, w_out, ln1_g, ln1_b, w_ffn_gate, w_ffn_up, w_ffn_down, ln2_g, ln2_b):
    rope = _rope_tables(x.shape[1])
    for l in range(w_in.shape[0]):
        x = _layer(x, w_in[l], b_gate[l], q_norm_a[l], k_norm_a[l], w_proj_a[l], w_proj_b[l], w_out[l],
                   ln1_g[l], ln1_b[l], w_ffn_gate[l], w_ffn_up[l], w_ffn_down[l], ln2_g[l], ln2_b[l], rope)
    return x
```

```python
import functools

import jax
import jax.numpy as jnp
import numpy as np
from jax import lax
from jax.experimental import pallas as pl
from jax.experimental.pallas import tpu as pltpu

F32 = jnp.float32
BF16 = jnp.bfloat16

D_MODEL = 2048
HEAD_DIM = 128
A_Q_HEADS = 8
A_KV_HEADS = 2
A_REP = A_Q_HEADS // A_KV_HEADS
ROPE_THETA = 10000.0
ROPE_AXIS_DIM = HEAD_DIM // 2
GRID_W = 64
B_DILATIONS = (1, 4, 16)
B_GROUPS = 3
B_HEADS_PER_GROUP = 4
B_HEADS = B_GROUPS * B_HEADS_PER_GROUP
B_SIDE = 64
A_Q_W = A_Q_HEADS * HEAD_DIM
A_KV_W = A_KV_HEADS * HEAD_DIM
B_W = B_HEADS * HEAD_DIM
B_OUT_W = B_HEADS_PER_GROUP * HEAD_DIM
LSE_LANES = HEAD_DIM // B_HEADS_PER_GROUP
FFN_HIDDEN = 5632
ALPHA = 2.0 ** 0.25
RMS_EPS = 1e-6
LN_EPS = 1e-5
Q_SCALE = HEAD_DIM ** -0.5

COL_QA = 0
COL_KA = A_Q_W
COL_VA = A_Q_W + A_KV_W
COL_QB = COL_VA + A_KV_W
COL_GA = COL_QB + 3 * B_W

VMEM_LIMIT = 60 * 1024 * 1024

NT_DIMS = (((1,), (1,)), ((), ()))
TN_DIMS = (((0,), (0,)), ((), ()))
LANES = 128
BF16_ROWS = 16
FAST_STRIDE = 4
LOG2_E = 1.4426950408889634


def _params(*sem):
    return pltpu.CompilerParams(dimension_semantics=sem, vmem_limit_bytes=VMEM_LIMIT)


def _with_casts(kernel, n_in, n_out, n_casts):
    def wrapped(*refs):
        ins, rest = refs[:n_in], refs[n_in:]
        cast_in, rest = rest[:n_casts], rest[n_casts:]
        outs, rest = rest[:n_out], rest[n_out:]
        cast_out, scratch = rest[:n_casts], rest[n_casts:]
        kernel(*ins, *outs, *scratch)
        for src, dst in zip(cast_in, cast_out):
            dst[...] = src[...].astype(dst.dtype)
    return wrapped


def _cast_specs(weights, grid):
    n_steps = 1
    for n in grid:
        n_steps *= n

    def step(*idx):
        s = 0
        for i, n in zip(idx, grid):
            s = s * n + i
        return s

    specs, shapes = [], []
    for w, n_cols in weights:
        rows = w.shape[0] // n_steps
        assert rows * n_steps == w.shape[0] and rows % BF16_ROWS == 0 and n_cols % LANES == 0
        specs.append(pl.BlockSpec((rows, n_cols), lambda *idx: (step(*idx), 0)))
        shapes.append(jax.ShapeDtypeStruct((w.shape[0], n_cols), BF16))
    return specs, shapes


def _layer_norm(z, g, b):
    mu = jnp.mean(z, axis=-1, keepdims=True)
    zc = z - mu
    var = jnp.mean(zc * zc, axis=-1, keepdims=True)
    return zc * lax.rsqrt(var + LN_EPS) * g + b


def _norm_rope(yh, gain, cos, sin):
    mean_w = jnp.full((2 * HEAD_DIM, HEAD_DIM), 1.0 / HEAD_DIM, BF16)
    sq = yh * yh
    hi = sq.astype(BF16)
    lo = (sq - hi.astype(F32)).astype(BF16)
    ms = jnp.dot(jnp.concatenate([hi, lo], axis=1), mean_w, preferred_element_type=F32)
    yn = yh * lax.rsqrt(ms + RMS_EPS) * gain
    half = ROPE_AXIS_DIM // 2
    lane = lax.broadcasted_iota(jnp.int32, yn.shape, 1)
    partner = jnp.where(lane % ROPE_AXIS_DIM < half, pltpu.roll(yn, HEAD_DIM - half, 1), pltpu.roll(yn, half, 1))
    return yn * cos + partner * sin


def _proj_a_kernel(x_ref, w_ref, gain_ref, cos_ref, sin_ref, q_ref, kv_ref):
    y = jnp.dot(x_ref[...], w_ref[...], preferred_element_type=F32)
    cos = cos_ref[...]
    sin = sin_ref[...]
    head = lambda c: slice(c * HEAD_DIM, (c + 1) * HEAD_DIM)
    for h in range(A_Q_HEADS):
        q_ref[:, head(h)] = _norm_rope(y[:, head(h)], gain_ref[:, head(h)], cos, sin).astype(q_ref.dtype)
    for h in range(A_KV_HEADS):
        c = A_Q_HEADS + h
        kv_ref[h] = _norm_rope(y[:, head(c)], gain_ref[:, head(c)], cos, sin).astype(kv_ref.dtype)
    for h in range(A_KV_HEADS):
        kv_ref[A_KV_HEADS + h] = y[:, head(A_Q_HEADS + A_KV_HEADS + h)].astype(kv_ref.dtype)


def _proj_a(xb, w, gain, rope, *, tm, seq_len):
    t, d = xb.shape
    n_cols = A_Q_W + 2 * A_KV_W
    pos_blocks = seq_len // tm
    tab = pl.BlockSpec((tm, HEAD_DIM), lambda i: (i % pos_blocks, 0))
    return pl.pallas_call(
        _proj_a_kernel,
        grid=(t // tm,),
        in_specs=[pl.BlockSpec((tm, d), lambda i: (i, 0)),
                  pl.BlockSpec((d, n_cols), lambda i: (0, COL_QA // n_cols)),
                  pl.BlockSpec((1, A_Q_W + A_KV_W), lambda i: (0, 0)), tab, tab],
        out_specs=[pl.BlockSpec((tm, A_Q_W), lambda i: (i, 0)),
                   pl.BlockSpec((2 * A_KV_HEADS, tm, HEAD_DIM), lambda i: (0, i, 0))],
        out_shape=[jax.ShapeDtypeStruct((t, A_Q_W), BF16), jax.ShapeDtypeStruct((2 * A_KV_HEADS, t, HEAD_DIM), BF16)],
        compiler_params=_params("parallel"),
    )(xb, w, gain, *rope)


def _proj_gate_kernel(x_ref, w_ref, b_ref, o_ref, xb_ref):
    @pl.when(pl.program_id(1) == 0)
    def _():
        xb_ref[...] = x_ref[...].astype(xb_ref.dtype)

    y = jnp.dot(xb_ref[...], w_ref[...], preferred_element_type=F32)
    o_ref[...] = (0.5 * jnp.tanh(0.5 * (y + b_ref[...])) + 0.5).astype(o_ref.dtype)


def _proj_gate(x2d, w, bias, casts, *, tm, tn):
    t, d = x2d.shape
    n_cols = w.shape[1]
    grid = (t // tm, n_cols // tn)
    cast_specs, cast_shapes = _cast_specs(casts, grid)
    return pl.pallas_call(
        _with_casts(_proj_gate_kernel, 3, 2, len(casts)),
        grid=grid,
        in_specs=[pl.BlockSpec((tm, d), lambda i, j: (i, 0)),
                  pl.BlockSpec((d, tn), lambda i, j: (0, j)),
                  pl.BlockSpec((1, tn), lambda i, j: (0, j))] + cast_specs,
        out_specs=[pl.BlockSpec((tm, tn), lambda i, j: (i, j)), pl.BlockSpec((tm, d), lambda i, j: (i, 0))] + cast_specs,
        out_shape=[jax.ShapeDtypeStruct((t, n_cols), BF16), jax.ShapeDtypeStruct((t, d), BF16)] + cast_shapes,
        compiler_params=_params("parallel", "arbitrary"),
    )(x2d, w, bias, *[w_f32 for w_f32, _ in casts])


def _gqa_kernel(q_ref, k_ref, v_ref, o_ref, m_sc, l_sc, acc_sc, s0_sc, s1_sc, *, tk):
    tq = q_ref.shape[1]
    n_chunks = k_ref.shape[1] // tk
    s_sc = (s0_sc, s1_sc)
    q = jnp.concatenate([q_ref[0, :, r * HEAD_DIM:(r + 1) * HEAD_DIM] for r in range(A_REP)], axis=0)
    m_sc[...] = jnp.full(m_sc.shape, -jnp.inf, F32)
    l_sc[...] = jnp.zeros(l_sc.shape, F32)
    acc_sc[...] = jnp.zeros(acc_sc.shape, F32)

    def scores(c, slot):
        s_sc[slot][...] = lax.dot_general(k_ref[0, c * tk:(c + 1) * tk, :], q, NT_DIMS, preferred_element_type=F32)

    def softmax_pv(c, slot):
        v = v_ref[0, c * tk:(c + 1) * tk, :]
        st = s_sc[slot][...]
        m_prev = m_sc[...]
        m_new = jnp.maximum(m_prev, jnp.max(st, axis=0, keepdims=True))
        alpha = jnp.exp2(m_prev - m_new)
        pt = jnp.exp2(st - m_new)
        l_sc[...] = alpha * l_sc[...] + jnp.sum(pt, axis=0, keepdims=True)
        pv = lax.dot_general(v, pt.astype(BF16), TN_DIMS, preferred_element_type=F32)
        acc_sc[...] = alpha * acc_sc[...] + pv
        m_sc[...] = m_new

    scores(0, 0)
    for c in range(n_chunks - 1):
        scores(c + 1, (c + 1) % 2)
        softmax_pv(c, c % 2)
    softmax_pv(n_chunks - 1, (n_chunks - 1) % 2)
    ot = acc_sc[...] / l_sc[...]
    for r in range(A_REP):
        o_ref[0, :, r * HEAD_DIM:(r + 1) * HEAD_DIM] = ot[:, r * tq:(r + 1) * tq].T.astype(o_ref.dtype)


def _gqa(qa, kv, casts, *, tq, tk):
    b, s, _ = qa.shape
    assert s % tk == 0 and s % tq == 0
    gw = A_REP * HEAD_DIM
    rows = A_REP * tq
    grid = (b, A_KV_HEADS, s // tq)
    cast_specs, cast_shapes = _cast_specs(casts, grid)
    return pl.pallas_call(
        _with_casts(functools.partial(_gqa_kernel, tk=tk), 3, 1, len(casts)),
        grid=grid,
        in_specs=[pl.BlockSpec((1, tq, gw), lambda bi, g, i: (bi, i, g)),
                  pl.BlockSpec((None, 1, s, HEAD_DIM), lambda bi, g, i: (g, bi, 0, 0)),
                  pl.BlockSpec((None, 1, s, HEAD_DIM), lambda bi, g, i: (A_KV_HEADS + g, bi, 0, 0))] + cast_specs,
        out_specs=[pl.BlockSpec((1, tq, gw), lambda bi, g, i: (bi, i, g))] + cast_specs,
        out_shape=[jax.ShapeDtypeStruct((b, s, A_Q_W), BF16)] + cast_shapes,
        scratch_shapes=[pltpu.VMEM((1, rows), F32), pltpu.VMEM((1, rows), F32), pltpu.VMEM((HEAD_DIM, rows), F32),
                        pltpu.VMEM((tk, rows), F32), pltpu.VMEM((tk, rows), F32)],
        compiler_params=_params("parallel", "parallel", "arbitrary"),
    )(qa, kv, kv, *[w_f32 for w_f32, _ in casts])


def _proj_deint_kernel(x_ref, wq_ref, wk_ref, wv_ref, o_ref, *stage, dil):
    x = x_ref[...]
    tm = x.shape[0]
    for which, w_ref in enumerate((wq_ref, wk_ref, wv_ref)):
        y = jnp.dot(x, w_ref[...], preferred_element_type=F32)
        if which == 0:
            y = y * Q_SCALE
        tn = y.shape[1]
        if dil == 1:
            o_ref[0, 0, :, which * tn:(which + 1) * tn] = y.astype(o_ref.dtype)
            continue
        y_sc = stage[which]
        n_slabs = tn // LANES
        for s in range(n_slabs):
            y_sc[s] = y[:, s * LANES:(s + 1) * LANES]
        if dil == FAST_STRIDE ** 2:
            z_sc = stage[3 + which]
            part = tm // FAST_STRIDE
            for lo in range(FAST_STRIDE):
                for s in range(n_slabs):
                    z_sc[s, lo * part:(lo + 1) * part, :] = y_sc[s, pl.ds(lo, part, stride=FAST_STRIDE), :]
            for lo in range(FAST_STRIDE):
                for hi in range(FAST_STRIDE):
                    for s in range(n_slabs):
                        c0 = which * tn + s * LANES
                        rows = z_sc[s, pl.ds(lo * part + hi, tm // dil, stride=FAST_STRIDE), :]
                        o_ref[0, lo + FAST_STRIDE * hi, :, c0:c0 + LANES] = rows.astype(o_ref.dtype)
            continue
        for r in range(dil):
            for s in range(n_slabs):
                c0 = which * tn + s * LANES
                o_ref[0, r, :, c0:c0 + LANES] = y_sc[s, pl.ds(r, tm // dil, stride=dil), :].astype(o_ref.dtype)


def _proj_deint(xb, w, g, casts=(), *, tm, bsz, seq_len):
    t, d = xb.shape
    dil = B_DILATIONS[g]
    tn = B_OUT_W
    tiles_per_seq = seq_len // tm
    col0 = COL_QB // tn
    w_spec = lambda which: pl.BlockSpec((d, tn), lambda i: (0, col0 + B_GROUPS * which + g))
    grid = (t // tm,)
    cast_specs, cast_shapes = _cast_specs(casts, grid)
    out_spec = pl.BlockSpec((1, dil, tm // dil, 3 * tn), lambda i: (i // tiles_per_seq, 0, i % tiles_per_seq, 0))
    qkv, *cast_out = pl.pallas_call(
        _with_casts(functools.partial(_proj_deint_kernel, dil=dil), 4, 1, len(casts)),
        grid=grid,
        in_specs=[pl.BlockSpec((tm, d), lambda i: (i, 0)), w_spec(0), w_spec(1), w_spec(2)] + cast_specs,
        out_specs=[out_spec] + cast_specs,
        out_shape=[jax.ShapeDtypeStruct((bsz, dil, seq_len // dil, 3 * tn), BF16)] + cast_shapes,
        scratch_shapes=[pltpu.VMEM((tn // LANES, tm, LANES), F32)] * {1: 0, FAST_STRIDE ** 2: 6}.get(dil, 3),
        compiler_params=_params("parallel"),
    )(xb, w, w, w, *[w_f32 for w_f32, _ in casts])
    return qkv, cast_out


def _dilated_kernel(q_ref, k_ref, v_ref, o_ref, lse_ref, *token_order_scratch, dil, slopes, n_blocks, n_res):
    qb = 2 * B_SIDE
    kw = 4 * B_SIDE
    n_rows = k_ref.shape[2]
    row = lax.broadcasted_iota(jnp.int32, (qb, kw), 0)
    col = lax.broadcasted_iota(jnp.int32, (qb, kw), 1)

    def attend(r, blk, h, ks, valid, token_dist):
        sl = slice(h * HEAD_DIM, (h + 1) * HEAD_DIM)
        q = q_ref[0, r, blk * qb:(blk + 1) * qb, sl]
        k = k_ref[0, r, pl.ds(ks, kw), sl]
        v = v_ref[0, r, pl.ds(ks, kw), sl]
        s = lax.dot_general(q, k, NT_DIMS, preferred_element_type=F32) - slopes[h] * token_dist
        s = jnp.where(valid, s, -jnp.inf)
        m = jnp.max(s, axis=-1, keepdims=True)
        p = jnp.exp(s - m)
        l = jnp.sum(p, axis=-1, keepdims=True)
        o = jnp.dot(p.astype(BF16), v, preferred_element_type=F32) / l
        if dil == 1:
            o_ref[0, blk * qb:(blk + 1) * qb, sl] = o.astype(o_ref.dtype)
        else:
            o_sc, _ = token_order_scratch
            o_sc[h, pl.ds(blk * qb * dil + r, qb, stride=dil), :] = o
        return m + jnp.log(l)

    lane = lax.broadcasted_iota(jnp.int32, (qb, LANES), 1)

    def residues(it, carry):
        for blk in range(n_blocks):
            q0 = (pl.program_id(1) * n_blocks + blk) * qb
            ks = pl.multiple_of(jnp.clip(q0 - B_SIDE, 0, n_rows - kw), B_SIDE)
            dist = jnp.abs(col - row + (ks - q0))
            valid = dist <= B_SIDE
            token_dist = dist.astype(F32) * float(dil)
            for u in range(n_res):
                r = it * n_res + u
                lse = None
                for h in reversed(range(B_HEADS_PER_GROUP)):
                    col_h = jnp.broadcast_to(attend(r, blk, h, ks, valid, token_dist), (qb, LANES))
                    lse = col_h if lse is None else jnp.where(lane < (h + 1) * LSE_LANES, col_h, lse)
                if dil == 1:
                    lse_ref[0, blk * qb:(blk + 1) * qb, :] = lse
                else:
                    _, l_sc = token_order_scratch
                    l_sc[pl.ds(blk * qb * dil + r, qb, stride=dil), :] = lse
        return carry

    if dil == 1:
        residues(0, 0)
    else:
        o_sc, l_sc = token_order_scratch
        lax.fori_loop(0, dil // n_res, residues, 0)
        for h in range(B_HEADS_PER_GROUP):
            sl = slice(h * HEAD_DIM, (h + 1) * HEAD_DIM)
            o_ref[0, :, sl] = o_sc[h].astype(o_ref.dtype)
        lse_ref[0] = l_sc[...]


DILATED_BATCHING = ((8, 1), (8, 1), (1, 8))


def _dilated_group(qkv, g):
    b, dil, n_rows, _ = qkv.shape
    s = n_rows * dil
    n_blocks, n_res = DILATED_BATCHING[g]
    rows = n_blocks * 2 * B_SIDE
    assert n_rows >= 4 * B_SIDE and n_rows % rows == 0 and dil % n_res == 0
    slopes = tuple(2.0 ** (-8.0 * (g * B_HEADS_PER_GROUP + h + 1) / B_HEADS) for h in range(B_HEADS_PER_GROUP))
    kv_spec = lambda which: pl.BlockSpec((1, dil, n_rows, B_OUT_W), lambda bi, i: (bi, 0, 0, which))
    out_spec = lambda w: pl.BlockSpec((1, rows * dil, w), lambda bi, i: (bi, i, 0))
    staging = [pltpu.VMEM((B_HEADS_PER_GROUP, rows * dil, LANES), F32), pltpu.VMEM((rows * dil, LANES), F32)]
    o, lse = pl.pallas_call(
        functools.partial(_dilated_kernel, dil=dil, slopes=slopes, n_blocks=n_blocks, n_res=n_res),
        grid=(b, n_rows // rows),
        in_specs=[pl.BlockSpec((1, dil, rows, B_OUT_W), lambda bi, i: (bi, 0, i, 0)), kv_spec(1), kv_spec(2)],
        out_specs=[out_spec(B_OUT_W), out_spec(LANES)],
        out_shape=[jax.ShapeDtypeStruct((b, s, B_OUT_W), BF16), jax.ShapeDtypeStruct((b, s, LANES), F32)],
        scratch_shapes=staging if dil > 1 else [],
        compiler_params=_params("parallel", "arbitrary"),
    )(qkv, qkv, qkv)
    return o.reshape(b * s, B_OUT_W), lse.reshape(b * s, LANES)


def _mix_out_kernel(oa_ref, o0_ref, o1_ref, o2_ref, l0_ref, l1_ref, l2_ref, ga_ref, gb_ref, x_ref,
                    wpa_ref, wpb_ref, wo_ref, g_ref, b_ref, out_ref, out_bf16_ref):
    l0, l1, l2 = l0_ref[...], l1_ref[...], l2_ref[...]
    mx = jnp.maximum(jnp.maximum(l0, l1), l2)
    e0, e1, e2 = jnp.exp(l0 - mx), jnp.exp(l1 - mx), jnp.exp(l2 - mx)
    inv = 1.0 / (e0 + e1 + e2)
    weights = (e0 * inv, e1 * inv, e2 * inv)
    rows = l0.shape[0]
    merged = []
    for h in range(B_HEADS_PER_GROUP):
        sl = slice(h * HEAD_DIM, (h + 1) * HEAD_DIM)
        acc = None
        for w, o_ref in zip(weights, (o0_ref, o1_ref, o2_ref)):
            term = jnp.broadcast_to(w[:, h * LSE_LANES:h * LSE_LANES + 1], (rows, HEAD_DIM)) * o_ref[:, sl]
            acc = term if acc is None else acc + term
        merged.append(acc.astype(BF16))
    ob = jnp.concatenate(merged, axis=1)
    pa = jnp.dot(oa_ref[...], wpa_ref[...], preferred_element_type=F32)
    pb = jnp.dot(ob, wpb_ref[...], preferred_element_type=F32)
    mixed = (ga_ref[...].astype(F32) * pa + gb_ref[...].astype(F32) * pb).astype(BF16)
    y = jnp.dot(mixed, wo_ref[...], preferred_element_type=F32)
    x1 = _layer_norm(ALPHA * x_ref[...] + y, g_ref[...], b_ref[...])
    out_ref[...] = x1
    out_bf16_ref[...] = x1.astype(out_bf16_ref.dtype)


def _mix_out(out_a, o_groups, lse_groups, gates, x2d, wpa, wpb, wo, g, b, *, tm):
    t = out_a.shape[0]
    row = lambda w, cb=0: pl.BlockSpec((tm, w), lambda i: (i, cb))
    const = lambda a: pl.BlockSpec(a.shape, lambda i: (0, 0), pipeline_mode=pl.Buffered(1))
    return pl.pallas_call(
        _mix_out_kernel,
        grid=(t // tm,),
        in_specs=[row(A_Q_W)] + [row(B_OUT_W)] * 3 + [row(LANES)] * 3 + [row(D_MODEL, 0), row(D_MODEL, 1), row(D_MODEL),
                  const(wpa), const(wpb), const(wo), const(g), const(b)],
        out_specs=[row(D_MODEL), row(D_MODEL)],
        out_shape=[jax.ShapeDtypeStruct((t, D_MODEL), F32), jax.ShapeDtypeStruct((t, D_MODEL), BF16)],
        compiler_params=_params("parallel"),
    )(out_a, *o_groups, *lse_groups, gates, gates, x2d, wpa, wpb, wo, g, b)


def _ffn_up_kernel(xb_ref, wg_ref, wu_ref, h_ref):
    xb = xb_ref[...]
    gate = jnp.dot(xb, wg_ref[...], preferred_element_type=F32)
    up = jnp.dot(xb, wu_ref[...], preferred_element_type=F32)
    h_ref[...] = (gate * (0.5 * jnp.tanh(0.5 * gate) + 0.5) * up).astype(h_ref.dtype)


def _ffn_down_kernel(h_ref, x_ref, wd_ref, g_ref, b_ref, o_ref):
    y = jnp.dot(h_ref[...], wd_ref[...], preferred_element_type=F32)
    o_ref[...] = _layer_norm(ALPHA * x_ref[...] + y, g_ref[...], b_ref[...])


def _ffn(x1, x1_bf16, wg, wu, wd, g, b, *, tm_up, tf, tm_down):
    t, d = x1.shape
    hidden = wg.shape[1]
    h = pl.pallas_call(
        _ffn_up_kernel,
        grid=(t // tm_up, hidden // tf),
        in_specs=[pl.BlockSpec((tm_up, d), lambda i, f: (i, 0)),
                  pl.BlockSpec((d, tf), lambda i, f: (0, f)),
                  pl.BlockSpec((d, tf), lambda i, f: (0, f))],
        out_specs=pl.BlockSpec((tm_up, tf), lambda i, f: (i, f)),
        out_shape=jax.ShapeDtypeStruct((t, hidden), BF16),
        compiler_params=_params("parallel", "arbitrary"),
    )(x1_bf16, wg, wu)
    const = lambda a: pl.BlockSpec(a.shape, lambda i: (0, 0), pipeline_mode=pl.Buffered(1))
    return pl.pallas_call(
        _ffn_down_kernel,
        grid=(t // tm_down,),
        in_specs=[pl.BlockSpec((tm_down, hidden), lambda i: (i, 0)),
                  pl.BlockSpec((tm_down, d), lambda i: (i, 0)),
                  const(wd), const(g), const(b)],
        out_specs=pl.BlockSpec((tm_down, d), lambda i: (i, 0)),
        out_shape=jax.ShapeDtypeStruct((t, d), F32),
        compiler_params=_params("parallel"),
    )(h, x1, wd, g, b)


def _rope_tables(seq_len):
    pos = np.arange(seq_len)
    n_freq = ROPE_AXIS_DIM // 2
    freqs = (np.float32(ROPE_THETA) ** (-np.arange(n_freq, dtype=np.float32) / np.float32(n_freq))).astype(np.float32)
    ang_r = (pos // GRID_W).astype(np.float32)[:, None] * freqs[None, :]
    ang_c = (pos % GRID_W).astype(np.float32)[:, None] * freqs[None, :]
    cos = np.concatenate([np.cos(ang_r)] * 2 + [np.cos(ang_c)] * 2, axis=-1).astype(np.float32)
    sin = np.concatenate([-np.sin(ang_r), np.sin(ang_r), -np.sin(ang_c), np.sin(ang_c)], axis=-1).astype(np.float32)
    return jnp.asarray(cos), jnp.asarray(sin)


def _layer(x, w_in, b_gate, q_norm_a, k_norm_a, w_proj_a, w_proj_b, w_out,
           ln1_g, ln1_b, w_ffn_gate, w_ffn_up, w_ffn_down, ln2_g, ln2_b, rope):
    bsz, seq_len, d = x.shape
    t = bsz * seq_len
    x2d = x.reshape(t, d)
    vec = lambda a: a.reshape(1, -1).astype(F32)

    gates, xb, w_in_b = _proj_gate(x2d, w_in[:, COL_GA:].astype(BF16), vec(b_gate), [(w_in, COL_GA)], tm=1024, tn=1024)

    qk_gain = jnp.concatenate([jnp.tile(q_norm_a * (Q_SCALE * LOG2_E), A_Q_HEADS), jnp.tile(k_norm_a, A_KV_HEADS)])
    qa, kv = _proj_a(xb, w_in_b, vec(qk_gain), rope, tm=1024, seq_len=seq_len)

    out_a, w_gate_b, w_up_b = _gqa(qa.reshape(bsz, seq_len, A_Q_W), kv.reshape(2 * A_KV_HEADS, bsz, seq_len, HEAD_DIM),
                                   [(w_ffn_gate, FFN_HIDDEN), (w_ffn_up, FFN_HIDDEN)], tq=256, tk=1024)
    casts = [[(w_ffn_down, D_MODEL)], [(w_proj_a, D_MODEL), (w_proj_b, D_MODEL), (w_out, D_MODEL)], []]
    groups, cast_out = [], []
    for g in range(B_GROUPS):
        qkv, converted = _proj_deint(xb, w_in_b, g, casts[g], tm=1024, bsz=bsz, seq_len=seq_len)
        groups.append(_dilated_group(qkv, g))
        cast_out += converted
    w_down_b, w_proj_a_b, w_proj_b_b, w_out_b = cast_out
    x1, x1_bf16 = _mix_out(out_a.reshape(t, A_Q_W), [o for o, _ in groups], [l for _, l in groups], gates, x2d,
                           w_proj_a_b, w_proj_b_b, w_out_b, vec(ln1_g), vec(ln1_b), tm=512)
    x2 = _ffn(x1, x1_bf16, w_gate_b, w_up_b, w_down_b, vec(ln2_g), vec(ln2_b), tm_up=2048, tf=512, tm_down=512)
    return x2.reshape(bsz, seq_len, d)


def kernel(x, w_in, b_gate, q_norm_a, k_norm_a, w_proj_a, w_proj_b, w_out, ln1_g, ln1_b, w_ffn_gate, w_ffn_up, w_ffn_down, ln2_g, ln2_b):
    rope = _rope_tables(x.shape[1])
    for l in range(w_in.shape[0]):
        x = _layer(x, w_in[l], b_gate[l], q_norm_a[l], k_norm_a[l], w_proj_a[l], w_proj_b[l], w_out[l],
                   ln1_g[l], ln1_b[l], w_ffn_gate[l], w_ffn_up[l], w_ffn_down[l], ln2_g[l], ln2_b[l], rope)
    return x
```
